```python
import math
import jax, jax.numpy as jnp
from jax import lax
import numpy as np

D_MODEL = 1024
BATCH = 2
SEQ = 8192
DEPTH = 1
DEC_BATCH = 128
DEC_SEQ = 8
PAST_LEN = 2048
PAGE_SIZE = 128

D_ATTN = D_MODEL // 2
D_HGRN = D_MODEL - D_ATTN
D_MIX = D_ATTN + D_HGRN
A_HEADS = 4
A_DV = D_ATTN // A_HEADS
A_DQK = A_DV // 2
ROT_DIM = A_DQK // 4
ROPE_THETA = 500000.0
H_HEADS = 4
H_DK = D_HGRN // H_HEADS
H_DV = D_HGRN // H_HEADS
HGRN_CHUNK = 64
Q_BLOCK = 128
EPS = 1e-6

kernel_name = "hymba_diffattn_hgrn2_step"


def _split_points():
    sizes = [A_HEADS * 2 * A_DQK, A_HEADS * 2 * A_DQK, A_HEADS * A_DV, D_ATTN,
             H_HEADS * H_DK, H_HEADS * H_DK, H_HEADS * H_DV, D_HGRN]
    return [int(c) for c in np.cumsum(sizes)[:-1]]


def _rms_norm(x, g):
    xf = x.astype(jnp.float32)
    y = xf * lax.rsqrt(jnp.mean(xf * xf, axis=-1, keepdims=True) + EPS)
    return (y * g.astype(jnp.float32)).astype(x.dtype)


def _rope(x, pos):
    inv = ROPE_THETA ** (-jnp.arange(0, ROT_DIM, 2, dtype=jnp.float32) / ROT_DIM)
    ang = pos.astype(jnp.float32)[:, None] * inv[None, :]
    cos = jnp.cos(ang)[None, :, None, None, :]
    sin = jnp.sin(ang)[None, :, None, None, :]
    xf = x.astype(jnp.float32)
    half = ROT_DIM // 2
    x1, x2 = xf[..., :half], xf[..., half:ROT_DIM]
    out = jnp.concatenate([x1 * cos - x2 * sin, x2 * cos + x1 * sin, xf[..., ROT_DIM:]], axis=-1)
    return out.astype(x.dtype)


def _mixer_inputs(x, pos, norm_g, w_in, q_norm_g, k_norm_g, lb):
    B, T, _ = x.shape
    h = _rms_norm(x, norm_g)
    aq, ak, av, ag, hq, hf, hi, hg = jnp.split(h @ w_in, _split_points(), axis=-1)
    aq = _rope(_rms_norm(aq.reshape(B, T, A_HEADS, 2, A_DQK), q_norm_g), pos)
    ak = _rope(_rms_norm(ak.reshape(B, T, A_HEADS, 2, A_DQK), k_norm_g), pos)
    av = av.reshape(B, T, A_HEADS, A_DV)
    hq = jax.nn.silu(hq).reshape(B, T, H_HEADS, H_DK)
    f = lb + (1.0 - lb) * jax.nn.sigmoid(hf.astype(jnp.float32))
    logf = jnp.log(f).reshape(B, T, H_HEADS, H_DK)
    hk = (1.0 - f).reshape(B, T, H_HEADS, H_DK)
    hi = hi.reshape(B, T, H_HEADS, H_DV)
    return aq, ak, av, ag, hq, hk, logf, hi, hg


def _diff_attend(q, k, v, q_pos, k_pos, lam):
    s = jnp.einsum('bqhmd,bshmd->bhmqs', q.astype(jnp.float32), k.astype(jnp.float32)) * (A_DQK ** -0.5)
    mask = k_pos[None, :] <= q_pos[:, None]
    s = jnp.where(mask, s, -jnp.inf)
    p = jax.nn.softmax(s, axis=-1)
    w = p[:, :, 0] - lam * p[:, :, 1]
    return jnp.einsum('bhqs,bshd->bqhd', w, v.astype(jnp.float32))


def _prompt_attend(q, k, v, pos, lam):
    B, T = q.shape[:2]
    nb = T // Q_BLOCK
    qb = q.reshape(B, nb, Q_BLOCK, A_HEADS, 2, A_DQK).swapaxes(0, 1)
    pb = pos.reshape(nb, Q_BLOCK)
    o = lax.map(lambda a: _diff_attend(a[0], k, v, a[1], pos, lam), (qb, pb))
    return o.swapaxes(0, 1).reshape(B, T, A_HEADS, A_DV)


def _hgrn2(q, k, logf, v, S0):
    B, T, H, DK = q.shape
    DV = v.shape[-1]
    C = HGRN_CHUNK if T % HGRN_CHUNK == 0 else T
    n = T // C
    def to_chunks(a):
        return a.astype(jnp.float32).reshape(B, n, C, H, a.shape[-1]).swapaxes(0, 1)
    xs = (to_chunks(q), to_chunks(k), to_chunks(logf), to_chunks(v))
    tri = jnp.tril(jnp.ones((C, C), dtype=bool))[None, :, :, None, None]
    def step(S, inp):
        qc, kc, lfc, vc = inp
        b = jnp.cumsum(lfc, axis=1)
        diff = b[:, :, None] - b[:, None, :]
        decay = jnp.exp(jnp.where(tri, diff, -jnp.inf))
        att = jnp.einsum('bthd,bshd,btshd->bhts', qc, kc, decay)
        o = jnp.einsum('bhts,bshv->bthv', att, vc) + jnp.einsum('bthd,bhdv->bthv', qc * jnp.exp(b), S)
        bC = b[:, -1]
        S_new = jnp.exp(bC)[..., None] * S + jnp.einsum('bshd,bshv->bhdv', kc * jnp.exp(bC[:, None] - b), vc)
        return S_new, o
    S_fin, o = lax.scan(step, S0.astype(jnp.float32), xs)
    return o.swapaxes(0, 1).reshape(B, T, H, DV), S_fin


def _merge(x, o_attn, ag, o_hgrn, hg, subln_g, hgrn_norm_g, lam_init, w_out):
    B, T, _ = x.shape
    a = (_rms_norm(o_attn, subln_g) * (1.0 - lam_init)).reshape(B, T, D_ATTN) * jax.nn.silu(ag.astype(jnp.float32))
    r = _rms_norm(o_hgrn, hgrn_norm_g).reshape(B, T, D_HGRN) * jax.nn.silu(hg.astype(jnp.float32))
    mix = jnp.concatenate([a, r], axis=-1).astype(x.dtype)
    return x + mix @ w_out


def setup_inputs(seed: int = 0) -> dict:
    key = jax.random.key(seed)
    ks = jax.random.split(key, 20)
    n_pages = PAST_LEN // PAGE_SIZE
    n_used = DEC_BATCH * n_pages
    n_pool = n_used + max(1, n_used // 4)
    f32 = jnp.float32
    d_in = 2 * A_HEADS * 2 * A_DQK + A_HEADS * A_DV + D_ATTN + 2 * H_HEADS * H_DK + H_HEADS * H_DV + D_HGRN
    page_table = jax.random.permutation(ks[5], n_pool)[:n_used].reshape(DEC_BATCH, n_pages).astype(jnp.int32)
    return {
        "x_prompt": jax.random.normal(ks[0], (BATCH, SEQ, D_MODEL), f32),
        "x_sample": jax.random.normal(ks[1], (DEC_BATCH, DEC_SEQ, D_MODEL), f32),
        "cache_k": jax.random.normal(ks[2], (DEPTH, n_pool, PAGE_SIZE, A_HEADS, 2 * A_DQK), f32),
        "cache_v": jax.random.normal(ks[3], (DEPTH, n_pool, PAGE_SIZE, A_HEADS, A_DV), f32),
        "state_hgrn": 0.5 * jax.random.normal(ks[4], (DEPTH, DEC_BATCH, H_HEADS, H_DK, H_DV), f32),
        "page_table": page_table,
        "norm_g": 1.0 + 0.02 * jax.random.normal(ks[6], (DEPTH, D_MODEL), f32),
        "w_in": jax.random.normal(ks[7], (DEPTH, D_MODEL, d_in), f32) * D_MODEL ** -0.5,
        "q_norm_g": 1.0 + 0.02 * jax.random.normal(ks[8], (DEPTH, A_DQK), f32),
        "k_norm_g": 1.0 + 0.02 * jax.random.normal(ks[9], (DEPTH, A_DQK), f32),
        "lambda_q1": 0.1 * jax.random.normal(ks[10], (DEPTH, A_DQK), f32),
        "lambda_k1": 0.1 * jax.random.normal(ks[11], (DEPTH, A_DQK), f32),
        "lambda_q2": 0.1 * jax.random.normal(ks[12], (DEPTH, A_DQK), f32),
        "lambda_k2": 0.1 * jax.random.normal(ks[13], (DEPTH, A_DQK), f32),
        "subln_g": 1.0 + 0.02 * jax.random.normal(ks[14], (DEPTH, A_DV), f32),
        "hgrn_lb_logits": 0.1 * jax.random.normal(ks[15], (DEPTH + 1, H_HEADS * H_DK), f32),
        "hgrn_norm_g": 1.0 + 0.02 * jax.random.normal(ks[16], (DEPTH, H_DV), f32),
        "w_out": jax.random.normal(ks[17], (DEPTH, D_MIX, D_MODEL), f32) * D_MIX ** -0.5,
    }


def reference(x_prompt, x_sample, cache_k, cache_v, state_hgrn, page_table, norm_g, w_in, q_norm_g,
              k_norm_g, lambda_q1, lambda_k1, lambda_q2, lambda_k2, subln_g, hgrn_lb_logits,
              hgrn_norm_g, w_out):
    Bp, Tp, _ = x_prompt.shape
    Bs, Ts, _ = x_sample.shape
    past_len = page_table.shape[1] * cache_k.shape[2]
    pos_p = jnp.arange(Tp, dtype=jnp.int32)
    pos_s = past_len + jnp.arange(Ts, dtype=jnp.int32)
    kpos_s = jnp.arange(past_len + Ts, dtype=jnp.int32)
    lb_all = jnp.cumsum(jax.nn.softmax(hgrn_lb_logits.astype(jnp.float32), axis=0), axis=0)
    xp, xs = x_prompt, x_sample
    nk_p, nv_p, ns_p, nk_s, nv_s, ns_s = [], [], [], [], [], []
    for l in range(DEPTH):
        lam_init = 0.8 - 0.6 * math.exp(-0.3 * l)
        lam = (jnp.exp(jnp.sum(lambda_q1[l].astype(jnp.float32) * lambda_k1[l].astype(jnp.float32)))
               - jnp.exp(jnp.sum(lambda_q2[l].astype(jnp.float32) * lambda_k2[l].astype(jnp.float32)))
               + lam_init)
        lb = lb_all[l]
        aq, ak, av, ag, hq, hk, lf, hi, hg = _mixer_inputs(xp, pos_p, norm_g[l], w_in[l], q_norm_g[l], k_norm_g[l], lb)
        o_a = _prompt_attend(aq, ak, av, pos_p, lam)
        S0 = jnp.zeros((Bp, H_HEADS, H_DK, H_DV), jnp.float32)
        o_h, S_p = _hgrn2(hq, hk, lf, hi, S0)
        xp = _merge(xp, o_a, ag, o_h, hg, subln_g[l], hgrn_norm_g[l], lam_init, w_out[l])
        nk_p.append(ak.reshape(Bp, Tp, A_HEADS, 2 * A_DQK))
        nv_p.append(av)
        ns_p.append(S_p)
        saq, sak, sav, sag, shq, shk, slf, shi, shg = _mixer_inputs(xs, pos_s, norm_g[l], w_in[l], q_norm_g[l], k_norm_g[l], lb)
        pk = cache_k[l][page_table].reshape(Bs, past_len, A_HEADS, 2, A_DQK)
        pv = cache_v[l][page_table].reshape(Bs, past_len, A_HEADS, A_DV)
        k_all = jnp.concatenate([pk, sak.astype(pk.dtype)], axis=1)
        v_all = jnp.concatenate([pv, sav.astype(pv.dtype)], axis=1)
        so_a = _diff_attend(saq, k_all, v_all, pos_s, kpos_s, lam)
        so_h, S_s = _hgrn2(shq, shk, slf, shi, state_hgrn[l])
        xs = _merge(xs, so_a, sag, so_h, shg, subln_g[l], hgrn_norm_g[l], lam_init, w_out[l])
        nk_s.append(sak.reshape(Bs, Ts, A_HEADS, 2 * A_DQK))
        nv_s.append(sav)
        ns_s.append(S_s)
    return (xp, xs, jnp.stack(nk_p), jnp.stack(nv_p), jnp.stack(ns_p),
            jnp.stack(nk_s), jnp.stack(nv_s), jnp.stack(ns_s))
```

```python
import functools
import math

import numpy as np
import jax
import jax.numpy as jnp
from jax import lax
from jax.experimental import pallas as pl
from jax.experimental.pallas import tpu as pltpu

F32 = jnp.float32
BF16 = jnp.bfloat16

LANES = 128
A_HEADS = 4
A_DQK = 64
A_DV = 128
H_HEADS = 4
H_DK = 128
H_DV = 128
SEG = 512
ROT_DIM = A_DQK // 4
ROPE_THETA = 500000.0
EPS = 1e-6
LAM_INIT = 0.8 - 0.6 * math.exp(-0.3 * 0)
HGRN_ROWS = 128
VMEM_LIMIT = 48 * 1024 * 1024


def _sigmoid(x):
    return 1.0 / (1.0 + jnp.exp(-x))


def _split_hi_lo(x):
    hi = x.astype(BF16)
    lo = (x - hi.astype(F32)).astype(BF16)
    return hi, lo


def _lam_from_ref(lam_ref):
    lp = lam_ref[...]
    s1 = jnp.sum(lp[0:1] * lp[1:2], axis=1, keepdims=True)
    s2 = jnp.sum(lp[2:3] * lp[3:4], axis=1, keepdims=True)
    return jnp.exp(s1) - jnp.exp(s2) + LAM_INIT


def _inproj_kernel(x_ref, ng_ref, w_ref, qg_ref, kg_ref, cos_ref, sa_ref, sb_ref, lbl_ref, gm_ref,
                   q_ref, k_ref, v_ref, ga_ref, hq_ref, f_ref, hi_ref, gh_ref, *maybe_bf16_refs):
    x = x_ref[...]
    ms = jnp.mean(x * x, axis=-1, keepdims=True)
    h = (x * lax.rsqrt(ms + EPS) * ng_ref[...]).astype(BF16)

    def proj(seg):
        return jnp.dot(h, w_ref[:, seg * SEG:(seg + 1) * SEG], preferred_element_type=F32)

    cos, sa, sb = cos_ref[...], sa_ref[...], sb_ref[...]

    def norm_rope(y, g_ref, hh):
        yh = y[:, hh * LANES:(hh + 1) * LANES]
        hi, lo = _split_hi_lo(yh * yh)
        gms = jnp.dot(jnp.concatenate([hi, lo], axis=1), gm_ref[...], preferred_element_type=F32)
        yn = yh * lax.rsqrt(gms + EPS) * g_ref[...]
        return yn * cos + pltpu.roll(yn, LANES - ROT_DIM // 2, 1) * sa + pltpu.roll(yn, ROT_DIM // 2, 1) * sb

    aq = proj(0)
    for hh in range(A_HEADS):
        q_ref[:, hh * LANES:(hh + 1) * LANES] = norm_rope(aq, qg_ref, hh) * (A_DQK ** -0.5)
    ak = proj(1)
    for hh in range(A_HEADS):
        kh = norm_rope(ak, kg_ref, hh)
        k_ref[:, hh * LANES:(hh + 1) * LANES] = kh
        if maybe_bf16_refs:
            maybe_bf16_refs[0][:, hh * LANES:(hh + 1) * LANES] = kh.astype(BF16)
    av = proj(2)
    v_ref[...] = av
    if maybe_bf16_refs:
        maybe_bf16_refs[1][...] = av.astype(BF16)
    ag = proj(3)
    ga_ref[...] = ag * _sigmoid(ag)
    hq = proj(4)
    hq_ref[...] = hq * _sigmoid(hq)
    hf = proj(5)
    lbl = lbl_ref[...]
    l0, l1 = lbl[0:1], lbl[1:2]
    mx = jnp.maximum(l0, l1)
    e0, e1 = jnp.exp(l0 - mx), jnp.exp(l1 - mx)
    lb = e0 / (e0 + e1)
    f_ref[...] = lb + (1.0 - lb) * _sigmoid(hf)
    hi_ref[...] = proj(6)
    hg = proj(7)
    gh_ref[...] = hg * _sigmoid(hg)


def _rope_tables(pos):
    half = ROT_DIM // 2
    inv = ROPE_THETA ** (-jnp.arange(0, ROT_DIM, 2, dtype=F32) / ROT_DIM)
    ang = pos.astype(F32)[:, None] * inv[None, :]
    cos, sin = jnp.cos(ang), jnp.sin(ang)
    t = pos.shape[0]
    one = jnp.ones((t, A_DQK - ROT_DIM), F32)
    zero = jnp.zeros((t, A_DQK - ROT_DIM), F32)
    zh = jnp.zeros((t, half), F32)
    c = jnp.concatenate([cos, cos, one], axis=1)
    a = jnp.concatenate([-sin, zh, zero], axis=1)
    b = jnp.concatenate([zh, sin, zero], axis=1)
    return tuple(jnp.concatenate([m, m], axis=1) for m in (c, a, b))


def _group_mean_matrix():
    g = np.zeros((LANES, LANES), np.float32)
    for s in range(0, LANES, A_DQK):
        g[s:s + A_DQK, s:s + A_DQK] = 1.0 / A_DQK
    return jnp.asarray(np.concatenate([g, g], axis=0), BF16)


def _inproj(x2d, pos, n_pos_blocks, tm, with_bf16, norm_g, w_in_bf16, q_norm_g, k_norm_g, lb_logits):
    rows, d_model = x2d.shape
    d_in = w_in_bf16.shape[1]
    cos, sa, sb = _rope_tables(pos)
    qg = jnp.concatenate([q_norm_g, q_norm_g]).reshape(1, LANES)
    kg = jnp.concatenate([k_norm_g, k_norm_g]).reshape(1, LANES)
    const = lambda i: (0, 0)
    row_blk = lambda i: (i, 0)
    pos_blk = lambda i: (i % n_pos_blocks, 0)
    out_f32 = jax.ShapeDtypeStruct((rows, SEG), F32)
    out_shape = [out_f32] * 8
    out_specs = [pl.BlockSpec((tm, SEG), row_blk)] * 8
    if with_bf16:
        out_shape += [jax.ShapeDtypeStruct((rows, SEG), BF16)] * 2
        out_specs += [pl.BlockSpec((tm, SEG), row_blk)] * 2
    return pl.pallas_call(
        _inproj_kernel,
        grid=(rows // tm,),
        in_specs=[
            pl.BlockSpec((tm, d_model), row_blk),
            pl.BlockSpec((1, d_model), const),
            pl.BlockSpec((d_model, d_in), const),
            pl.BlockSpec((1, LANES), const),
            pl.BlockSpec((1, LANES), const),
            pl.BlockSpec((tm, LANES), pos_blk),
            pl.BlockSpec((tm, LANES), pos_blk),
            pl.BlockSpec((tm, LANES), pos_blk),
            pl.BlockSpec((2, SEG), const),
            pl.BlockSpec((2 * LANES, LANES), const),
        ],
        out_specs=out_specs,
        out_shape=out_shape,
        compiler_params=pltpu.CompilerParams(dimension_semantics=("parallel",), vmem_limit_bytes=VMEM_LIMIT),
        name="inproj",
    )(x2d, norm_g.reshape(1, d_model), w_in_bf16, qg, kg, cos, sa, sb, lb_logits, _group_mean_matrix())


def _stack_maps(q_bf16):
    lane = lax.broadcasted_iota(jnp.int32, q_bf16.shape, 1)
    zero = jnp.zeros_like(q_bf16)
    return jnp.concatenate([jnp.where(lane < A_DQK, q_bf16, zero), jnp.where(lane >= A_DQK, q_bf16, zero)], axis=0)


def _online_softmax_step(qs, k, v, carry, mask):
    m, l, acc = carry
    s = lax.dot_general(qs, k, (((1,), (1,)), ((), ())), preferred_element_type=F32)
    if mask is not None:
        s = jnp.where(mask, s, -jnp.inf)
    m_new = jnp.maximum(m, jnp.max(s, axis=1, keepdims=True))
    alpha = jnp.exp(m - m_new)
    p = jnp.exp(s - m_new)
    l = alpha * l + jnp.sum(p, axis=1, keepdims=True)
    acc = alpha * acc + jnp.dot(p.astype(BF16), v, preferred_element_type=F32)
    return m_new, l, acc


def _combine_maps(l, acc, lam, n):
    return acc[:n] * (1.0 / l[:n]) - lam * (acc[n:] * (1.0 / l[n:]))


def _pattn_kernel(q_ref, k_ref, v_ref, lam_ref, o_ref, *, blk):
    qi = pl.program_id(2)
    qs = _stack_maps(q_ref[...].astype(BF16))

    def kv(kb):
        start = pl.multiple_of(kb * blk, blk)
        return k_ref[pl.ds(start, blk), :], v_ref[pl.ds(start, blk), :]

    def body(kb, carry):
        k, v = kv(kb)
        return _online_softmax_step(qs, k, v, carry, None)

    init = (jnp.full((2 * blk, 1), -jnp.inf, F32), jnp.zeros((2 * blk, 1), F32), jnp.zeros((2 * blk, A_DV), F32))
    carry = lax.fori_loop(0, qi, body, init)
    row = lax.broadcasted_iota(jnp.int32, (2 * blk, blk), 0)
    col = lax.broadcasted_iota(jnp.int32, (2 * blk, blk), 1)
    tok = jnp.where(row >= blk, row - blk, row)
    k, v = kv(qi)
    _, l, acc = _online_softmax_step(qs, k, v, carry, col <= tok)
    o_ref[...] = _combine_maps(l, acc, _lam_from_ref(lam_ref), blk)


def _prompt_attention(q, kb16, vb16, lam_params, batch, seq, blk):
    nq = seq // blk
    return pl.pallas_call(
        functools.partial(_pattn_kernel, blk=blk),
        grid=(batch, A_HEADS, nq),
        in_specs=[
            pl.BlockSpec((blk, LANES), lambda b, h, i: (b * nq + i, h)),
            pl.BlockSpec((seq, LANES), lambda b, h, i: (b, h)),
            pl.BlockSpec((seq, LANES), lambda b, h, i: (b, h)),
            pl.BlockSpec((4, A_DQK), lambda b, h, i: (0, 0)),
        ],
        out_specs=pl.BlockSpec((blk, LANES), lambda b, h, i: (b * nq + i, h)),
        out_shape=jax.ShapeDtypeStruct((batch * seq, A_HEADS * A_DV), F32),
        compiler_params=pltpu.CompilerParams(
            dimension_semantics=("parallel", "parallel", "arbitrary"), vmem_limit_bytes=VMEM_LIMIT),
        name="prompt_attn",
    )(q, kb16, vb16, lam_params)


def _sattn_kernel(pt_ref, q_ref, kn_ref, vn_ref, lam_ref, *rest, pages_per_step, n_new):
    del pt_ref
    k_refs = rest[:pages_per_step]
    v_refs = rest[pages_per_step:2 * pages_per_step]
    o_ref = rest[2 * pages_per_step]
    m_s, l_s, acc_s = rest[2 * pages_per_step + 1:]
    j = pl.program_id(1)
    q = q_ref[...].astype(BF16)
    qs = [_stack_maps(q[:, h * LANES:(h + 1) * LANES]) for h in range(A_HEADS)]

    def head(ref, h):
        return ref[:, h * LANES:(h + 1) * LANES]

    @pl.when(j == 0)
    def _():
        pad = jnp.zeros((LANES - n_new, LANES), F32)
        row = lax.broadcasted_iota(jnp.int32, (2 * n_new, LANES), 0)
        col = lax.broadcasted_iota(jnp.int32, (2 * n_new, LANES), 1)
        tok = jnp.where(row >= n_new, row - n_new, row)
        mask = col <= tok
        for h in range(A_HEADS):
            k = jnp.concatenate([head(kn_ref, h), pad], axis=0).astype(BF16)
            v = jnp.concatenate([head(vn_ref, h), pad], axis=0).astype(BF16)
            init = (jnp.full((2 * n_new, 1), -jnp.inf, F32), jnp.zeros((2 * n_new, 1), F32),
                    jnp.zeros((2 * n_new, A_DV), F32))
            m, l, acc = _online_softmax_step(qs[h], k, v, init, mask)
            m_s[h] = jnp.broadcast_to(m, (2 * n_new, LANES))
            l_s[h] = jnp.broadcast_to(l, (2 * n_new, LANES))
            acc_s[h] = acc

    for h in range(A_HEADS):
        k = jnp.concatenate([head(r, h).astype(BF16) for r in k_refs], axis=0)
        v = jnp.concatenate([head(r, h).astype(BF16) for r in v_refs], axis=0)
        carry = (m_s[h][:, 0:1], l_s[h][:, 0:1], acc_s[h])
        m, l, acc = _online_softmax_step(qs[h], k, v, carry, None)
        m_s[h] = jnp.broadcast_to(m, (2 * n_new, LANES))
        l_s[h] = jnp.broadcast_to(l, (2 * n_new, LANES))
        acc_s[h] = acc

    @pl.when(j == pl.num_programs(1) - 1)
    def _():
        lam = _lam_from_ref(lam_ref)
        for h in range(A_HEADS):
            o_ref[:, h * LANES:(h + 1) * LANES] = _combine_maps(l_s[h][:, 0:1], acc_s[h], lam, n_new)


def _sample_attention(q, k_new, v_new, cache_k, cache_v, page_table, lam_params, pages_per_step):
    bs, n_pages = page_table.shape
    n_new = q.shape[0] // bs
    page_size = cache_k.shape[1]
    width = cache_k.shape[2]
    steps = n_pages // pages_per_step
    row_blk = lambda b, j, pt: (b, 0)

    def page_spec(i):
        return pl.BlockSpec((None, page_size, width),
                            lambda b, j, pt: (pt[b * n_pages + j * pages_per_step + i], 0, 0))

    grid_spec = pltpu.PrefetchScalarGridSpec(
        num_scalar_prefetch=1,
        grid=(bs, steps),
        in_specs=[
            pl.BlockSpec((n_new, width), row_blk),
            pl.BlockSpec((n_new, width), row_blk),
            pl.BlockSpec((n_new, width), row_blk),
            pl.BlockSpec((4, A_DQK), lambda b, j, pt: (0, 0)),
        ] + [page_spec(i) for i in range(pages_per_step)] * 2,
        out_specs=pl.BlockSpec((n_new, width), row_blk),
        scratch_shapes=[
            pltpu.VMEM((A_HEADS, 2 * n_new, LANES), F32),
            pltpu.VMEM((A_HEADS, 2 * n_new, LANES), F32),
            pltpu.VMEM((A_HEADS, 2 * n_new, A_DV), F32),
        ],
    )
    return pl.pallas_call(
        functools.partial(_sattn_kernel, pages_per_step=pages_per_step, n_new=n_new),
        grid_spec=grid_spec,
        out_shape=jax.ShapeDtypeStruct(q.shape, F32),
        compiler_params=pltpu.CompilerParams(
            dimension_semantics=("parallel", "arbitrary"), vmem_limit_bytes=VMEM_LIMIT),
        name="sample_attn",
    )(page_table.reshape(-1), q, k_new, v_new, lam_params,
      *([cache_k] * pages_per_step), *([cache_v] * pages_per_step))


def _hgrn_kernel(q_ref, f_ref, v_ref, s0_ref, tri_ref, o_ref, s_ref, *, rows):
    c = pl.program_id(1)

    @pl.when(c == 0)
    def _():
        s_ref[...] = s0_ref[...]

    n = HGRN_ROWS
    pad = n - rows
    r_idx = lax.broadcasted_iota(jnp.int32, (n, n), 0)
    c_idx = lax.broadcasted_iota(jnp.int32, (n, n), 1)
    causal = c_idx <= r_idx
    for h in range(H_HEADS):
        sl = slice(h * LANES, (h + 1) * LANES)
        q, f, v = q_ref[:, sl], f_ref[:, sl], v_ref[:, sl]
        if pad:
            q = jnp.concatenate([q, jnp.zeros((pad, LANES), F32)], axis=0)
            f = jnp.concatenate([f, jnp.ones((pad, LANES), F32)], axis=0)
            v = jnp.concatenate([v, jnp.zeros((pad, LANES), F32)], axis=0)
        k = 1.0 - f
        hi, lo = _split_hi_lo(jnp.log(f))
        b = jnp.dot(tri_ref[...], jnp.concatenate([hi, lo], axis=0), preferred_element_type=F32)
        b_end = b[n - 1:n]
        b_mid = b[n // 2 - 1:n // 2]
        state = s_ref[h]
        v16 = v.astype(BF16)
        att = lax.dot_general((q * jnp.exp(b - b_mid)).astype(BF16), (k * jnp.exp(b_mid - b)).astype(BF16),
                              (((1,), (1,)), ((), ())), preferred_element_type=F32)
        att = jnp.where(causal, att, 0.0)
        o = jnp.dot(att.astype(BF16), v16, preferred_element_type=F32)
        o = o + jnp.dot((q * jnp.exp(b)).astype(BF16), state.astype(BF16), preferred_element_type=F32)
        o_ref[:, sl] = o[:rows]
        kv = lax.dot_general((k * jnp.exp(b_end - b)).astype(BF16), v16, (((0,), (0,)), ((), ())),
                             preferred_element_type=F32)
        decay_col = jnp.transpose(jnp.broadcast_to(jnp.exp(b_end), (8, LANES)))[:, 0:1]
        s_ref[h] = decay_col * state + kv


def _hgrn(hq, f, hi, s0, batch, rows):
    total = hq.shape[0]
    nch = total // (batch * rows)
    n = HGRN_ROWS
    tri = np.tril(np.ones((n, n), np.float32))
    tri = jnp.asarray(np.concatenate([tri, tri], axis=1), BF16)
    blk = lambda b, c: (b * nch + c, 0)
    st = lambda b, c: (b, 0, 0, 0)
    return pl.pallas_call(
        functools.partial(_hgrn_kernel, rows=rows),
        grid=(batch, nch),
        in_specs=[
            pl.BlockSpec((rows, SEG), blk),
            pl.BlockSpec((rows, SEG), blk),
            pl.BlockSpec((rows, SEG), blk),
            pl.BlockSpec((None, H_HEADS, H_DK, H_DV), st),
            pl.BlockSpec((n, 2 * n), lambda b, c: (0, 0)),
        ],
        out_specs=[
            pl.BlockSpec((rows, SEG), blk),
            pl.BlockSpec((None, H_HEADS, H_DK, H_DV), st),
        ],
        out_shape=[
            jax.ShapeDtypeStruct((total, SEG), F32),
            jax.ShapeDtypeStruct((batch, H_HEADS, H_DK, H_DV), F32),
        ],
        compiler_params=pltpu.CompilerParams(
            dimension_semantics=("parallel", "arbitrary"), vmem_limit_bytes=VMEM_LIMIT),
        name="hgrn2",
    )(hq, f, hi, s0, tri)


def _merge_kernel(x_ref, oa_ref, ga_ref, oh_ref, gh_ref, sg_ref, hg_ref, w_ref, y_ref):
    def head_norm(o_ref, g_ref, h):
        o = o_ref[:, h * LANES:(h + 1) * LANES]
        return o * lax.rsqrt(jnp.mean(o * o, axis=-1, keepdims=True) + EPS) * g_ref[...]

    a = jnp.concatenate([head_norm(oa_ref, sg_ref, h) for h in range(A_HEADS)], axis=1)
    a = a * (1.0 - LAM_INIT) * ga_ref[...]
    r = jnp.concatenate([head_norm(oh_ref, hg_ref, h) for h in range(H_HEADS)], axis=1)
    r = r * gh_ref[...]
    mix = jnp.concatenate([a, r], axis=1).astype(BF16)
    y_ref[...] = x_ref[...] + jnp.dot(mix, w_ref[...], preferred_element_type=F32)


def _merge(x2d, o_attn, ga, o_hgrn, gh, subln_g, hgrn_norm_g, w_out_bf16, tm):
    rows, d_model = x2d.shape
    row_blk = lambda i: (i, 0)
    const = lambda i: (0, 0)
    return pl.pallas_call(
        _merge_kernel,
        grid=(rows // tm,),
        in_specs=[
            pl.BlockSpec((tm, d_model), row_blk),
            pl.BlockSpec((tm, SEG), row_blk),
            pl.BlockSpec((tm, SEG), row_blk),
            pl.BlockSpec((tm, SEG), row_blk),
            pl.BlockSpec((tm, SEG), row_blk),
            pl.BlockSpec((1, LANES), const),
            pl.BlockSpec((1, LANES), const),
            pl.BlockSpec(w_out_bf16.shape, const),
        ],
        out_specs=pl.BlockSpec((tm, d_model), row_blk),
        out_shape=jax.ShapeDtypeStruct((rows, d_model), F32),
        compiler_params=pltpu.CompilerParams(dimension_semantics=("parallel",), vmem_limit_bytes=VMEM_LIMIT),
        name="merge",
    )(x2d, o_attn, ga, o_hgrn, gh, subln_g.reshape(1, LANES), hgrn_norm_g.reshape(1, LANES), w_out_bf16)


def kernel(x_prompt, x_sample, cache_k, cache_v, state_hgrn, page_table, norm_g, w_in, q_norm_g, k_norm_g,
           lambda_q1, lambda_k1, lambda_q2, lambda_k2, subln_g, hgrn_lb_logits, hgrn_norm_g, w_out):
    bp, tp, d_model = x_prompt.shape
    bs, ts, _ = x_sample.shape
    depth, n_pool, page_size = cache_k.shape[:3]
    assert depth == 1 and hgrn_lb_logits.shape[0] == 2
    assert (A_DV, H_DK, H_DV, 2 * A_DQK) == (LANES,) * 4
    past_len = page_table.shape[1] * page_size
    tm = 256
    attn_blk = 256

    w_in16 = w_in[0].astype(BF16)
    w_out16 = w_out[0].astype(BF16)
    lam_params = jnp.stack([lambda_q1[0], lambda_k1[0], lambda_q2[0], lambda_k2[0]])
    proj = functools.partial(_inproj, norm_g=norm_g[0], w_in_bf16=w_in16, q_norm_g=q_norm_g[0],
                             k_norm_g=k_norm_g[0], lb_logits=hgrn_lb_logits)
    fin = functools.partial(_merge, subln_g=subln_g[0], hgrn_norm_g=hgrn_norm_g[0], w_out_bf16=w_out16, tm=tm)

    xp = x_prompt.reshape(bp * tp, d_model)
    q, k, v, ga, hq, f, hi, gh, k16, v16 = proj(xp, jnp.arange(tp, dtype=jnp.int32), tp // tm, tm, True)
    o_attn = _prompt_attention(q, k16, v16, lam_params, bp, tp, attn_blk)
    o_hgrn, s_p = _hgrn(hq, f, hi, jnp.zeros((bp, H_HEADS, H_DK, H_DV), F32), bp, HGRN_ROWS)
    y_p = fin(xp, o_attn, ga, o_hgrn, gh)

    xs = x_sample.reshape(bs * ts, d_model)
    pos_s = past_len + jnp.tile(jnp.arange(ts, dtype=jnp.int32), tm // ts)
    sq, sk, sv, sga, shq, sf, shi, sgh = proj(xs, pos_s, 1, tm, False)
    ck = cache_k[0].reshape(n_pool, page_size, A_HEADS * 2 * A_DQK)
    cv = cache_v[0].reshape(n_pool, page_size, A_HEADS * A_DV)
    so_attn = _sample_attention(sq, sk, sv, ck, cv, page_table, lam_params, pages_per_step=4)
    so_hgrn, s_s = _hgrn(shq, sf, shi, state_hgrn[0], bs, ts)
    y_s = fin(xs, so_attn, sga, so_hgrn, sgh)

    return (y_p.reshape(bp, tp, d_model), y_s.reshape(bs, ts, d_model),
            k.reshape(1, bp, tp, A_HEADS, 2 * A_DQK), v.reshape(1, bp, tp, A_HEADS, A_DV), s_p[None],
            sk.reshape(1, bs, ts, A_HEADS, 2 * A_DQK), sv.reshape(1, bs, ts, A_HEADS, A_DV), s_s[None])
```

```python
import functools
import math

import numpy as np
import jax
import jax.numpy as jnp
from jax import lax
from jax.experimental import pallas as pl
from jax.experimental.pallas import tpu as pltpu

F32 = jnp.float32
BF16 = jnp.bfloat16

LANES = 128
A_HEADS = 4
A_DQK = 64
A_DV = 128
H_HEADS = 4
H_DK = 128
H_DV = 128
SEG = 512
ROT_DIM = A_DQK // 4
ROPE_THETA = 500000.0
EPS = 1e-6
LAM_INIT = 0.8 - 0.6 * math.exp(-0.3 * 0)
Q_SCALE = A_DQK ** -0.5 * math.log2(math.e)
HGRN_ROWS = 128
VMEM_LIMIT = 48 * 1024 * 1024


def _sigmoid(x):
    return 1.0 / (1.0 + jnp.exp(-x))


def _split_hi_lo(x):
    hi = x.astype(BF16)
    lo = (x - hi.astype(F32)).astype(BF16)
    return hi, lo


def _lam_from_ref(lam_ref):
    lp = lam_ref[...]
    s1 = jnp.sum(lp[0:1] * lp[1:2], axis=1, keepdims=True)
    s2 = jnp.sum(lp[2:3] * lp[3:4], axis=1, keepdims=True)
    return jnp.exp(s1) - jnp.exp(s2) + LAM_INIT


def _inproj_kernel(x_ref, ng_ref, w_ref, qg_ref, kg_ref, cos_ref, sa_ref, sb_ref, lbl_ref, gm_ref,
                   q_ref, k_ref, v_ref, ga_ref, hq_ref, f_ref, hi_ref, gh_ref, *maybe_bf16_refs):
    x = x_ref[...]
    ms = jnp.mean(x * x, axis=-1, keepdims=True)
    h = (x * lax.rsqrt(ms + EPS) * ng_ref[...]).astype(BF16)

    def proj(seg):
        return jnp.dot(h, w_ref[:, seg * SEG:(seg + 1) * SEG], preferred_element_type=F32)

    cos, sa, sb = cos_ref[...], sa_ref[...], sb_ref[...]

    def norm_rope(y, g_ref, hh):
        yh = y[:, hh * LANES:(hh + 1) * LANES]
        hi, lo = _split_hi_lo(yh * yh)
        gms = jnp.dot(jnp.concatenate([hi, lo], axis=1), gm_ref[...], preferred_element_type=F32)
        yn = yh * lax.rsqrt(gms + EPS) * g_ref[...]
        return yn * cos + pltpu.roll(yn, LANES - ROT_DIM // 2, 1) * sa + pltpu.roll(yn, ROT_DIM // 2, 1) * sb

    tm = x.shape[0]
    aq = proj(0)
    for hh in range(A_HEADS):
        q_ref[:, hh * LANES:(hh + 1) * LANES] = norm_rope(aq, qg_ref, hh) * Q_SCALE
    ak = proj(1)
    for hh in range(A_HEADS):
        kh = norm_rope(ak, kg_ref, hh)
        k_ref[pl.ds(hh, tm, stride=A_HEADS), :] = kh
        if maybe_bf16_refs:
            maybe_bf16_refs[0][:, hh * LANES:(hh + 1) * LANES] = kh.astype(BF16)
    av = proj(2)
    for hh in range(A_HEADS):
        v_ref[pl.ds(hh, tm, stride=A_HEADS), :] = av[:, hh * LANES:(hh + 1) * LANES]
    if maybe_bf16_refs:
        maybe_bf16_refs[1][...] = av.astype(BF16)
    ag = proj(3)
    ga_ref[...] = ag * _sigmoid(ag)
    hq = proj(4)
    hq_ref[...] = hq * _sigmoid(hq)
    hf = proj(5)
    lbl = lbl_ref[...]
    l0, l1 = lbl[0:1], lbl[1:2]
    mx = jnp.maximum(l0, l1)
    e0, e1 = jnp.exp(l0 - mx), jnp.exp(l1 - mx)
    lb = e0 / (e0 + e1)
    f_ref[...] = lb + (1.0 - lb) * _sigmoid(hf)
    hi_ref[...] = proj(6)
    hg = proj(7)
    gh_ref[...] = hg * _sigmoid(hg)


def _rope_tables(pos):
    half = ROT_DIM // 2
    inv = ROPE_THETA ** (-jnp.arange(0, ROT_DIM, 2, dtype=F32) / ROT_DIM)
    ang = pos.astype(F32)[:, None] * inv[None, :]
    cos, sin = jnp.cos(ang), jnp.sin(ang)
    t = pos.shape[0]
    one = jnp.ones((t, A_DQK - ROT_DIM), F32)
    zero = jnp.zeros((t, A_DQK - ROT_DIM), F32)
    zh = jnp.zeros((t, half), F32)
    c = jnp.concatenate([cos, cos, one], axis=1)
    a = jnp.concatenate([-sin, zh, zero], axis=1)
    b = jnp.concatenate([zh, sin, zero], axis=1)
    return tuple(jnp.concatenate([m, m], axis=1) for m in (c, a, b))


def _group_mean_matrix():
    g = np.zeros((LANES, LANES), np.float32)
    for s in range(0, LANES, A_DQK):
        g[s:s + A_DQK, s:s + A_DQK] = 1.0 / A_DQK
    return jnp.asarray(np.concatenate([g, g], axis=0), BF16)


def _inproj(x2d, pos, n_pos_blocks, tm, with_bf16, norm_g, w_in_bf16, q_norm_g, k_norm_g, lb_logits):
    rows, d_model = x2d.shape
    d_in = w_in_bf16.shape[1]
    cos, sa, sb = _rope_tables(pos)
    qg = jnp.concatenate([q_norm_g, q_norm_g]).reshape(1, LANES)
    kg = jnp.concatenate([k_norm_g, k_norm_g]).reshape(1, LANES)
    const = lambda i: (0, 0)
    row_blk = lambda i: (i, 0)
    pos_blk = lambda i: (i % n_pos_blocks, 0)
    out_f32 = jax.ShapeDtypeStruct((rows, SEG), F32)
    out_shape = [out_f32] * 8
    out_specs = [pl.BlockSpec((tm, SEG), row_blk)] * 8
    for i in (1, 2):
        out_shape[i] = jax.ShapeDtypeStruct((rows * A_HEADS, LANES), F32)
        out_specs[i] = pl.BlockSpec((tm * A_HEADS, LANES), row_blk)
    if with_bf16:
        out_shape += [jax.ShapeDtypeStruct((rows, SEG), BF16)] * 2
        out_specs += [pl.BlockSpec((tm, SEG), row_blk)] * 2
    return pl.pallas_call(
        _inproj_kernel,
        grid=(rows // tm,),
        in_specs=[
            pl.BlockSpec((tm, d_model), row_blk),
            pl.BlockSpec((1, d_model), const),
            pl.BlockSpec((d_model, d_in), const),
            pl.BlockSpec((1, LANES), const),
            pl.BlockSpec((1, LANES), const),
            pl.BlockSpec((tm, LANES), pos_blk),
            pl.BlockSpec((tm, LANES), pos_blk),
            pl.BlockSpec((tm, LANES), pos_blk),
            pl.BlockSpec((2, SEG), const),
            pl.BlockSpec((2 * LANES, LANES), const),
        ],
        out_specs=out_specs,
        out_shape=out_shape,
        compiler_params=pltpu.CompilerParams(dimension_semantics=("parallel",), vmem_limit_bytes=VMEM_LIMIT),
        name="inproj",
    )(x2d, norm_g.reshape(1, d_model), w_in_bf16, qg, kg, cos, sa, sb, lb_logits, _group_mean_matrix())


def _stack_maps(q_bf16):
    lane = lax.broadcasted_iota(jnp.int32, q_bf16.shape, 1)
    zero = jnp.zeros_like(q_bf16)
    return jnp.concatenate([jnp.where(lane < A_DQK, q_bf16, zero), jnp.where(lane >= A_DQK, q_bf16, zero)], axis=0)


def _softmax_block_update(qs, k, v1, mask, m_ref, l_ref, acc_ref):
    width = k.shape[0]
    s = lax.dot_general(qs, k, (((1,), (1,)), ((), ())), preferred_element_type=F32)
    if mask is not None:
        s = jnp.where(mask, s, -jnp.inf)
    m_prev = m_ref[...]
    m_new = jnp.maximum(m_prev, jnp.max(s, axis=1, keepdims=True))
    alpha = jnp.exp2(m_prev - m_new)
    p = jnp.exp2(s - pltpu.repeat(m_new, width // LANES, axis=1))
    pv = jnp.dot(p.astype(BF16), v1, preferred_element_type=F32)
    acc_ref[...] = alpha * acc_ref[...] + pv[:, :A_DV]
    l_ref[...] = alpha * l_ref[...] + pv[:, A_DV:]
    m_ref[...] = m_new


def _reset_softmax_state(m_ref, l_ref, acc_ref):
    m_ref[...] = jnp.full(m_ref.shape, -jnp.inf, F32)
    l_ref[...] = jnp.zeros(l_ref.shape, F32)
    acc_ref[...] = jnp.zeros(acc_ref.shape, F32)


def _combine_maps(l0, acc0, l1, acc1, lam):
    return acc0 * (1.0 / l0) - lam * (acc1 * (1.0 / l1))


def _pattn_kernel(q_ref, k_ref, v_ref, lam_ref, o_ref, v1_ref, m_ref, l_ref, acc_ref, *, blk):
    qi = pl.program_id(2)

    @pl.when(qi == 0)
    def _():
        v1_ref[:, :A_DV] = v_ref[...]
        v1_ref[:, A_DV:] = jnp.ones((v1_ref.shape[0], LANES), BF16)

    qs = _stack_maps(q_ref[...].astype(BF16))
    _reset_softmax_state(m_ref, l_ref, acc_ref)

    def block(kb, mask):
        start = pl.multiple_of(kb * blk, blk)
        _softmax_block_update(qs, k_ref[pl.ds(start, blk), :], v1_ref[pl.ds(start, blk), :], mask,
                              m_ref, l_ref, acc_ref)

    def body(kb, carry):
        block(kb, None)
        return carry

    lax.fori_loop(0, qi, body, 0)
    row = lax.broadcasted_iota(jnp.int32, (2 * blk, blk), 0)
    col = lax.broadcasted_iota(jnp.int32, (2 * blk, blk), 1)
    block(qi, col <= jnp.where(row >= blk, row - blk, row))
    o_ref[...] = _combine_maps(l_ref[:blk], acc_ref[:blk], l_ref[blk:], acc_ref[blk:], _lam_from_ref(lam_ref))


def _prompt_attention(q, kb16, vb16, lam_params, batch, seq, blk):
    nq = seq // blk
    return pl.pallas_call(
        functools.partial(_pattn_kernel, blk=blk),
        grid=(batch, A_HEADS, nq),
        in_specs=[
            pl.BlockSpec((blk, LANES), lambda b, h, i: (b * nq + i, h)),
            pl.BlockSpec((seq, LANES), lambda b, h, i: (b, h)),
            pl.BlockSpec((seq, LANES), lambda b, h, i: (b, h)),
            pl.BlockSpec((4, A_DQK), lambda b, h, i: (0, 0)),
        ],
        out_specs=pl.BlockSpec((blk, LANES), lambda b, h, i: (b * nq + i, h)),
        out_shape=jax.ShapeDtypeStruct((batch * seq, A_HEADS * A_DV), F32),
        scratch_shapes=[
            pltpu.VMEM((seq, 2 * LANES), BF16),
            pltpu.VMEM((2 * blk, LANES), F32),
            pltpu.VMEM((2 * blk, LANES), F32),
            pltpu.VMEM((2 * blk, A_DV), F32),
        ],
        compiler_params=pltpu.CompilerParams(
            dimension_semantics=("parallel", "parallel", "arbitrary"), vmem_limit_bytes=VMEM_LIMIT),
        name="prompt_attn",
    )(q, kb16, vb16, lam_params)


def _sattn_kernel(pt_ref, q_ref, kn_ref, vn_ref, lam_ref, *rest, pages_per_step, n_new):
    del pt_ref
    k_refs = rest[:pages_per_step]
    v_refs = rest[pages_per_step:2 * pages_per_step]
    o_ref = rest[2 * pages_per_step]
    m_ref, l_ref, acc_ref = rest[2 * pages_per_step + 1:]
    j = pl.program_id(1)
    q = q_ref[...].astype(BF16)
    qs = jnp.concatenate([_stack_maps(q[:, h * LANES:(h + 1) * LANES]) for h in range(A_HEADS)], axis=0)
    n_rows = 2 * n_new * A_HEADS
    ones = jnp.ones((LANES, LANES), BF16)

    def same_head(width):
        row = lax.broadcasted_iota(jnp.int32, (n_rows, width), 0)
        col = lax.broadcasted_iota(jnp.int32, (n_rows, width), 1)
        return row, col, (col % A_HEADS) == (row // (2 * n_new))

    @pl.when(j == 0)
    def _():
        _reset_softmax_state(m_ref, l_ref, acc_ref)
        n_kv = n_new * A_HEADS
        pad = jnp.zeros((LANES - n_kv, LANES), F32)
        k = jnp.concatenate([kn_ref[...], pad], axis=0).astype(BF16)
        v = jnp.concatenate([vn_ref[...], pad], axis=0).astype(BF16)
        row, col, ok = same_head(LANES)
        mask = ok & ((col // A_HEADS) <= (row % n_new))
        _softmax_block_update(qs, k, jnp.concatenate([v, ones], axis=1), mask, m_ref, l_ref, acc_ref)

    k = jnp.concatenate([r[...] for r in k_refs], axis=0).astype(BF16)
    v = jnp.concatenate([r[...] for r in v_refs], axis=0).astype(BF16)
    v1 = jnp.concatenate([v, jnp.ones(v.shape, BF16)], axis=1)
    _softmax_block_update(qs, k, v1, same_head(k.shape[0])[2], m_ref, l_ref, acc_ref)

    @pl.when(j == pl.num_programs(1) - 1)
    def _():
        lam = _lam_from_ref(lam_ref)
        for h in range(A_HEADS):
            r0 = h * 2 * n_new
            r1 = r0 + n_new
            o_ref[:, h * LANES:(h + 1) * LANES] = _combine_maps(
                l_ref[r0:r1], acc_ref[r0:r1], l_ref[r1:r1 + n_new], acc_ref[r1:r1 + n_new], lam)


def _sample_attention(q, k_new, v_new, cache_k, cache_v, page_table, lam_params, pages_per_step):
    bs, n_pages = page_table.shape
    n_new = q.shape[0] // bs
    rows_per_page = cache_k.shape[1]
    steps = n_pages // pages_per_step
    n_rows = 2 * n_new * A_HEADS
    row_blk = lambda b, j, pt: (b, 0)

    def page_spec(i):
        return pl.BlockSpec((None, rows_per_page, LANES),
                            lambda b, j, pt: (pt[b * n_pages + j * pages_per_step + i], 0, 0))

    grid_spec = pltpu.PrefetchScalarGridSpec(
        num_scalar_prefetch=1,
        grid=(bs, steps),
        in_specs=[
            pl.BlockSpec((n_new, q.shape[1]), row_blk),
            pl.BlockSpec((n_new * A_HEADS, LANES), row_blk),
            pl.BlockSpec((n_new * A_HEADS, LANES), row_blk),
            pl.BlockSpec((4, A_DQK), lambda b, j, pt: (0, 0)),
        ] + [page_spec(i) for i in range(pages_per_step)] * 2,
        out_specs=pl.BlockSpec((n_new, q.shape[1]), row_blk),
        scratch_shapes=[
            pltpu.VMEM((n_rows, LANES), F32),
            pltpu.VMEM((n_rows, LANES), F32),
            pltpu.VMEM((n_rows, A_DV), F32),
        ],
    )
    return pl.pallas_call(
        functools.partial(_sattn_kernel, pages_per_step=pages_per_step, n_new=n_new),
        grid_spec=grid_spec,
        out_shape=jax.ShapeDtypeStruct(q.shape, F32),
        compiler_params=pltpu.CompilerParams(
            dimension_semantics=("parallel", "arbitrary"), vmem_limit_bytes=VMEM_LIMIT),
        name="sample_attn",
    )(page_table.reshape(-1), q, k_new, v_new, lam_params,
      *([cache_k] * pages_per_step), *([cache_v] * pages_per_step))


def _hgrn_kernel(q_ref, f_ref, v_ref, s0_ref, tri_ref, o_ref, s_ref, *, rows):
    c = pl.program_id(1)

    @pl.when(c == 0)
    def _():
        s_ref[...] = s0_ref[...]

    n = HGRN_ROWS
    pad = n - rows
    r_idx = lax.broadcasted_iota(jnp.int32, (n, n), 0)
    c_idx = lax.broadcasted_iota(jnp.int32, (n, n), 1)
    causal = c_idx <= r_idx
    for h in range(H_HEADS):
        sl = slice(h * LANES, (h + 1) * LANES)
        q, f, v = q_ref[:, sl], f_ref[:, sl], v_ref[:, sl]
        if pad:
            q = jnp.concatenate([q, jnp.zeros((pad, LANES), F32)], axis=0)
            f = jnp.concatenate([f, jnp.ones((pad, LANES), F32)], axis=0)
            v = jnp.concatenate([v, jnp.zeros((pad, LANES), F32)], axis=0)
        k = 1.0 - f
        hi, lo = _split_hi_lo(jnp.log(f))
        b = jnp.dot(tri_ref[...], jnp.concatenate([hi, lo], axis=0), preferred_element_type=F32)
        b_end = b[n - 1:n]
        b_mid = b[n // 2 - 1:n // 2]
        state = s_ref[h]
        v16 = v.astype(BF16)
        att = lax.dot_general((q * jnp.exp(b - b_mid)).astype(BF16), (k * jnp.exp(b_mid - b)).astype(BF16),
                              (((1,), (1,)), ((), ())), preferred_element_type=F32)
        att = jnp.where(causal, att, 0.0)
        o = jnp.dot(att.astype(BF16), v16, preferred_element_type=F32)
        o = o + jnp.dot((q * jnp.exp(b)).astype(BF16), state.astype(BF16), preferred_element_type=F32)
        o_ref[:, sl] = o[:rows]
        kv = lax.dot_general((k * jnp.exp(b_end - b)).astype(BF16), v16, (((0,), (0,)), ((), ())),
                             preferred_element_type=F32)
        decay_col = jnp.transpose(jnp.broadcast_to(jnp.exp(b_end), (8, LANES)))[:, 0:1]
        s_ref[h] = decay_col * state + kv


def _hgrn(hq, f, hi, s0, batch, rows):
    total = hq.shape[0]
    nch = total // (batch * rows)
    n = HGRN_ROWS
    tri = np.tril(np.ones((n, n), np.float32))
    tri = jnp.asarray(np.concatenate([tri, tri], axis=1), BF16)
    blk = lambda b, c: (b * nch + c, 0)
    st = lambda b, c: (b, 0, 0, 0)
    return pl.pallas_call(
        functools.partial(_hgrn_kernel, rows=rows),
        grid=(batch, nch),
        in_specs=[
            pl.BlockSpec((rows, SEG), blk),
            pl.BlockSpec((rows, SEG), blk),
            pl.BlockSpec((rows, SEG), blk),
            pl.BlockSpec((None, H_HEADS, H_DK, H_DV), st),
            pl.BlockSpec((n, 2 * n), lambda b, c: (0, 0)),
        ],
        out_specs=[
            pl.BlockSpec((rows, SEG), blk),
            pl.BlockSpec((None, H_HEADS, H_DK, H_DV), st),
        ],
        out_shape=[
            jax.ShapeDtypeStruct((total, SEG), F32),
            jax.ShapeDtypeStruct((batch, H_HEADS, H_DK, H_DV), F32),
        ],
        compiler_params=pltpu.CompilerParams(
            dimension_semantics=("parallel", "arbitrary"), vmem_limit_bytes=VMEM_LIMIT),
        name="hgrn2",
    )(hq, f, hi, s0, tri)


def _merge_kernel(x_ref, oa_ref, ga_ref, oh_ref, gh_ref, sg_ref, hg_ref, w_ref, y_ref):
    def head_norm(o_ref, g_ref, h):
        o = o_ref[:, h * LANES:(h + 1) * LANES]
        return o * lax.rsqrt(jnp.mean(o * o, axis=-1, keepdims=True) + EPS) * g_ref[...]

    a = jnp.concatenate([head_norm(oa_ref, sg_ref, h) for h in range(A_HEADS)], axis=1)
    a = a * (1.0 - LAM_INIT) * ga_ref[...]
    r = jnp.concatenate([head_norm(oh_ref, hg_ref, h) for h in range(H_HEADS)], axis=1)
    r = r * gh_ref[...]
    mix = jnp.concatenate([a, r], axis=1).astype(BF16)
    y_ref[...] = x_ref[...] + jnp.dot(mix, w_ref[...], preferred_element_type=F32)


def _merge(x2d, o_attn, ga, o_hgrn, gh, subln_g, hgrn_norm_g, w_out_bf16, tm):
    rows, d_model = x2d.shape
    row_blk = lambda i: (i, 0)
    const = lambda i: (0, 0)
    return pl.pallas_call(
        _merge_kernel,
        grid=(rows // tm,),
        in_specs=[
            pl.BlockSpec((tm, d_model), row_blk),
            pl.BlockSpec((tm, SEG), row_blk),
            pl.BlockSpec((tm, SEG), row_blk),
            pl.BlockSpec((tm, SEG), row_blk),
            pl.BlockSpec((tm, SEG), row_blk),
            pl.BlockSpec((1, LANES), const),
            pl.BlockSpec((1, LANES), const),
            pl.BlockSpec(w_out_bf16.shape, const),
        ],
        out_specs=pl.BlockSpec((tm, d_model), row_blk),
        out_shape=jax.ShapeDtypeStruct((rows, d_model), F32),
        compiler_params=pltpu.CompilerParams(dimension_semantics=("parallel",), vmem_limit_bytes=VMEM_LIMIT),
        name="merge",
    )(x2d, o_attn, ga, o_hgrn, gh, subln_g.reshape(1, LANES), hgrn_norm_g.reshape(1, LANES), w_out_bf16)


def kernel(x_prompt, x_sample, cache_k, cache_v, state_hgrn, page_table, norm_g, w_in, q_norm_g, k_norm_g,
           lambda_q1, lambda_k1, lambda_q2, lambda_k2, subln_g, hgrn_lb_logits, hgrn_norm_g, w_out):
    bp, tp, d_model = x_prompt.shape
    bs, ts, _ = x_sample.shape
    depth, n_pool, page_size = cache_k.shape[:3]
    assert depth == 1 and hgrn_lb_logits.shape[0] == 2
    assert (A_DV, H_DK, H_DV, 2 * A_DQK) == (LANES,) * 4
    past_len = page_table.shape[1] * page_size
    tm = 256
    attn_blk = 512

    w_in16 = w_in[0].astype(BF16)
    w_out16 = w_out[0].astype(BF16)
    lam_params = jnp.stack([lambda_q1[0], lambda_k1[0], lambda_q2[0], lambda_k2[0]])
    proj = functools.partial(_inproj, norm_g=norm_g[0], w_in_bf16=w_in16, q_norm_g=q_norm_g[0],
                             k_norm_g=k_norm_g[0], lb_logits=hgrn_lb_logits)
    fin = functools.partial(_merge, subln_g=subln_g[0], hgrn_norm_g=hgrn_norm_g[0], w_out_bf16=w_out16, tm=tm)

    xp = x_prompt.reshape(bp * tp, d_model)
    q, k, v, ga, hq, f, hi, gh, k16, v16 = proj(xp, jnp.arange(tp, dtype=jnp.int32), tp // tm, tm, True)
    o_attn = _prompt_attention(q, k16, v16, lam_params, bp, tp, attn_blk)
    o_hgrn, s_p = _hgrn(hq, f, hi, jnp.zeros((bp, H_HEADS, H_DK, H_DV), F32), bp, HGRN_ROWS)
    y_p = fin(xp, o_attn, ga, o_hgrn, gh)

    xs = x_sample.reshape(bs * ts, d_model)
    pos_s = past_len + jnp.tile(jnp.arange(ts, dtype=jnp.int32), tm // ts)
    sq, sk, sv, sga, shq, sf, shi, sgh = proj(xs, pos_s, 1, tm, False)
    ck = cache_k.reshape(n_pool, page_size * A_HEADS, 2 * A_DQK)
    cv = cache_v.reshape(n_pool, page_size * A_HEADS, A_DV)
    so_attn = _sample_attention(sq, sk, sv, ck, cv, page_table, lam_params, pages_per_step=4)
    so_hgrn, s_s = _hgrn(shq, sf, shi, state_hgrn[0], bs, ts)
    y_s = fin(xs, so_attn, sga, so_hgrn, sgh)

    return (y_p.reshape(bp, tp, d_model), y_s.reshape(bs, ts, d_model),
            k.reshape(1, bp, tp, A_HEADS, 2 * A_DQK), v.reshape(1, bp, tp, A_HEADS, A_DV), s_p[None],
            sk.reshape(1, bs, ts, A_HEADS, 2 * A_DQK), sv.reshape(1, bs, ts, A_HEADS, A_DV), s_s[None])
```

```python
import functools
import math

import numpy as np
import jax
import jax.numpy as jnp
from jax import lax
from jax.experimental import pallas as pl
from jax.experimental.pallas import tpu as pltpu

F32 = jnp.float32
BF16 = jnp.bfloat16

LANES = 128
A_HEADS = 4
A_DQK = 64
A_DV = 128
H_HEADS = 4
H_DK = 128
H_DV = 128
SEG = 512
ROT_DIM = A_DQK // 4
ROPE_THETA = 500000.0
EPS = 1e-6
LAM_INIT = 0.8 - 0.6 * math.exp(-0.3 * 0)
Q_SCALE = A_DQK ** -0.5 * math.log2(math.e)
HGRN_ROWS = 128
VMEM_LIMIT = 48 * 1024 * 1024


def _sigmoid(x):
    return 1.0 / (1.0 + jnp.exp(-x))


def _split_hi_lo(x):
    hi = x.astype(BF16)
    lo = (x - hi.astype(F32)).astype(BF16)
    return hi, lo


def _lam_from_ref(lam_ref):
    lp = lam_ref[...]
    s1 = jnp.sum(lp[0:1] * lp[1:2], axis=1, keepdims=True)
    s2 = jnp.sum(lp[2:3] * lp[3:4], axis=1, keepdims=True)
    return jnp.exp(s1) - jnp.exp(s2) + LAM_INIT


def _inproj_kernel(x_ref, ng_ref, w_ref, qg_ref, kg_ref, cos_ref, sa_ref, sb_ref, lbl_ref, gm_ref,
                   q_ref, k_ref, v_ref, ga_ref, hq_ref, f_ref, hi_ref, gh_ref, *maybe_bf16_refs):
    x = x_ref[...]
    ms = jnp.mean(x * x, axis=-1, keepdims=True)
    h = (x * lax.rsqrt(ms + EPS) * ng_ref[...]).astype(BF16)

    def proj(seg):
        return jnp.dot(h, w_ref[:, seg * SEG:(seg + 1) * SEG], preferred_element_type=F32)

    cos, sa, sb = cos_ref[...], sa_ref[...], sb_ref[...]

    def norm_rope(y, g_ref, hh):
        yh = y[:, hh * LANES:(hh + 1) * LANES]
        hi, lo = _split_hi_lo(yh * yh)
        gms = jnp.dot(jnp.concatenate([hi, lo], axis=1), gm_ref[...], preferred_element_type=F32)
        yn = yh * lax.rsqrt(gms + EPS) * g_ref[...]
        return yn * cos + pltpu.roll(yn, LANES - ROT_DIM // 2, 1) * sa + pltpu.roll(yn, ROT_DIM // 2, 1) * sb

    tm = x.shape[0]
    aq = proj(0)
    for hh in range(A_HEADS):
        q_ref[:, hh * LANES:(hh + 1) * LANES] = norm_rope(aq, qg_ref, hh) * Q_SCALE
    ak = proj(1)
    for hh in range(A_HEADS):
        kh = norm_rope(ak, kg_ref, hh)
        k_ref[pl.ds(hh, tm, stride=A_HEADS), :] = kh
        if maybe_bf16_refs:
            maybe_bf16_refs[0][:, hh * LANES:(hh + 1) * LANES] = kh.astype(BF16)
    av = proj(2)
    for hh in range(A_HEADS):
        v_ref[pl.ds(hh, tm, stride=A_HEADS), :] = av[:, hh * LANES:(hh + 1) * LANES]
    if maybe_bf16_refs:
        maybe_bf16_refs[1][...] = av.astype(BF16)
    ag = proj(3)
    ga_ref[...] = ag * _sigmoid(ag)
    hq = proj(4)
    hq_ref[...] = hq * _sigmoid(hq)
    hf = proj(5)
    lbl = lbl_ref[...]
    l0, l1 = lbl[0:1], lbl[1:2]
    mx = jnp.maximum(l0, l1)
    e0, e1 = jnp.exp(l0 - mx), jnp.exp(l1 - mx)
    lb = e0 / (e0 + e1)
    f_ref[...] = lb + (1.0 - lb) * _sigmoid(hf)
    hi_ref[...] = proj(6)
    hg = proj(7)
    gh_ref[...] = hg * _sigmoid(hg)


def _rope_tables(pos):
    half = ROT_DIM // 2
    inv = ROPE_THETA ** (-jnp.arange(0, ROT_DIM, 2, dtype=F32) / ROT_DIM)
    lane = np.arange(LANES) % A_DQK
    ang = pos.astype(F32)[:, None] * inv[lane % half][None, :]
    cos = jnp.where((lane < ROT_DIM)[None, :], jnp.cos(ang), 1.0)
    sin = jnp.sin(ang)
    a = jnp.where((lane < half)[None, :], -sin, 0.0)
    b = jnp.where(((lane >= half) & (lane < ROT_DIM))[None, :], sin, 0.0)
    return cos, a, b


def _group_mean_matrix():
    g = np.zeros((LANES, LANES), np.float32)
    for s in range(0, LANES, A_DQK):
        g[s:s + A_DQK, s:s + A_DQK] = 1.0 / A_DQK
    return jnp.asarray(np.concatenate([g, g], axis=0), BF16)


def _inproj(x2d, pos, n_pos_blocks, tm, with_bf16, norm_g, w_in_bf16, q_norm_g, k_norm_g, lb_logits):
    rows, d_model = x2d.shape
    d_in = w_in_bf16.shape[1]
    cos, sa, sb = _rope_tables(pos)
    qg = jnp.concatenate([q_norm_g, q_norm_g]).reshape(1, LANES)
    kg = jnp.concatenate([k_norm_g, k_norm_g]).reshape(1, LANES)
    const = lambda i: (0, 0)
    row_blk = lambda i: (i, 0)
    pos_blk = lambda i: (i % n_pos_blocks, 0)
    out_f32 = jax.ShapeDtypeStruct((rows, SEG), F32)
    out_shape = [out_f32] * 8
    out_specs = [pl.BlockSpec((tm, SEG), row_blk)] * 8
    for i in (1, 2):
        out_shape[i] = jax.ShapeDtypeStruct((rows * A_HEADS, LANES), F32)
        out_specs[i] = pl.BlockSpec((tm * A_HEADS, LANES), row_blk)
    if with_bf16:
        out_shape += [jax.ShapeDtypeStruct((rows, SEG), BF16)] * 2
        out_specs += [pl.BlockSpec((tm, SEG), row_blk)] * 2
    return pl.pallas_call(
        _inproj_kernel,
        grid=(rows // tm,),
        in_specs=[
            pl.BlockSpec((tm, d_model), row_blk),
            pl.BlockSpec((1, d_model), const),
            pl.BlockSpec((d_model, d_in), const),
            pl.BlockSpec((1, LANES), const),
            pl.BlockSpec((1, LANES), const),
            pl.BlockSpec((tm, LANES), pos_blk),
            pl.BlockSpec((tm, LANES), pos_blk),
            pl.BlockSpec((tm, LANES), pos_blk),
            pl.BlockSpec((2, SEG), const),
            pl.BlockSpec((2 * LANES, LANES), const),
        ],
        out_specs=out_specs,
        out_shape=out_shape,
        compiler_params=pltpu.CompilerParams(dimension_semantics=("parallel",), vmem_limit_bytes=VMEM_LIMIT),
        name="inproj",
    )(x2d, norm_g.reshape(1, d_model), w_in_bf16, qg, kg, cos, sa, sb, lb_logits, _group_mean_matrix())


def _stack_maps(q_bf16):
    lane = lax.broadcasted_iota(jnp.int32, q_bf16.shape, 1)
    zero = jnp.zeros_like(q_bf16)
    return jnp.concatenate([jnp.where(lane < A_DQK, q_bf16, zero), jnp.where(lane >= A_DQK, q_bf16, zero)], axis=0)


def _softmax_block_update(qs_parts, k, v1, masks, m_ref, l_ref, acc_ref):
    width = k.shape[0]
    n = qs_parts[0].shape[0]
    scores = [lax.dot_general(qs, k, (((1,), (1,)), ((), ())), preferred_element_type=F32) for qs in qs_parts]
    probs, alphas, m_news = [], [], []
    for i, s in enumerate(scores):
        if masks is not None:
            s = jnp.where(masks[i], s, -jnp.inf)
        m_prev = m_ref[i * n:(i + 1) * n]
        m_new = jnp.maximum(m_prev, jnp.max(s, axis=1, keepdims=True))
        alphas.append(jnp.exp2(m_prev - m_new))
        probs.append(jnp.exp2(s - pltpu.repeat(m_new, width // LANES, axis=1)).astype(BF16))
        m_news.append(m_new)
    for i, p in enumerate(probs):
        rows = slice(i * n, (i + 1) * n)
        pv = jnp.dot(p, v1, preferred_element_type=F32)
        acc_ref[rows] = alphas[i] * acc_ref[rows] + pv[:, :A_DV]
        l_ref[rows] = alphas[i] * l_ref[rows] + pv[:, A_DV:]
        m_ref[rows] = m_news[i]


def _reset_softmax_state(m_ref, l_ref, acc_ref):
    m_ref[...] = jnp.full(m_ref.shape, -jnp.inf, F32)
    l_ref[...] = jnp.zeros(l_ref.shape, F32)
    acc_ref[...] = jnp.zeros(acc_ref.shape, F32)


def _combine_maps(l0, acc0, l1, acc1, lam):
    return acc0 * (1.0 / l0) - lam * (acc1 * (1.0 / l1))


def _pattn_kernel(q_ref, k_ref, v_ref, lam_ref, o_ref, v1_ref, m_ref, l_ref, acc_ref, *, blk):
    qi = pl.program_id(2)

    @pl.when(qi == 0)
    def _():
        v1_ref[:, :A_DV] = v_ref[...]
        v1_ref[:, A_DV:] = jnp.ones((v1_ref.shape[0], LANES), BF16)

    qs = _stack_maps(q_ref[...].astype(BF16))
    qs_parts = [qs[:blk], qs[blk:]]
    _reset_softmax_state(m_ref, l_ref, acc_ref)

    def block(kb, masks):
        start = pl.multiple_of(kb * blk, blk)
        _softmax_block_update(qs_parts, k_ref[pl.ds(start, blk), :], v1_ref[pl.ds(start, blk), :], masks,
                              m_ref, l_ref, acc_ref)

    def body(kb, carry):
        block(kb, None)
        return carry

    lax.fori_loop(0, qi, body, 0)
    row = lax.broadcasted_iota(jnp.int32, (blk, blk), 0)
    col = lax.broadcasted_iota(jnp.int32, (blk, blk), 1)
    block(qi, [col <= row] * 2)
    o_ref[...] = _combine_maps(l_ref[:blk], acc_ref[:blk], l_ref[blk:], acc_ref[blk:], _lam_from_ref(lam_ref))


def _prompt_attention(q, kb16, vb16, lam_params, batch, seq, blk):
    nq = seq // blk
    return pl.pallas_call(
        functools.partial(_pattn_kernel, blk=blk),
        grid=(batch, A_HEADS, nq),
        in_specs=[
            pl.BlockSpec((blk, LANES), lambda b, h, i: (b * nq + i, h)),
            pl.BlockSpec((seq, LANES), lambda b, h, i: (b, h)),
            pl.BlockSpec((seq, LANES), lambda b, h, i: (b, h)),
            pl.BlockSpec((4, A_DQK), lambda b, h, i: (0, 0)),
        ],
        out_specs=pl.BlockSpec((blk, LANES), lambda b, h, i: (b * nq + i, h)),
        out_shape=jax.ShapeDtypeStruct((batch * seq, A_HEADS * A_DV), F32),
        scratch_shapes=[
            pltpu.VMEM((seq, 2 * LANES), BF16),
            pltpu.VMEM((2 * blk, LANES), F32),
            pltpu.VMEM((2 * blk, LANES), F32),
            pltpu.VMEM((2 * blk, A_DV), F32),
        ],
        compiler_params=pltpu.CompilerParams(
            dimension_semantics=("parallel", "parallel", "arbitrary"), vmem_limit_bytes=VMEM_LIMIT),
        name="prompt_attn",
    )(q, kb16, vb16, lam_params)


def _partial_softmax(qs, k, v, mask):
    s = lax.dot_general(qs, k, (((1,), (1,)), ((), ())), preferred_element_type=F32)
    s = jnp.where(mask, s, -jnp.inf)
    m = jnp.max(s, axis=1, keepdims=True)
    p = jnp.exp2(s - m)
    return m, jnp.sum(p, axis=1, keepdims=True), jnp.dot(p.astype(BF16), v, preferred_element_type=F32)


def _sattn_kernel(pt_ref, q_ref, kn_ref, vn_ref, lam_ref, *rest, n_pages, group, n_new):
    del pt_ref
    k_refs, v_refs, o_ref = rest[:n_pages], rest[n_pages:2 * n_pages], rest[2 * n_pages]
    q = q_ref[...].astype(BF16)
    qs = jnp.concatenate([_stack_maps(q[:, h * LANES:(h + 1) * LANES]) for h in range(A_HEADS)], axis=0)
    n_rows = 2 * n_new * A_HEADS

    def same_head(width):
        row = lax.broadcasted_iota(jnp.int32, (n_rows, width), 0)
        col = lax.broadcasted_iota(jnp.int32, (n_rows, width), 1)
        return row, col, (col % A_HEADS) == (row // (2 * n_new))

    n_kv = n_new * A_HEADS
    pad = jnp.zeros((LANES - n_kv, LANES), F32)
    row, col, ok = same_head(LANES)
    parts = [_partial_softmax(qs, jnp.concatenate([kn_ref[...], pad], axis=0).astype(BF16),
                              jnp.concatenate([vn_ref[...], pad], axis=0).astype(BF16),
                              ok & ((col // A_HEADS) <= (row % n_new)))]
    page_mask = same_head(group * k_refs[0].shape[0])[2]
    for g in range(0, n_pages, group):
        k = jnp.concatenate([r[...] for r in k_refs[g:g + group]], axis=0).astype(BF16)
        v = jnp.concatenate([r[...] for r in v_refs[g:g + group]], axis=0).astype(BF16)
        parts.append(_partial_softmax(qs, k, v, page_mask))
    m = functools.reduce(jnp.maximum, [p[0] for p in parts])
    weights = [jnp.exp2(p[0] - m) for p in parts]
    l = sum(w * p[1] for w, p in zip(weights, parts))
    acc = sum(w * p[2] for w, p in zip(weights, parts))
    lam = _lam_from_ref(lam_ref)
    for h in range(A_HEADS):
        r0 = h * 2 * n_new
        r1 = r0 + n_new
        o_ref[:, h * LANES:(h + 1) * LANES] = _combine_maps(
            l[r0:r1], acc[r0:r1], l[r1:r1 + n_new], acc[r1:r1 + n_new], lam)


def _sample_attention(q, k_new, v_new, cache_k, cache_v, page_table, lam_params, group):
    bs, n_pages = page_table.shape
    n_new = q.shape[0] // bs
    rows_per_page = cache_k.shape[1]
    row_blk = lambda b, pt: (b, 0)

    def page_spec(i):
        return pl.BlockSpec((None, rows_per_page, LANES), lambda b, pt: (pt[b * n_pages + i], 0, 0))

    grid_spec = pltpu.PrefetchScalarGridSpec(
        num_scalar_prefetch=1,
        grid=(bs,),
        in_specs=[
            pl.BlockSpec((n_new, q.shape[1]), row_blk),
            pl.BlockSpec((n_new * A_HEADS, LANES), row_blk),
            pl.BlockSpec((n_new * A_HEADS, LANES), row_blk),
            pl.BlockSpec((4, A_DQK), lambda b, pt: (0, 0)),
        ] + [page_spec(i) for i in range(n_pages)] * 2,
        out_specs=pl.BlockSpec((n_new, q.shape[1]), row_blk),
    )
    return pl.pallas_call(
        functools.partial(_sattn_kernel, n_pages=n_pages, group=group, n_new=n_new),
        grid_spec=grid_spec,
        out_shape=jax.ShapeDtypeStruct(q.shape, F32),
        compiler_params=pltpu.CompilerParams(dimension_semantics=("parallel",), vmem_limit_bytes=VMEM_LIMIT),
        name="sample_attn",
    )(page_table.reshape(-1), q, k_new, v_new, lam_params, *([cache_k] * n_pages), *([cache_v] * n_pages))


def _hgrn_kernel(q_ref, f_ref, v_ref, s0_ref, tri_ref, o_ref, s_ref, *, rows):
    c = pl.program_id(1)

    @pl.when(c == 0)
    def _():
        s_ref[...] = s0_ref[...]

    n = HGRN_ROWS
    pad = n - rows
    r_idx = lax.broadcasted_iota(jnp.int32, (n, n), 0)
    c_idx = lax.broadcasted_iota(jnp.int32, (n, n), 1)
    causal = c_idx <= r_idx
    for h in range(H_HEADS):
        sl = slice(h * LANES, (h + 1) * LANES)
        q, f, v = q_ref[:, sl], f_ref[:, sl], v_ref[:, sl]
        if pad:
            q = jnp.concatenate([q, jnp.zeros((pad, LANES), F32)], axis=0)
            f = jnp.concatenate([f, jnp.ones((pad, LANES), F32)], axis=0)
            v = jnp.concatenate([v, jnp.zeros((pad, LANES), F32)], axis=0)
        k = 1.0 - f
        hi, lo = _split_hi_lo(jnp.log(f))
        b = jnp.dot(tri_ref[...], jnp.concatenate([hi, lo], axis=0), preferred_element_type=F32)
        b_end = b[n - 1:n]
        b_mid = b[n // 2 - 1:n // 2]
        state = s_ref[h]
        v16 = v.astype(BF16)
        att = lax.dot_general((q * jnp.exp(b - b_mid)).astype(BF16), (k * jnp.exp(b_mid - b)).astype(BF16),
                              (((1,), (1,)), ((), ())), preferred_element_type=F32)
        att = jnp.where(causal, att, 0.0)
        o = jnp.dot(att.astype(BF16), v16, preferred_element_type=F32)
        o = o + jnp.dot((q * jnp.exp(b)).astype(BF16), state.astype(BF16), preferred_element_type=F32)
        o_ref[:, sl] = o[:rows]
        kv = lax.dot_general((k * jnp.exp(b_end - b)).astype(BF16), v16, (((0,), (0,)), ((), ())),
                             preferred_element_type=F32)
        decay_col = jnp.transpose(jnp.broadcast_to(jnp.exp(b_end), (8, LANES)))[:, 0:1]
        s_ref[h] = decay_col * state + kv


def _hgrn(hq, f, hi, s0, batch, rows):
    total = hq.shape[0]
    nch = total // (batch * rows)
    n = HGRN_ROWS
    tri = np.tril(np.ones((n, n), np.float32))
    tri = jnp.asarray(np.concatenate([tri, tri], axis=1), BF16)
    blk = lambda b, c: (b * nch + c, 0)
    st = lambda b, c: (b, 0, 0, 0)
    return pl.pallas_call(
        functools.partial(_hgrn_kernel, rows=rows),
        grid=(batch, nch),
        in_specs=[
            pl.BlockSpec((rows, SEG), blk),
            pl.BlockSpec((rows, SEG), blk),
            pl.BlockSpec((rows, SEG), blk),
            pl.BlockSpec((None, H_HEADS, H_DK, H_DV), st),
            pl.BlockSpec((n, 2 * n), lambda b, c: (0, 0)),
        ],
        out_specs=[
            pl.BlockSpec((rows, SEG), blk),
            pl.BlockSpec((None, H_HEADS, H_DK, H_DV), st),
        ],
        out_shape=[
            jax.ShapeDtypeStruct((total, SEG), F32),
            jax.ShapeDtypeStruct((batch, H_HEADS, H_DK, H_DV), F32),
        ],
        compiler_params=pltpu.CompilerParams(
            dimension_semantics=("parallel", "arbitrary"), vmem_limit_bytes=VMEM_LIMIT),
        name="hgrn2",
    )(hq, f, hi, s0, tri)


def _merge_kernel(x_ref, oa_ref, ga_ref, oh_ref, gh_ref, sg_ref, hg_ref, w_ref, y_ref):
    def head_norm(o_ref, g_ref, h):
        o = o_ref[:, h * LANES:(h + 1) * LANES]
        return o * lax.rsqrt(jnp.mean(o * o, axis=-1, keepdims=True) + EPS) * g_ref[...]

    a = jnp.concatenate([head_norm(oa_ref, sg_ref, h) for h in range(A_HEADS)], axis=1)
    a = a * (1.0 - LAM_INIT) * ga_ref[...]
    r = jnp.concatenate([head_norm(oh_ref, hg_ref, h) for h in range(H_HEADS)], axis=1)
    r = r * gh_ref[...]
    mix = jnp.concatenate([a, r], axis=1).astype(BF16)
    y_ref[...] = x_ref[...] + jnp.dot(mix, w_ref[...], preferred_element_type=F32)


def _merge(x2d, o_attn, ga, o_hgrn, gh, subln_g, hgrn_norm_g, w_out_bf16, tm):
    rows, d_model = x2d.shape
    row_blk = lambda i: (i, 0)
    const = lambda i: (0, 0)
    return pl.pallas_call(
        _merge_kernel,
        grid=(rows // tm,),
        in_specs=[
            pl.BlockSpec((tm, d_model), row_blk),
            pl.BlockSpec((tm, SEG), row_blk),
            pl.BlockSpec((tm, SEG), row_blk),
            pl.BlockSpec((tm, SEG), row_blk),
            pl.BlockSpec((tm, SEG), row_blk),
            pl.BlockSpec((1, LANES), const),
            pl.BlockSpec((1, LANES), const),
            pl.BlockSpec(w_out_bf16.shape, const),
        ],
        out_specs=pl.BlockSpec((tm, d_model), row_blk),
        out_shape=jax.ShapeDtypeStruct((rows, d_model), F32),
        compiler_params=pltpu.CompilerParams(dimension_semantics=("parallel",), vmem_limit_bytes=VMEM_LIMIT),
        name="merge",
    )(x2d, o_attn, ga, o_hgrn, gh, subln_g.reshape(1, LANES), hgrn_norm_g.reshape(1, LANES), w_out_bf16)


def kernel(x_prompt, x_sample, cache_k, cache_v, state_hgrn, page_table, norm_g, w_in, q_norm_g, k_norm_g,
           lambda_q1, lambda_k1, lambda_q2, lambda_k2, subln_g, hgrn_lb_logits, hgrn_norm_g, w_out):
    bp, tp, d_model = x_prompt.shape
    bs, ts, _ = x_sample.shape
    depth, n_pool, page_size = cache_k.shape[:3]
    assert depth == 1 and hgrn_lb_logits.shape[0] == 2
    assert (A_DV, H_DK, H_DV, 2 * A_DQK) == (LANES,) * 4
    past_len = page_table.shape[1] * page_size
    tm = 256
    attn_blk = 512

    w_in16 = w_in[0].astype(BF16)
    w_out16 = w_out[0].astype(BF16)
    lam_params = jnp.stack([lambda_q1[0], lambda_k1[0], lambda_q2[0], lambda_k2[0]])
    proj = functools.partial(_inproj, norm_g=norm_g[0], w_in_bf16=w_in16, q_norm_g=q_norm_g[0],
                             k_norm_g=k_norm_g[0], lb_logits=hgrn_lb_logits)
    fin = functools.partial(_merge, subln_g=subln_g[0], hgrn_norm_g=hgrn_norm_g[0], w_out_bf16=w_out16, tm=tm)

    xp = x_prompt.reshape(bp * tp, d_model)
    q, k, v, ga, hq, f, hi, gh, k16, v16 = proj(xp, jnp.arange(tp, dtype=jnp.int32), tp // tm, tm, True)
    o_attn = _prompt_attention(q, k16, v16, lam_params, bp, tp, attn_blk)
    o_hgrn, s_p = _hgrn(hq, f, hi, jnp.zeros((bp, H_HEADS, H_DK, H_DV), F32), bp, HGRN_ROWS)
    y_p = fin(xp, o_attn, ga, o_hgrn, gh)

    xs = x_sample.reshape(bs * ts, d_model)
    pos_s = past_len + jnp.tile(jnp.arange(ts, dtype=jnp.int32), tm // ts)
    sq, sk, sv, sga, shq, sf, shi, sgh = proj(xs, pos_s, 1, tm, False)
    ck = cache_k.reshape(n_pool, page_size * A_HEADS, 2 * A_DQK)
    cv = cache_v.reshape(n_pool, page_size * A_HEADS, A_DV)
    so_attn = _sample_attention(sq, sk, sv, ck, cv, page_table, lam_params, group=4)
    so_hgrn, s_s = _hgrn(shq, sf, shi, state_hgrn[0], bs, ts)
    y_s = fin(xs, so_attn, sga, so_hgrn, sgh)

    return (y_p.reshape(bp, tp, d_model), y_s.reshape(bs, ts, d_model),
            k.reshape(1, bp, tp, A_HEADS, 2 * A_DQK), v.reshape(1, bp, tp, A_HEADS, A_DV), s_p[None],
            sk.reshape(1, bs, ts, A_HEADS, 2 * A_DQK), sv.reshape(1, bs, ts, A_HEADS, A_DV), s_s[None])
```

```python
import functools
import math

import numpy as np
import jax
import jax.numpy as jnp
from jax import lax
from jax.experimental import pallas as pl
from jax.experimental.pallas import tpu as pltpu

F32 = jnp.float32
BF16 = jnp.bfloat16

LANES = 128
A_HEADS = 4
A_DQK = 64
A_DV = 128
H_HEADS = 4
H_DK = 128
H_DV = 128
SEG = 512
ROT_DIM = A_DQK // 4
ROPE_THETA = 500000.0
EPS = 1e-6
LAM_INIT = 0.8 - 0.6 * math.exp(-0.3 * 0)
Q_SCALE = A_DQK ** -0.5 * math.log2(math.e)
HGRN_ROWS = 128
SCORE_BOUND_LIMIT = 60.0
ATTN_CHAINS = 4
VMEM_LIMIT = 48 * 1024 * 1024


def _sigmoid(x):
    return 1.0 / (1.0 + jnp.exp(-x))


def _split_hi_lo(x):
    hi = x.astype(BF16)
    lo = (x - hi.astype(F32)).astype(BF16)
    return hi, lo


def _lam_from_ref(lam_ref):
    lp = lam_ref[...]
    s1 = jnp.sum(lp[0:1] * lp[1:2], axis=1, keepdims=True)
    s2 = jnp.sum(lp[2:3] * lp[3:4], axis=1, keepdims=True)
    return jnp.exp(s1) - jnp.exp(s2) + LAM_INIT


def _inproj_kernel(x_ref, ng_ref, w_ref, qg_ref, kg_ref, cos_ref, sa_ref, sb_ref, lbl_ref, gm_ref,
                   q_ref, k_ref, v_ref, ga_ref, hq_ref, f_ref, hi_ref, gh_ref, *maybe_bf16_refs):
    x = x_ref[...]
    ms = jnp.mean(x * x, axis=-1, keepdims=True)
    h = (x * lax.rsqrt(ms + EPS) * ng_ref[...]).astype(BF16)

    def proj(seg):
        return jnp.dot(h, w_ref[:, seg * SEG:(seg + 1) * SEG], preferred_element_type=F32)

    cos, sa, sb = cos_ref[...], sa_ref[...], sb_ref[...]

    def norm_rope(y, g_ref, hh):
        yh = y[:, hh * LANES:(hh + 1) * LANES]
        hi, lo = _split_hi_lo(yh * yh)
        gms = jnp.dot(jnp.concatenate([hi, lo], axis=1), gm_ref[...], preferred_element_type=F32)
        yn = yh * lax.rsqrt(gms + EPS) * g_ref[...]
        return yn * cos + pltpu.roll(yn, LANES - ROT_DIM // 2, 1) * sa + pltpu.roll(yn, ROT_DIM // 2, 1) * sb

    tm = x.shape[0]
    aq = proj(0)
    for hh in range(A_HEADS):
        q_ref[:, hh * LANES:(hh + 1) * LANES] = norm_rope(aq, qg_ref, hh) * Q_SCALE
    ak = proj(1)
    for hh in range(A_HEADS):
        kh = norm_rope(ak, kg_ref, hh)
        k_ref[pl.ds(hh, tm, stride=A_HEADS), :] = kh
        if maybe_bf16_refs:
            maybe_bf16_refs[0][:, hh * LANES:(hh + 1) * LANES] = kh.astype(BF16)
    av = proj(2)
    for hh in range(A_HEADS):
        v_ref[pl.ds(hh, tm, stride=A_HEADS), :] = av[:, hh * LANES:(hh + 1) * LANES]
    if maybe_bf16_refs:
        maybe_bf16_refs[1][...] = av.astype(BF16)
    ag = proj(3)
    ga_ref[...] = ag * _sigmoid(ag)
    hq = proj(4)
    hq_ref[...] = hq * _sigmoid(hq)
    hf = proj(5)
    lbl = lbl_ref[...]
    l0, l1 = lbl[0:1], lbl[1:2]
    mx = jnp.maximum(l0, l1)
    e0, e1 = jnp.exp(l0 - mx), jnp.exp(l1 - mx)
    lb = e0 / (e0 + e1)
    f_ref[...] = lb + (1.0 - lb) * _sigmoid(hf)
    hi_ref[...] = proj(6)
    hg = proj(7)
    gh_ref[...] = hg * _sigmoid(hg)


def _rope_tables(pos):
    half = ROT_DIM // 2
    inv = ROPE_THETA ** (-jnp.arange(0, ROT_DIM, 2, dtype=F32) / ROT_DIM)
    lane = np.arange(LANES) % A_DQK
    ang = pos.astype(F32)[:, None] * inv[lane % half][None, :]
    cos = jnp.where((lane < ROT_DIM)[None, :], jnp.cos(ang), 1.0)
    sin = jnp.sin(ang)
    a = jnp.where((lane < half)[None, :], -sin, 0.0)
    b = jnp.where(((lane >= half) & (lane < ROT_DIM))[None, :], sin, 0.0)
    return cos, a, b


def _group_mean_matrix():
    g = np.zeros((LANES, LANES), np.float32)
    for s in range(0, LANES, A_DQK):
        g[s:s + A_DQK, s:s + A_DQK] = 1.0 / A_DQK
    return jnp.asarray(np.concatenate([g, g], axis=0), BF16)


def _inproj(x2d, pos, n_pos_blocks, tm, with_bf16, norm_g, w_in_bf16, q_norm_g, k_norm_g, lb_logits):
    rows, d_model = x2d.shape
    d_in = w_in_bf16.shape[1]
    cos, sa, sb = _rope_tables(pos)
    qg = jnp.concatenate([q_norm_g, q_norm_g]).reshape(1, LANES)
    kg = jnp.concatenate([k_norm_g, k_norm_g]).reshape(1, LANES)
    const = lambda i: (0, 0)
    row_blk = lambda i: (i, 0)
    pos_blk = lambda i: (i % n_pos_blocks, 0)
    out_f32 = jax.ShapeDtypeStruct((rows, SEG), F32)
    out_shape = [out_f32] * 8
    out_specs = [pl.BlockSpec((tm, SEG), row_blk)] * 8
    for i in (1, 2):
        out_shape[i] = jax.ShapeDtypeStruct((rows * A_HEADS, LANES), F32)
        out_specs[i] = pl.BlockSpec((tm * A_HEADS, LANES), row_blk)
    if with_bf16:
        out_shape += [jax.ShapeDtypeStruct((rows, SEG), BF16)] * 2
        out_specs += [pl.BlockSpec((tm, SEG), row_blk)] * 2
    return pl.pallas_call(
        _inproj_kernel,
        grid=(rows // tm,),
        in_specs=[
            pl.BlockSpec((tm, d_model), row_blk),
            pl.BlockSpec((1, d_model), const),
            pl.BlockSpec((d_model, d_in), const),
            pl.BlockSpec((1, LANES), const),
            pl.BlockSpec((1, LANES), const),
            pl.BlockSpec((tm, LANES), pos_blk),
            pl.BlockSpec((tm, LANES), pos_blk),
            pl.BlockSpec((tm, LANES), pos_blk),
            pl.BlockSpec((2, SEG), const),
            pl.BlockSpec((2 * LANES, LANES), const),
        ],
        out_specs=out_specs,
        out_shape=out_shape,
        compiler_params=pltpu.CompilerParams(dimension_semantics=("parallel",), vmem_limit_bytes=VMEM_LIMIT),
        name="inproj",
    )(x2d, norm_g.reshape(1, d_model), w_in_bf16, qg, kg, cos, sa, sb, lb_logits, _group_mean_matrix())


def _stack_maps(q_bf16):
    lane = lax.broadcasted_iota(jnp.int32, q_bf16.shape, 1)
    zero = jnp.zeros_like(q_bf16)
    return jnp.concatenate([jnp.where(lane < A_DQK, q_bf16, zero), jnp.where(lane >= A_DQK, q_bf16, zero)], axis=0)


def _softmax_block_update(qs_parts, k, v1, masks, m_ref, l_ref, acc_ref):
    width = k.shape[0]
    n = qs_parts[0].shape[0]
    scores = [lax.dot_general(qs, k, (((1,), (1,)), ((), ())), preferred_element_type=F32) for qs in qs_parts]
    probs, alphas, m_news = [], [], []
    for i, s in enumerate(scores):
        if masks is not None:
            s = jnp.where(masks[i], s, -jnp.inf)
        m_prev = m_ref[i * n:(i + 1) * n]
        m_new = jnp.maximum(m_prev, jnp.max(s, axis=1, keepdims=True))
        alphas.append(jnp.exp2(m_prev - m_new))
        probs.append(jnp.exp2(s - jnp.concatenate([m_new] * (width // LANES), axis=1)).astype(BF16))
        m_news.append(m_new)
    for i, p in enumerate(probs):
        rows = slice(i * n, (i + 1) * n)
        pv = jnp.dot(p, v1, preferred_element_type=F32)
        acc_ref[rows] = alphas[i] * acc_ref[rows] + pv[:, :A_DV]
        l_ref[rows] = alphas[i] * l_ref[rows] + pv[:, A_DV:]
        m_ref[rows] = m_news[i]


def _combine_maps(l0, acc0, l1, acc1, lam):
    return acc0 * (1.0 / l0) - lam * (acc1 * (1.0 / l1))


def _bounded_block_update(qs_parts, k, v1, masks, ref_ref, l_ref, acc_ref):
    width = k.shape[0]
    n = qs_parts[0].shape[0]
    scores = [lax.dot_general(qs, k, (((1,), (1,)), ((), ())), preferred_element_type=F32) for qs in qs_parts]
    for i, s in enumerate(scores):
        rows = slice(i * n, (i + 1) * n)
        if masks is not None:
            s = jnp.where(masks[i], s, -jnp.inf)
        p = jnp.exp2(s - jnp.concatenate([ref_ref[rows]] * (width // LANES), axis=1)).astype(BF16)
        pv = jnp.dot(p, v1, preferred_element_type=F32)
        acc_ref[rows] = acc_ref[rows] + pv[:, :A_DV]
        l_ref[rows] = l_ref[rows] + pv[:, A_DV:]


def _map_sum_matrix():
    g = np.zeros((LANES, 2 * LANES), np.float32)
    g[:A_DQK, :LANES] = 1.0
    g[A_DQK:, LANES:] = 1.0
    return jnp.asarray(g, BF16)


def _pattn_kernel(q_ref, k_ref, v_ref, lam_ref, ms_ref, o_ref, v1_ref, kn_ref, m_ref, l_ref, acc_ref, *, blk):
    qi = pl.program_id(2)

    @pl.when(qi == 0)
    def _():
        v1_ref[:, :A_DV] = v_ref[...]
        v1_ref[:, A_DV:] = jnp.ones((v1_ref.shape[0], LANES), BF16)
        k = k_ref[...]
        kn_ref[...] = jnp.max(jnp.dot(k * k, ms_ref[...], preferred_element_type=F32), axis=0, keepdims=True)

    q = q_ref[...].astype(BF16)
    qs = _stack_maps(q)
    part = 2 * blk // ATTN_CHAINS
    qs_parts = [qs[i * part:(i + 1) * part] for i in range(ATTN_CHAINS)]
    row = lax.broadcasted_iota(jnp.int32, (part, blk), 0)
    col = lax.broadcasted_iota(jnp.int32, (part, blk), 1)
    diag_masks = [col <= row + (i * part) % blk for i in range(ATTN_CHAINS)]
    l_ref[...] = jnp.zeros(l_ref.shape, F32)
    acc_ref[...] = jnp.zeros(acc_ref.shape, F32)

    qn = jnp.dot(q * q, ms_ref[...], preferred_element_type=F32)
    kn = kn_ref[...]
    bound = jnp.concatenate([jnp.sqrt(qn[:, :LANES] * kn[:, :LANES]), jnp.sqrt(qn[:, LANES:] * kn[:, LANES:])],
                            axis=0) * 1.02
    m_ref[...] = bound
    bounded = jnp.max(bound) <= SCORE_BOUND_LIMIT

    def run(update, state_ref):
        def block(kb, masks):
            start = pl.multiple_of(kb * blk, blk)
            update(qs_parts, k_ref[pl.ds(start, blk), :], v1_ref[pl.ds(start, blk), :], masks,
                   state_ref, l_ref, acc_ref)

        def body(kb, carry):
            block(kb, None)
            return carry

        lax.fori_loop(0, qi, body, 0)
        block(qi, diag_masks)
        o_ref[...] = _combine_maps(l_ref[:blk], acc_ref[:blk], l_ref[blk:], acc_ref[blk:], _lam_from_ref(lam_ref))

    @pl.when(bounded)
    def _():
        run(_bounded_block_update, m_ref)

    @pl.when(jnp.logical_not(bounded))
    def _():
        m_ref[...] = jnp.full(m_ref.shape, -jnp.inf, F32)
        run(_softmax_block_update, m_ref)


def _prompt_attention(q, kb16, vb16, lam_params, batch, seq, blk):
    nq = seq // blk
    return pl.pallas_call(
        functools.partial(_pattn_kernel, blk=blk),
        grid=(batch, A_HEADS, nq),
        in_specs=[
            pl.BlockSpec((blk, LANES), lambda b, h, i: (b * nq + i, h)),
            pl.BlockSpec((seq, LANES), lambda b, h, i: (b, h)),
            pl.BlockSpec((seq, LANES), lambda b, h, i: (b, h)),
            pl.BlockSpec((4, A_DQK), lambda b, h, i: (0, 0)),
            pl.BlockSpec((LANES, 2 * LANES), lambda b, h, i: (0, 0)),
        ],
        out_specs=pl.BlockSpec((blk, LANES), lambda b, h, i: (b * nq + i, h)),
        out_shape=jax.ShapeDtypeStruct((batch * seq, A_HEADS * A_DV), F32),
        scratch_shapes=[
            pltpu.VMEM((seq, 2 * LANES), BF16),
            pltpu.VMEM((1, 2 * LANES), F32),
            pltpu.VMEM((2 * blk, LANES), F32),
            pltpu.VMEM((2 * blk, LANES), F32),
            pltpu.VMEM((2 * blk, A_DV), F32),
        ],
        compiler_params=pltpu.CompilerParams(
            dimension_semantics=("parallel", "parallel", "arbitrary"), vmem_limit_bytes=VMEM_LIMIT),
        name="prompt_attn",
    )(q, kb16, vb16, lam_params, _map_sum_matrix())


def _partial_softmax(qs, k, v, mask):
    s = lax.dot_general(qs, k, (((1,), (1,)), ((), ())), preferred_element_type=F32)
    s = jnp.where(mask, s, -jnp.inf)
    m = jnp.max(s, axis=1, keepdims=True)
    p = jnp.exp2(s - m)
    return m, jnp.sum(p, axis=1, keepdims=True), jnp.dot(p.astype(BF16), v, preferred_element_type=F32)


def _sattn_kernel(pt_ref, q_ref, kn_ref, vn_ref, lam_ref, *rest, n_pages, group, n_new):
    del pt_ref
    k_refs, v_refs, o_ref = rest[:n_pages], rest[n_pages:2 * n_pages], rest[2 * n_pages]
    q = q_ref[...].astype(BF16)
    qs = jnp.concatenate([_stack_maps(q[:, h * LANES:(h + 1) * LANES]) for h in range(A_HEADS)], axis=0)
    n_rows = 2 * n_new * A_HEADS

    def same_head(width):
        row = lax.broadcasted_iota(jnp.int32, (n_rows, width), 0)
        col = lax.broadcasted_iota(jnp.int32, (n_rows, width), 1)
        return row, col, (col % A_HEADS) == (row // (2 * n_new))

    n_kv = n_new * A_HEADS
    pad = jnp.zeros((LANES - n_kv, LANES), F32)
    row, col, ok = same_head(LANES)
    parts = [_partial_softmax(qs, jnp.concatenate([kn_ref[...], pad], axis=0).astype(BF16),
                              jnp.concatenate([vn_ref[...], pad], axis=0).astype(BF16),
                              ok & ((col // A_HEADS) <= (row % n_new)))]
    page_mask = same_head(group * k_refs[0].shape[0])[2]
    for g in range(0, n_pages, group):
        k = jnp.concatenate([r[...] for r in k_refs[g:g + group]], axis=0).astype(BF16)
        v = jnp.concatenate([r[...] for r in v_refs[g:g + group]], axis=0).astype(BF16)
        parts.append(_partial_softmax(qs, k, v, page_mask))
    m = functools.reduce(jnp.maximum, [p[0] for p in parts])
    weights = [jnp.exp2(p[0] - m) for p in parts]
    l = sum(w * p[1] for w, p in zip(weights, parts))
    acc = sum(w * p[2] for w, p in zip(weights, parts))
    lam = _lam_from_ref(lam_ref)
    for h in range(A_HEADS):
        r0 = h * 2 * n_new
        r1 = r0 + n_new
        o_ref[:, h * LANES:(h + 1) * LANES] = _combine_maps(
            l[r0:r1], acc[r0:r1], l[r1:r1 + n_new], acc[r1:r1 + n_new], lam)


def _sample_attention(q, k_new, v_new, cache_k, cache_v, page_table, lam_params, group):
    bs, n_pages = page_table.shape
    n_new = q.shape[0] // bs
    rows_per_page = cache_k.shape[1]
    row_blk = lambda b, pt: (b, 0)

    def page_spec(i):
        return pl.BlockSpec((None, rows_per_page, LANES), lambda b, pt: (pt[b * n_pages + i], 0, 0))

    grid_spec = pltpu.PrefetchScalarGridSpec(
        num_scalar_prefetch=1,
        grid=(bs,),
        in_specs=[
            pl.BlockSpec((n_new, q.shape[1]), row_blk),
            pl.BlockSpec((n_new * A_HEADS, LANES), row_blk),
            pl.BlockSpec((n_new * A_HEADS, LANES), row_blk),
            pl.BlockSpec((4, A_DQK), lambda b, pt: (0, 0)),
        ] + [page_spec(i) for i in range(n_pages)] * 2,
        out_specs=pl.BlockSpec((n_new, q.shape[1]), row_blk),
    )
    return pl.pallas_call(
        functools.partial(_sattn_kernel, n_pages=n_pages, group=group, n_new=n_new),
        grid_spec=grid_spec,
        out_shape=jax.ShapeDtypeStruct(q.shape, F32),
        compiler_params=pltpu.CompilerParams(dimension_semantics=("parallel",), vmem_limit_bytes=VMEM_LIMIT),
        name="sample_attn",
    )(page_table.reshape(-1), q, k_new, v_new, lam_params, *([cache_k] * n_pages), *([cache_v] * n_pages))


def _hgrn_kernel(q_ref, f_ref, v_ref, s0_ref, tri_ref, o_ref, s_ref, *, rows):
    c = pl.program_id(1)

    @pl.when(c == 0)
    def _():
        s_ref[...] = s0_ref[...]

    n = HGRN_ROWS
    pad = n - rows
    r_idx = lax.broadcasted_iota(jnp.int32, (n, n), 0)
    c_idx = lax.broadcasted_iota(jnp.int32, (n, n), 1)
    causal = c_idx <= r_idx
    for h in range(H_HEADS):
        sl = slice(h * LANES, (h + 1) * LANES)
        q, f, v = q_ref[:, sl], f_ref[:, sl], v_ref[:, sl]
        if pad:
            q = jnp.concatenate([q, jnp.zeros((pad, LANES), F32)], axis=0)
            f = jnp.concatenate([f, jnp.ones((pad, LANES), F32)], axis=0)
            v = jnp.concatenate([v, jnp.zeros((pad, LANES), F32)], axis=0)
        k = 1.0 - f
        hi, lo = _split_hi_lo(jnp.log(f))
        b = jnp.dot(tri_ref[...], jnp.concatenate([hi, lo], axis=0), preferred_element_type=F32)
        b_end = b[n - 1:n]
        b_mid = b[n // 2 - 1:n // 2]
        state = s_ref[h]
        v16 = v.astype(BF16)
        att = lax.dot_general((q * jnp.exp(b - b_mid)).astype(BF16), (k * jnp.exp(b_mid - b)).astype(BF16),
                              (((1,), (1,)), ((), ())), preferred_element_type=F32)
        att = jnp.where(causal, att, 0.0)
        o = jnp.dot(att.astype(BF16), v16, preferred_element_type=F32)
        o = o + jnp.dot((q * jnp.exp(b)).astype(BF16), state.astype(BF16), preferred_element_type=F32)
        o_ref[:, sl] = o[:rows]
        kv = lax.dot_general((k * jnp.exp(b_end - b)).astype(BF16), v16, (((0,), (0,)), ((), ())),
                             preferred_element_type=F32)
        decay_col = jnp.transpose(jnp.broadcast_to(jnp.exp(b_end), (8, LANES)))[:, 0:1]
        s_ref[h] = decay_col * state + kv


def _hgrn(hq, f, hi, s0, batch, rows):
    total = hq.shape[0]
    nch = total // (batch * rows)
    n = HGRN_ROWS
    tri = np.tril(np.ones((n, n), np.float32))
    tri = jnp.asarray(np.concatenate([tri, tri], axis=1), BF16)
    blk = lambda b, c: (b * nch + c, 0)
    st = lambda b, c: (b, 0, 0, 0)
    return pl.pallas_call(
        functools.partial(_hgrn_kernel, rows=rows),
        grid=(batch, nch),
        in_specs=[
            pl.BlockSpec((rows, SEG), blk),
            pl.BlockSpec((rows, SEG), blk),
            pl.BlockSpec((rows, SEG), blk),
            pl.BlockSpec((None, H_HEADS, H_DK, H_DV), st),
            pl.BlockSpec((n, 2 * n), lambda b, c: (0, 0)),
        ],
        out_specs=[
            pl.BlockSpec((rows, SEG), blk),
            pl.BlockSpec((None, H_HEADS, H_DK, H_DV), st),
        ],
        out_shape=[
            jax.ShapeDtypeStruct((total, SEG), F32),
            jax.ShapeDtypeStruct((batch, H_HEADS, H_DK, H_DV), F32),
        ],
        compiler_params=pltpu.CompilerParams(
            dimension_semantics=("parallel", "arbitrary"), vmem_limit_bytes=VMEM_LIMIT),
        name="hgrn2",
    )(hq, f, hi, s0, tri)


def _merge_kernel(x_ref, oa_ref, ga_ref, oh_ref, gh_ref, sg_ref, hg_ref, w_ref, y_ref):
    def head_norm(o_ref, g_ref, h):
        o = o_ref[:, h * LANES:(h + 1) * LANES]
        return o * lax.rsqrt(jnp.mean(o * o, axis=-1, keepdims=True) + EPS) * g_ref[...]

    a = jnp.concatenate([head_norm(oa_ref, sg_ref, h) for h in range(A_HEADS)], axis=1)
    a = a * (1.0 - LAM_INIT) * ga_ref[...]
    r = jnp.concatenate([head_norm(oh_ref, hg_ref, h) for h in range(H_HEADS)], axis=1)
    r = r * gh_ref[...]
    mix = jnp.concatenate([a, r], axis=1).astype(BF16)
    y_ref[...] = x_ref[...] + jnp.dot(mix, w_ref[...], preferred_element_type=F32)


def _merge(x2d, o_attn, ga, o_hgrn, gh, subln_g, hgrn_norm_g, w_out_bf16, tm):
    rows, d_model = x2d.shape
    row_blk = lambda i: (i, 0)
    const = lambda i: (0, 0)
    return pl.pallas_call(
        _merge_kernel,
        grid=(rows // tm,),
        in_specs=[
            pl.BlockSpec((tm, d_model), row_blk),
            pl.BlockSpec((tm, SEG), row_blk),
            pl.BlockSpec((tm, SEG), row_blk),
            pl.BlockSpec((tm, SEG), row_blk),
            pl.BlockSpec((tm, SEG), row_blk),
            pl.BlockSpec((1, LANES), const),
            pl.BlockSpec((1, LANES), const),
            pl.BlockSpec(w_out_bf16.shape, const),
        ],
        out_specs=pl.BlockSpec((tm, d_model), row_blk),
        out_shape=jax.ShapeDtypeStruct((rows, d_model), F32),
        compiler_params=pltpu.CompilerParams(dimension_semantics=("parallel",), vmem_limit_bytes=VMEM_LIMIT),
        name="merge",
    )(x2d, o_attn, ga, o_hgrn, gh, subln_g.reshape(1, LANES), hgrn_norm_g.reshape(1, LANES), w_out_bf16)


def kernel(x_prompt, x_sample, cache_k, cache_v, state_hgrn, page_table, norm_g, w_in, q_norm_g, k_norm_g,
           lambda_q1, lambda_k1, lambda_q2, lambda_k2, subln_g, hgrn_lb_logits, hgrn_norm_g, w_out):
    bp, tp, d_model = x_prompt.shape
    bs, ts, _ = x_sample.shape
    depth, n_pool, page_size = cache_k.shape[:3]
    assert depth == 1 and hgrn_lb_logits.shape[0] == 2
    assert (A_DV, H_DK, H_DV, 2 * A_DQK) == (LANES,) * 4
    past_len = page_table.shape[1] * page_size
    tm = 256
    attn_blk = 512

    w_in16 = w_in[0].astype(BF16)
    w_out16 = w_out[0].astype(BF16)
    lam_params = jnp.stack([lambda_q1[0], lambda_k1[0], lambda_q2[0], lambda_k2[0]])
    proj = functools.partial(_inproj, norm_g=norm_g[0], w_in_bf16=w_in16, q_norm_g=q_norm_g[0],
                             k_norm_g=k_norm_g[0], lb_logits=hgrn_lb_logits)
    fin = functools.partial(_merge, subln_g=subln_g[0], hgrn_norm_g=hgrn_norm_g[0], w_out_bf16=w_out16, tm=tm)

    xp = x_prompt.reshape(bp * tp, d_model)
    q, k, v, ga, hq, f, hi, gh, k16, v16 = proj(xp, jnp.arange(tp, dtype=jnp.int32), tp // tm, tm, True)
    o_attn = _prompt_attention(q, k16, v16, lam_params, bp, tp, attn_blk)
    o_hgrn, s_p = _hgrn(hq, f, hi, jnp.zeros((bp, H_HEADS, H_DK, H_DV), F32), bp, HGRN_ROWS)
    y_p = fin(xp, o_attn, ga, o_hgrn, gh)

    xs = x_sample.reshape(bs * ts, d_model)
    pos_s = past_len + jnp.tile(jnp.arange(ts, dtype=jnp.int32), tm // ts)
    sq, sk, sv, sga, shq, sf, shi, sgh = proj(xs, pos_s, 1, tm, False)
    ck = cache_k.reshape(n_pool, page_size * A_HEADS, 2 * A_DQK)
    cv = cache_v.reshape(n_pool, page_size * A_HEADS, A_DV)
    so_attn = _sample_attention(sq, sk, sv, ck, cv, page_table, lam_params, group=4)
    so_hgrn, s_s = _hgrn(shq, sf, shi, state_hgrn[0], bs, ts)
    y_s = fin(xs, so_attn, sga, so_hgrn, sgh)

    return (y_p.reshape(bp, tp, d_model), y_s.reshape(bs, ts, d_model),
            k.reshape(1, bp, tp, A_HEADS, 2 * A_DQK), v.reshape(1, bp, tp, A_HEADS, A_DV), s_p[None],
            sk.reshape(1, bs, ts, A_HEADS, 2 * A_DQK), sv.reshape(1, bs, ts, A_HEADS, A_DV), s_s[None])
```

```python
import functools
import math

import numpy as np
import jax
import jax.numpy as jnp
from jax import lax
from jax.experimental import pallas as pl
from jax.experimental.pallas import tpu as pltpu

F32 = jnp.float32
BF16 = jnp.bfloat16

LANES = 128
A_HEADS = 4
A_DQK = 64
A_DV = 128
H_HEADS = 4
H_DK = 128
H_DV = 128
SEG = 512
ROT_DIM = A_DQK // 4
ROPE_THETA = 500000.0
EPS = 1e-6
LAM_INIT = 0.8 - 0.6 * math.exp(-0.3 * 0)
Q_SCALE = A_DQK ** -0.5 * math.log2(math.e)
HGRN_ROWS = 128
SCORE_BOUND_LIMIT = 60.0
ATTN_CHAINS = 4
VMEM_LIMIT = 48 * 1024 * 1024


def _sigmoid(x):
    return 1.0 / (1.0 + jnp.exp(-x))


def _split_hi_lo(x):
    hi = x.astype(BF16)
    lo = (x - hi.astype(F32)).astype(BF16)
    return hi, lo


def _lam_from_ref(lam_ref):
    lp = lam_ref[...]
    s1 = jnp.sum(lp[0:1] * lp[1:2], axis=1, keepdims=True)
    s2 = jnp.sum(lp[2:3] * lp[3:4], axis=1, keepdims=True)
    return jnp.exp(s1) - jnp.exp(s2) + LAM_INIT


def _inproj_kernel(x_ref, ng_ref, w_ref, qg_ref, kg_ref, cos_ref, sa_ref, sb_ref, lbl_ref, gm_ref,
                   q_ref, k_ref, v_ref, ga_ref, hq_ref, f_ref, hi_ref, gh_ref, *maybe_bf16_refs):
    x = x_ref[...]
    ms = jnp.mean(x * x, axis=-1, keepdims=True)
    h = (x * lax.rsqrt(ms + EPS) * ng_ref[...]).astype(BF16)

    def proj(seg):
        return jnp.dot(h, w_ref[:, seg * SEG:(seg + 1) * SEG], preferred_element_type=F32)

    cos, sa, sb = cos_ref[...], sa_ref[...], sb_ref[...]

    def norm_rope(y, g_ref, hh):
        yh = y[:, hh * LANES:(hh + 1) * LANES]
        hi, lo = _split_hi_lo(yh * yh)
        gms = jnp.dot(jnp.concatenate([hi, lo], axis=1), gm_ref[...], preferred_element_type=F32)
        yn = yh * lax.rsqrt(gms + EPS) * g_ref[...]
        return yn * cos + pltpu.roll(yn, LANES - ROT_DIM // 2, 1) * sa + pltpu.roll(yn, ROT_DIM // 2, 1) * sb

    tm = x.shape[0]
    aq = proj(0)
    for hh in range(A_HEADS):
        q_ref[:, hh * LANES:(hh + 1) * LANES] = norm_rope(aq, qg_ref, hh) * Q_SCALE
    ak = proj(1)
    for hh in range(A_HEADS):
        kh = norm_rope(ak, kg_ref, hh)
        k_ref[pl.ds(hh, tm, stride=A_HEADS), :] = kh
        if maybe_bf16_refs:
            maybe_bf16_refs[0][:, hh * LANES:(hh + 1) * LANES] = kh.astype(BF16)
    av = proj(2)
    for hh in range(A_HEADS):
        v_ref[pl.ds(hh, tm, stride=A_HEADS), :] = av[:, hh * LANES:(hh + 1) * LANES]
    if maybe_bf16_refs:
        maybe_bf16_refs[1][...] = av.astype(BF16)
    ag = proj(3)
    ga_ref[...] = ag * _sigmoid(ag)
    hq = proj(4)
    hq_ref[...] = hq * _sigmoid(hq)
    hf = proj(5)
    lbl = lbl_ref[...]
    l0, l1 = lbl[0:1], lbl[1:2]
    mx = jnp.maximum(l0, l1)
    e0, e1 = jnp.exp(l0 - mx), jnp.exp(l1 - mx)
    lb = e0 / (e0 + e1)
    f_ref[...] = lb + (1.0 - lb) * _sigmoid(hf)
    hi_ref[...] = proj(6)
    hg = proj(7)
    gh_ref[...] = hg * _sigmoid(hg)


def _rope_tables(pos):
    half = ROT_DIM // 2
    inv = ROPE_THETA ** (-jnp.arange(0, ROT_DIM, 2, dtype=F32) / ROT_DIM)
    lane = np.arange(LANES) % A_DQK
    ang = pos.astype(F32)[:, None] * inv[lane % half][None, :]
    cos = jnp.where((lane < ROT_DIM)[None, :], jnp.cos(ang), 1.0)
    sin = jnp.sin(ang)
    a = jnp.where((lane < half)[None, :], -sin, 0.0)
    b = jnp.where(((lane >= half) & (lane < ROT_DIM))[None, :], sin, 0.0)
    return cos, a, b


def _group_mean_matrix():
    g = np.zeros((LANES, LANES), np.float32)
    for s in range(0, LANES, A_DQK):
        g[s:s + A_DQK, s:s + A_DQK] = 1.0 / A_DQK
    return jnp.asarray(np.concatenate([g, g], axis=0), BF16)


def _inproj(x2d, pos, n_pos_blocks, tm, with_bf16, norm_g, w_in_bf16, q_norm_g, k_norm_g, lb_logits):
    rows, d_model = x2d.shape
    d_in = w_in_bf16.shape[1]
    cos, sa, sb = _rope_tables(pos)
    qg = jnp.concatenate([q_norm_g, q_norm_g]).reshape(1, LANES)
    kg = jnp.concatenate([k_norm_g, k_norm_g]).reshape(1, LANES)
    const = lambda i: (0, 0)
    row_blk = lambda i: (i, 0)
    pos_blk = lambda i: (i % n_pos_blocks, 0)
    out_f32 = jax.ShapeDtypeStruct((rows, SEG), F32)
    out_shape = [out_f32] * 8
    out_specs = [pl.BlockSpec((tm, SEG), row_blk)] * 8
    for i in (1, 2):
        out_shape[i] = jax.ShapeDtypeStruct((rows * A_HEADS, LANES), F32)
        out_specs[i] = pl.BlockSpec((tm * A_HEADS, LANES), row_blk)
    if with_bf16:
        out_shape += [jax.ShapeDtypeStruct((rows, SEG), BF16)] * 2
        out_specs += [pl.BlockSpec((tm, SEG), row_blk)] * 2
    return pl.pallas_call(
        _inproj_kernel,
        grid=(rows // tm,),
        in_specs=[
            pl.BlockSpec((tm, d_model), row_blk),
            pl.BlockSpec((1, d_model), const),
            pl.BlockSpec((d_model, d_in), const),
            pl.BlockSpec((1, LANES), const),
            pl.BlockSpec((1, LANES), const),
            pl.BlockSpec((tm, LANES), pos_blk),
            pl.BlockSpec((tm, LANES), pos_blk),
            pl.BlockSpec((tm, LANES), pos_blk),
            pl.BlockSpec((2, SEG), const),
            pl.BlockSpec((2 * LANES, LANES), const),
        ],
        out_specs=out_specs,
        out_shape=out_shape,
        compiler_params=pltpu.CompilerParams(dimension_semantics=("parallel",), vmem_limit_bytes=VMEM_LIMIT),
        name="inproj",
    )(x2d, norm_g.reshape(1, d_model), w_in_bf16, qg, kg, cos, sa, sb, lb_logits, _group_mean_matrix())


def _stack_maps(q_bf16):
    lane = lax.broadcasted_iota(jnp.int32, q_bf16.shape, 1)
    zero = jnp.zeros_like(q_bf16)
    return jnp.concatenate([jnp.where(lane < A_DQK, q_bf16, zero), jnp.where(lane >= A_DQK, q_bf16, zero)], axis=0)


def _softmax_block_update(qs_parts, k, v1, masks, m_ref, l_ref, acc_ref):
    width = k.shape[0]
    n = qs_parts[0].shape[0]
    scores = [lax.dot_general(qs, k, (((1,), (1,)), ((), ())), preferred_element_type=F32) for qs in qs_parts]
    probs, alphas, m_news = [], [], []
    for i, s in enumerate(scores):
        if masks is not None:
            s = jnp.where(masks[i], s, -jnp.inf)
        m_prev = m_ref[i * n:(i + 1) * n]
        m_new = jnp.maximum(m_prev, jnp.max(s, axis=1, keepdims=True))
        alphas.append(jnp.exp2(m_prev - m_new))
        probs.append(jnp.exp2(s - jnp.concatenate([m_new] * (width // LANES), axis=1)).astype(BF16))
        m_news.append(m_new)
    for i, p in enumerate(probs):
        rows = slice(i * n, (i + 1) * n)
        pv = jnp.dot(p, v1, preferred_element_type=F32)
        acc_ref[rows] = alphas[i] * acc_ref[rows] + pv[:, :A_DV]
        l_ref[rows] = alphas[i] * l_ref[rows] + pv[:, A_DV:]
        m_ref[rows] = m_news[i]


def _combine_maps(l0, acc0, l1, acc1, lam):
    return acc0 * (1.0 / l0) - lam * (acc1 * (1.0 / l1))


def _bounded_block_update(qs_parts, k, v1, masks, ref_ref, l_ref, acc_ref):
    width = k.shape[0]
    n = qs_parts[0].shape[0]
    scores = [lax.dot_general(qs, k, (((1,), (1,)), ((), ())), preferred_element_type=F32) for qs in qs_parts]
    for i, s in enumerate(scores):
        rows = slice(i * n, (i + 1) * n)
        if masks is not None:
            s = jnp.where(masks[i], s, -jnp.inf)
        p = jnp.exp2(s - jnp.concatenate([ref_ref[rows]] * (width // LANES), axis=1)).astype(BF16)
        pv = jnp.dot(p, v1, preferred_element_type=F32)
        acc_ref[rows] = acc_ref[rows] + pv[:, :A_DV]
        l_ref[rows] = l_ref[rows] + pv[:, A_DV:]


def _map_sum_matrix():
    g = np.zeros((LANES, 2 * LANES), np.float32)
    g[:A_DQK, :LANES] = 1.0
    g[A_DQK:, LANES:] = 1.0
    return jnp.asarray(g, BF16)


def _pattn_kernel(q_ref, k_ref, v_ref, lam_ref, ms_ref, o_ref, v1_ref, kn_ref, m_ref, l_ref, acc_ref, *, blk):
    qi = pl.program_id(2)

    @pl.when(qi == 0)
    def _():
        v1_ref[:, :A_DV] = v_ref[...]
        v1_ref[:, A_DV:] = jnp.ones((v1_ref.shape[0], LANES), BF16)
        k = k_ref[...]
        kn_ref[...] = jnp.max(jnp.dot(k * k, ms_ref[...], preferred_element_type=F32), axis=0, keepdims=True)

    q = q_ref[...].astype(BF16)
    qs = _stack_maps(q)
    part = 2 * blk // ATTN_CHAINS
    qs_parts = [qs[i * part:(i + 1) * part] for i in range(ATTN_CHAINS)]
    row = lax.broadcasted_iota(jnp.int32, (part, blk), 0)
    col = lax.broadcasted_iota(jnp.int32, (part, blk), 1)
    diag_masks = [col <= row + (i * part) % blk for i in range(ATTN_CHAINS)]
    l_ref[...] = jnp.zeros(l_ref.shape, F32)
    acc_ref[...] = jnp.zeros(acc_ref.shape, F32)

    qn = jnp.dot(q * q, ms_ref[...], preferred_element_type=F32)
    kn = kn_ref[...]
    bound = jnp.concatenate([jnp.sqrt(qn[:, :LANES] * kn[:, :LANES]), jnp.sqrt(qn[:, LANES:] * kn[:, LANES:])],
                            axis=0) * 1.02
    m_ref[...] = bound
    bounded = jnp.max(bound) <= SCORE_BOUND_LIMIT

    def run(update, state_ref, unroll):
        def block(kb, masks):
            start = pl.multiple_of(kb * blk, blk)
            update(qs_parts, k_ref[pl.ds(start, blk), :], v1_ref[pl.ds(start, blk), :], masks,
                   state_ref, l_ref, acc_ref)

        def body(t, carry):
            for u in range(unroll):
                block(unroll * t + u, None)
            return carry

        lax.fori_loop(0, qi // unroll, body, 0)
        for u in range(1, unroll):
            @pl.when(qi % unroll >= u)
            def _():
                block(qi - (qi % unroll) + u - 1, None)
        block(qi, diag_masks)
        o_ref[...] = _combine_maps(l_ref[:blk], acc_ref[:blk], l_ref[blk:], acc_ref[blk:], _lam_from_ref(lam_ref))

    @pl.when(bounded)
    def _():
        run(_bounded_block_update, m_ref, 2)

    @pl.when(jnp.logical_not(bounded))
    def _():
        m_ref[...] = jnp.full(m_ref.shape, -jnp.inf, F32)
        run(_softmax_block_update, m_ref, 1)


def _prompt_attention(q, kb16, vb16, lam_params, batch, seq, blk):
    nq = seq // blk
    return pl.pallas_call(
        functools.partial(_pattn_kernel, blk=blk),
        grid=(batch, A_HEADS, nq),
        in_specs=[
            pl.BlockSpec((blk, LANES), lambda b, h, i: (b * nq + i, h)),
            pl.BlockSpec((seq, LANES), lambda b, h, i: (b, h)),
            pl.BlockSpec((seq, LANES), lambda b, h, i: (b, h)),
            pl.BlockSpec((4, A_DQK), lambda b, h, i: (0, 0)),
            pl.BlockSpec((LANES, 2 * LANES), lambda b, h, i: (0, 0)),
        ],
        out_specs=pl.BlockSpec((blk, LANES), lambda b, h, i: (b * nq + i, h)),
        out_shape=jax.ShapeDtypeStruct((batch * seq, A_HEADS * A_DV), F32),
        scratch_shapes=[
            pltpu.VMEM((seq, 2 * LANES), BF16),
            pltpu.VMEM((1, 2 * LANES), F32),
            pltpu.VMEM((2 * blk, LANES), F32),
            pltpu.VMEM((2 * blk, LANES), F32),
            pltpu.VMEM((2 * blk, A_DV), F32),
        ],
        compiler_params=pltpu.CompilerParams(
            dimension_semantics=("parallel", "parallel", "arbitrary"), vmem_limit_bytes=VMEM_LIMIT),
        name="prompt_attn",
    )(q, kb16, vb16, lam_params, _map_sum_matrix())


def _partial_softmax(qs, k, v, mask):
    s = lax.dot_general(qs, k, (((1,), (1,)), ((), ())), preferred_element_type=F32)
    s = jnp.where(mask, s, -jnp.inf)
    m = jnp.max(s, axis=1, keepdims=True)
    p = jnp.exp2(s - m)
    return m, jnp.sum(p, axis=1, keepdims=True), jnp.dot(p.astype(BF16), v, preferred_element_type=F32)


def _sattn_kernel(pt_ref, q_ref, kn_ref, vn_ref, lam_ref, *rest, n_pages, group, n_new):
    del pt_ref
    k_refs, v_refs, o_ref = rest[:n_pages], rest[n_pages:2 * n_pages], rest[2 * n_pages]
    q = q_ref[...].astype(BF16)
    qs = jnp.concatenate([_stack_maps(q[:, h * LANES:(h + 1) * LANES]) for h in range(A_HEADS)], axis=0)
    n_rows = 2 * n_new * A_HEADS

    def same_head(width):
        row = lax.broadcasted_iota(jnp.int32, (n_rows, width), 0)
        col = lax.broadcasted_iota(jnp.int32, (n_rows, width), 1)
        return row, col, (col % A_HEADS) == (row // (2 * n_new))

    n_kv = n_new * A_HEADS
    pad = jnp.zeros((LANES - n_kv, LANES), F32)
    row, col, ok = same_head(LANES)
    parts = [_partial_softmax(qs, jnp.concatenate([kn_ref[...], pad], axis=0).astype(BF16),
                              jnp.concatenate([vn_ref[...], pad], axis=0).astype(BF16),
                              ok & ((col // A_HEADS) <= (row % n_new)))]
    page_mask = same_head(group * k_refs[0].shape[0])[2]
    for g in range(0, n_pages, group):
        k = jnp.concatenate([r[...] for r in k_refs[g:g + group]], axis=0).astype(BF16)
        v = jnp.concatenate([r[...] for r in v_refs[g:g + group]], axis=0).astype(BF16)
        parts.append(_partial_softmax(qs, k, v, page_mask))
    m = functools.reduce(jnp.maximum, [p[0] for p in parts])
    weights = [jnp.exp2(p[0] - m) for p in parts]
    l = sum(w * p[1] for w, p in zip(weights, parts))
    acc = sum(w * p[2] for w, p in zip(weights, parts))
    lam = _lam_from_ref(lam_ref)
    for h in range(A_HEADS):
        r0 = h * 2 * n_new
        r1 = r0 + n_new
        o_ref[:, h * LANES:(h + 1) * LANES] = _combine_maps(
            l[r0:r1], acc[r0:r1], l[r1:r1 + n_new], acc[r1:r1 + n_new], lam)


def _sample_attention(q, k_new, v_new, cache_k, cache_v, page_table, lam_params, group):
    bs, n_pages = page_table.shape
    n_new = q.shape[0] // bs
    rows_per_page = cache_k.shape[1]
    row_blk = lambda b, pt: (b, 0)

    def page_spec(i):
        return pl.BlockSpec((None, rows_per_page, LANES), lambda b, pt: (pt[b * n_pages + i], 0, 0))

    grid_spec = pltpu.PrefetchScalarGridSpec(
        num_scalar_prefetch=1,
        grid=(bs,),
        in_specs=[
            pl.BlockSpec((n_new, q.shape[1]), row_blk),
            pl.BlockSpec((n_new * A_HEADS, LANES), row_blk),
            pl.BlockSpec((n_new * A_HEADS, LANES), row_blk),
            pl.BlockSpec((4, A_DQK), lambda b, pt: (0, 0)),
        ] + [page_spec(i) for i in range(n_pages)] * 2,
        out_specs=pl.BlockSpec((n_new, q.shape[1]), row_blk),
    )
    return pl.pallas_call(
        functools.partial(_sattn_kernel, n_pages=n_pages, group=group, n_new=n_new),
        grid_spec=grid_spec,
        out_shape=jax.ShapeDtypeStruct(q.shape, F32),
        compiler_params=pltpu.CompilerParams(dimension_semantics=("parallel",), vmem_limit_bytes=VMEM_LIMIT),
        name="sample_attn",
    )(page_table.reshape(-1), q, k_new, v_new, lam_params, *([cache_k] * n_pages), *([cache_v] * n_pages))


def _hgrn_kernel(q_ref, f_ref, v_ref, s0_ref, tri_ref, o_ref, s_ref, *, rows, seqs, chunks):
    @pl.when(pl.program_id(1) == 0)
    def _():
        s_ref[...] = s0_ref[...]

    n = HGRN_ROWS
    pad = n - rows
    r_idx = lax.broadcasted_iota(jnp.int32, (n, n), 0)
    c_idx = lax.broadcasted_iota(jnp.int32, (n, n), 1)
    causal = c_idx <= r_idx
    tri = tri_ref[...]
    units = [(e, h, c) for e in range(seqs) for h in range(H_HEADS) for c in range(chunks)]

    def rows_of(e, c):
        return slice((e * chunks + c) * rows, (e * chunks + c + 1) * rows)

    def lanes_of(h):
        return slice(h * LANES, (h + 1) * LANES)

    def padded(x, fill):
        return jnp.concatenate([x, jnp.full((pad, LANES), fill, F32)], axis=0) if pad else x

    logs = {}
    for (e, h, c) in units:
        hi, lo = _split_hi_lo(jnp.log(padded(f_ref[rows_of(e, c), lanes_of(h)], 1.0)))
        logs[e, h, c] = jnp.concatenate([hi, lo], axis=0)
    cums = {u: jnp.dot(tri, logs[u], preferred_element_type=F32) for u in units}

    q_end, k_end, vals, decay, att = {}, {}, {}, {}, {}
    for u in units:
        e, h, c = u
        b = cums[u]
        b_end = b[n - 1:n]
        b_mid = b[n // 2 - 1:n // 2]
        q = padded(q_ref[rows_of(e, c), lanes_of(h)], 0.0)
        k = 1.0 - padded(f_ref[rows_of(e, c), lanes_of(h)], 1.0)
        vals[u] = padded(v_ref[rows_of(e, c), lanes_of(h)], 0.0).astype(BF16)
        q_end[u] = (q * jnp.exp(b)).astype(BF16)
        k_end[u] = (k * jnp.exp(b_end - b)).astype(BF16)
        decay[u] = jnp.transpose(jnp.broadcast_to(jnp.exp(b_end), (8, LANES)))[:, 0:1]
        att[u] = lax.dot_general((q * jnp.exp(b - b_mid)).astype(BF16), (k * jnp.exp(b_mid - b)).astype(BF16),
                                 (((1,), (1,)), ((), ())), preferred_element_type=F32)
    kv = {u: lax.dot_general(k_end[u], vals[u], (((0,), (0,)), ((), ())), preferred_element_type=F32)
          for u in units}

    incoming = {}
    for e in range(seqs):
        for h in range(H_HEADS):
            state = s_ref[e, h]
            for c in range(chunks):
                incoming[e, h, c] = state.astype(BF16)
                state = decay[e, h, c] * state + kv[e, h, c]
            s_ref[e, h] = state
    for u in units:
        e, h, c = u
        lhs = jnp.concatenate([jnp.where(causal, att[u], 0.0).astype(BF16), q_end[u]], axis=1)
        o = jnp.dot(lhs, jnp.concatenate([vals[u], incoming[u]], axis=0), preferred_element_type=F32)
        o_ref[rows_of(e, c), lanes_of(h)] = o[:rows]


def _hgrn(hq, f, hi, s0, batch, rows, seqs, chunks):
    total = hq.shape[0]
    steps = total // (batch * rows * chunks)
    n = HGRN_ROWS
    tri = np.tril(np.ones((n, n), np.float32))
    tri = jnp.asarray(np.concatenate([tri, tri], axis=1), BF16)
    assert seqs == 1 or steps == 1
    blk = lambda b, c: (b * steps + c, 0)
    st = lambda b, c: (b, 0, 0, 0)
    return pl.pallas_call(
        functools.partial(_hgrn_kernel, rows=rows, seqs=seqs, chunks=chunks),
        grid=(batch // seqs, steps),
        in_specs=[
            pl.BlockSpec((seqs * chunks * rows, SEG), blk),
            pl.BlockSpec((seqs * chunks * rows, SEG), blk),
            pl.BlockSpec((seqs * chunks * rows, SEG), blk),
            pl.BlockSpec((seqs, H_HEADS, H_DK, H_DV), st),
            pl.BlockSpec((n, 2 * n), lambda b, c: (0, 0)),
        ],
        out_specs=[
            pl.BlockSpec((seqs * chunks * rows, SEG), blk),
            pl.BlockSpec((seqs, H_HEADS, H_DK, H_DV), st),
        ],
        out_shape=[
            jax.ShapeDtypeStruct((total, SEG), F32),
            jax.ShapeDtypeStruct((batch, H_HEADS, H_DK, H_DV), F32),
        ],
        compiler_params=pltpu.CompilerParams(
            dimension_semantics=("parallel", "arbitrary"), vmem_limit_bytes=VMEM_LIMIT),
        name="hgrn2",
    )(hq, f, hi, s0, tri)


def _merge_kernel(x_ref, oa_ref, ga_ref, oh_ref, gh_ref, sg_ref, hg_ref, w_ref, y_ref):
    def head_norm(o_ref, g_ref, h):
        o = o_ref[:, h * LANES:(h + 1) * LANES]
        return o * lax.rsqrt(jnp.mean(o * o, axis=-1, keepdims=True) + EPS) * g_ref[...]

    a = jnp.concatenate([head_norm(oa_ref, sg_ref, h) for h in range(A_HEADS)], axis=1)
    a = a * (1.0 - LAM_INIT) * ga_ref[...]
    r = jnp.concatenate([head_norm(oh_ref, hg_ref, h) for h in range(H_HEADS)], axis=1)
    r = r * gh_ref[...]
    mix = jnp.concatenate([a, r], axis=1).astype(BF16)
    y_ref[...] = x_ref[...] + jnp.dot(mix, w_ref[...], preferred_element_type=F32)


def _merge(x2d, o_attn, ga, o_hgrn, gh, subln_g, hgrn_norm_g, w_out_bf16, tm):
    rows, d_model = x2d.shape
    row_blk = lambda i: (i, 0)
    const = lambda i: (0, 0)
    return pl.pallas_call(
        _merge_kernel,
        grid=(rows // tm,),
        in_specs=[
            pl.BlockSpec((tm, d_model), row_blk),
            pl.BlockSpec((tm, SEG), row_blk),
            pl.BlockSpec((tm, SEG), row_blk),
            pl.BlockSpec((tm, SEG), row_blk),
            pl.BlockSpec((tm, SEG), row_blk),
            pl.BlockSpec((1, LANES), const),
            pl.BlockSpec((1, LANES), const),
            pl.BlockSpec(w_out_bf16.shape, const),
        ],
        out_specs=pl.BlockSpec((tm, d_model), row_blk),
        out_shape=jax.ShapeDtypeStruct((rows, d_model), F32),
        compiler_params=pltpu.CompilerParams(dimension_semantics=("parallel",), vmem_limit_bytes=VMEM_LIMIT),
        name="merge",
    )(x2d, o_attn, ga, o_hgrn, gh, subln_g.reshape(1, LANES), hgrn_norm_g.reshape(1, LANES), w_out_bf16)


def kernel(x_prompt, x_sample, cache_k, cache_v, state_hgrn, page_table, norm_g, w_in, q_norm_g, k_norm_g,
           lambda_q1, lambda_k1, lambda_q2, lambda_k2, subln_g, hgrn_lb_logits, hgrn_norm_g, w_out):
    bp, tp, d_model = x_prompt.shape
    bs, ts, _ = x_sample.shape
    depth, n_pool, page_size = cache_k.shape[:3]
    assert depth == 1 and hgrn_lb_logits.shape[0] == 2
    assert (A_DV, H_DK, H_DV, 2 * A_DQK) == (LANES,) * 4
    past_len = page_table.shape[1] * page_size
    tm = 256
    attn_blk = 512

    w_in16 = w_in[0].astype(BF16)
    w_out16 = w_out[0].astype(BF16)
    lam_params = jnp.stack([lambda_q1[0], lambda_k1[0], lambda_q2[0], lambda_k2[0]])
    proj = functools.partial(_inproj, norm_g=norm_g[0], w_in_bf16=w_in16, q_norm_g=q_norm_g[0],
                             k_norm_g=k_norm_g[0], lb_logits=hgrn_lb_logits)
    fin = functools.partial(_merge, subln_g=subln_g[0], hgrn_norm_g=hgrn_norm_g[0], w_out_bf16=w_out16, tm=tm)

    xp = x_prompt.reshape(bp * tp, d_model)
    q, k, v, ga, hq, f, hi, gh, k16, v16 = proj(xp, jnp.arange(tp, dtype=jnp.int32), tp // tm, tm, True)
    o_attn = _prompt_attention(q, k16, v16, lam_params, bp, tp, attn_blk)
    o_hgrn, s_p = _hgrn(hq, f, hi, jnp.zeros((bp, H_HEADS, H_DK, H_DV), F32), bp, HGRN_ROWS, seqs=1, chunks=4)
    y_p = fin(xp, o_attn, ga, o_hgrn, gh)

    xs = x_sample.reshape(bs * ts, d_model)
    pos_s = past_len + jnp.tile(jnp.arange(ts, dtype=jnp.int32), tm // ts)
    sq, sk, sv, sga, shq, sf, shi, sgh = proj(xs, pos_s, 1, tm, False)
    ck = cache_k.reshape(n_pool, page_size * A_HEADS, 2 * A_DQK)
    cv = cache_v.reshape(n_pool, page_size * A_HEADS, A_DV)
    so_attn = _sample_attention(sq, sk, sv, ck, cv, page_table, lam_params, group=4)
    so_hgrn, s_s = _hgrn(shq, sf, shi, state_hgrn[0], bs, ts, seqs=8, chunks=1)
    y_s = fin(xs, so_attn, sga, so_hgrn, sgh)

    return (y_p.reshape(bp, tp, d_model), y_s.reshape(bs, ts, d_model),
            k.reshape(1, bp, tp, A_HEADS, 2 * A_DQK), v.reshape(1, bp, tp, A_HEADS, A_DV), s_p[None],
            sk.reshape(1, bs, ts, A_HEADS, 2 * A_DQK), sv.reshape(1, bs, ts, A_HEADS, A_DV), s_s[None])
```

```python
import functools
import math

import numpy as np
import jax
import jax.numpy as jnp
from jax import lax
from jax.experimental import pallas as pl
from jax.experimental.pallas import tpu as pltpu

F32 = jnp.float32
BF16 = jnp.bfloat16

LANES = 128
SUBLANES = 8
BF16_ROWS = 16
A_HEADS = 4
A_DQK = 64
A_DV = 128
H_HEADS = 4
H_DK = 128
H_DV = 128
SEG = 512
ROT_DIM = A_DQK // 4
ROPE_THETA = 500000.0
EPS = 1e-6
LAM_INIT = 0.8 - 0.6 * math.exp(-0.3 * 0)
Q_SCALE = A_DQK ** -0.5 * math.log2(math.e)
HGRN_ROWS = 128
SCORE_BOUND_LIMIT = 60.0
ATTN_CHAINS = 4
VMEM_LIMIT = 48 * 1024 * 1024


def _sigmoid(x):
    return 1.0 / (1.0 + jnp.exp(-x))


def _split_hi_lo(x):
    hi = x.astype(BF16)
    lo = (x - hi.astype(F32)).astype(BF16)
    return hi, lo


def _lam_from_ref(lam_ref):
    lp = lam_ref[...]
    s1 = jnp.sum(lp[0:1] * lp[1:2], axis=1, keepdims=True)
    s2 = jnp.sum(lp[2:3] * lp[3:4], axis=1, keepdims=True)
    return jnp.exp(s1) - jnp.exp(s2) + LAM_INIT


def _inproj_kernel(x_ref, ng_ref, w_ref, qg_ref, kg_ref, cos_ref, sa_ref, sb_ref, lbl_ref, gm_ref,
                   q_ref, k_ref, v_ref, ga_ref, hq_ref, f_ref, hi_ref, gh_ref, *maybe_bf16_refs):
    x = x_ref[...]
    ms = jnp.mean(x * x, axis=-1, keepdims=True)
    h = (x * lax.rsqrt(ms + EPS) * ng_ref[...]).astype(BF16)

    def proj(seg):
        return jnp.dot(h, w_ref[:, seg * SEG:(seg + 1) * SEG], preferred_element_type=F32)

    cos, sa, sb = cos_ref[...], sa_ref[...], sb_ref[...]

    def norm_rope(y, g_ref, hh):
        pair = y[:, (hh // 2) * 2 * LANES:(hh // 2 + 1) * 2 * LANES]
        gms = jnp.dot((pair * pair).astype(BF16), gm_ref[...], preferred_element_type=F32)
        own = slice((hh % 2) * LANES, (hh % 2 + 1) * LANES)
        yn = pair[:, own] * lax.rsqrt(gms[:, own] + EPS) * g_ref[...]
        return yn * cos + pltpu.roll(yn, LANES - ROT_DIM // 2, 1) * sa + pltpu.roll(yn, ROT_DIM // 2, 1) * sb

    tm = x.shape[0]
    aq = proj(0)
    for hh in range(A_HEADS):
        q_ref[:, hh * LANES:(hh + 1) * LANES] = norm_rope(aq, qg_ref, hh) * Q_SCALE
    ak = proj(1)
    for hh in range(A_HEADS):
        kh = norm_rope(ak, kg_ref, hh)
        k_ref[pl.ds(hh, tm, stride=A_HEADS), :] = kh
        if maybe_bf16_refs:
            maybe_bf16_refs[0][:, hh * LANES:(hh + 1) * LANES] = kh.astype(BF16)
    av = proj(2)
    for hh in range(A_HEADS):
        v_ref[pl.ds(hh, tm, stride=A_HEADS), :] = av[:, hh * LANES:(hh + 1) * LANES]
    if maybe_bf16_refs:
        maybe_bf16_refs[1][...] = av.astype(BF16)
    ag = proj(3)
    ga_ref[...] = ag * _sigmoid(ag)
    hq = proj(4)
    hq_ref[...] = hq * _sigmoid(hq)
    hf = proj(5)
    lbl = lbl_ref[...]
    l0, l1 = lbl[0:1], lbl[1:2]
    mx = jnp.maximum(l0, l1)
    e0, e1 = jnp.exp(l0 - mx), jnp.exp(l1 - mx)
    lb = e0 / (e0 + e1)
    f_ref[...] = lb + (1.0 - lb) * _sigmoid(hf)
    hi_ref[...] = proj(6)
    hg = proj(7)
    gh_ref[...] = hg * _sigmoid(hg)


def _rope_tables(pos):
    half = ROT_DIM // 2
    inv = ROPE_THETA ** (-jnp.arange(0, ROT_DIM, 2, dtype=F32) / ROT_DIM)
    lane = np.arange(LANES) % A_DQK
    ang = pos.astype(F32)[:, None] * inv[lane % half][None, :]
    cos = jnp.where((lane < ROT_DIM)[None, :], jnp.cos(ang), 1.0)
    sin = jnp.sin(ang)
    a = jnp.where((lane < half)[None, :], -sin, 0.0)
    b = jnp.where(((lane >= half) & (lane < ROT_DIM))[None, :], sin, 0.0)
    return cos, a, b


def _group_mean_matrix():
    g = np.zeros((2 * LANES, 2 * LANES), np.float32)
    for s in range(0, 2 * LANES, A_DQK):
        g[s:s + A_DQK, s:s + A_DQK] = 1.0 / A_DQK
    return jnp.asarray(g, BF16)


def _inproj(x2d, pos, n_pos_blocks, tm, with_bf16, norm_g, w_in_bf16, q_norm_g, k_norm_g, lb_logits):
    rows, d_model = x2d.shape
    d_in = w_in_bf16.shape[1]
    cos, sa, sb = _rope_tables(pos)
    qg = jnp.concatenate([q_norm_g, q_norm_g]).reshape(1, LANES)
    kg = jnp.concatenate([k_norm_g, k_norm_g]).reshape(1, LANES)
    const = lambda i: (0, 0)
    row_blk = lambda i: (i, 0)
    pos_blk = lambda i: (i % n_pos_blocks, 0)
    out_f32 = jax.ShapeDtypeStruct((rows, SEG), F32)
    out_shape = [out_f32] * 8
    out_specs = [pl.BlockSpec((tm, SEG), row_blk)] * 8
    for i in (1, 2):
        out_shape[i] = jax.ShapeDtypeStruct((rows * A_HEADS, LANES), F32)
        out_specs[i] = pl.BlockSpec((tm * A_HEADS, LANES), row_blk)
    if with_bf16:
        out_shape += [jax.ShapeDtypeStruct((rows, SEG), BF16)] * 2
        out_specs += [pl.BlockSpec((tm, SEG), row_blk)] * 2
    return pl.pallas_call(
        _inproj_kernel,
        grid=(rows // tm,),
        in_specs=[
            pl.BlockSpec((tm, d_model), row_blk),
            pl.BlockSpec((1, d_model), const),
            pl.BlockSpec((d_model, d_in), const, pipeline_mode=pl.Buffered(1)),
            pl.BlockSpec((1, LANES), const),
            pl.BlockSpec((1, LANES), const),
            pl.BlockSpec((tm, LANES), pos_blk),
            pl.BlockSpec((tm, LANES), pos_blk),
            pl.BlockSpec((tm, LANES), pos_blk),
            pl.BlockSpec((2, SEG), const),
            pl.BlockSpec((2 * LANES, 2 * LANES), const),
        ],
        out_specs=out_specs,
        out_shape=out_shape,
        compiler_params=pltpu.CompilerParams(dimension_semantics=("parallel",), vmem_limit_bytes=VMEM_LIMIT),
        name="inproj",
    )(x2d, norm_g.reshape(1, d_model), w_in_bf16, qg, kg, cos, sa, sb, lb_logits, _group_mean_matrix())


def _stack_maps(q_bf16):
    lane = lax.broadcasted_iota(jnp.int32, q_bf16.shape, 1)
    zero = jnp.zeros_like(q_bf16)
    return jnp.concatenate([jnp.where(lane < A_DQK, q_bf16, zero), jnp.where(lane >= A_DQK, q_bf16, zero)], axis=0)


def _softmax_block_update(qs_parts, k, v1, masks, m_ref, l_ref, acc_ref):
    width = k.shape[0]
    n = qs_parts[0].shape[0]
    scores = [lax.dot_general(qs, k, (((1,), (1,)), ((), ())), preferred_element_type=F32) for qs in qs_parts]
    probs, alphas, m_news = [], [], []
    for i, s in enumerate(scores):
        if masks is not None:
            s = jnp.where(masks[i], s, -jnp.inf)
        m_prev = m_ref[i * n:(i + 1) * n]
        m_new = jnp.maximum(m_prev, jnp.max(s, axis=1, keepdims=True))
        alphas.append(jnp.exp2(m_prev - m_new))
        probs.append(jnp.exp2(s - jnp.concatenate([m_new] * (width // LANES), axis=1)).astype(BF16))
        m_news.append(m_new)
    for i, p in enumerate(probs):
        rows = slice(i * n, (i + 1) * n)
        pv = jnp.dot(p, v1, preferred_element_type=F32)
        acc_ref[rows] = alphas[i] * acc_ref[rows] + pv[:, :A_DV]
        l_ref[rows] = alphas[i] * l_ref[rows] + pv[:, A_DV:]
        m_ref[rows] = m_news[i]


def _combine_maps(l0, acc0, l1, acc1, lam):
    return acc0 * (1.0 / l0) - lam * (acc1 * (1.0 / l1))


def _map_sum_matrix():
    g = np.zeros((LANES, 2 * LANES), np.float32)
    g[:A_DQK, :LANES] = 1.0
    g[A_DQK:, LANES:] = 1.0
    return jnp.asarray(g, BF16)


def _map_rows_matrix():
    g = np.zeros((2 * SUBLANES, LANES), np.float32)
    g[:SUBLANES, :A_DQK] = 1.0
    g[SUBLANES:, A_DQK:] = 1.0
    return jnp.asarray(g, BF16)


def _transposed_block_update(qs_parts, k, vt, masks, ref_ref, acc_ref):
    n = qs_parts[0].shape[0]
    scores = [lax.dot_general(k, qs, (((1,), (1,)), ((), ())), preferred_element_type=F32) for qs in qs_parts]
    for i, st in enumerate(scores):
        cols = slice(i * n, (i + 1) * n)
        if masks is not None:
            st = jnp.where(masks[i], st, -jnp.inf)
        p = jnp.exp2(st - ref_ref[0:1, cols]).astype(BF16)
        acc_ref[:, cols] = acc_ref[:, cols] + jnp.dot(vt, p, preferred_element_type=F32)


def _pattn_kernel(q_ref, k_ref, v_ref, lam_ref, ms_ref, mr_ref, o_ref,
                  v1_ref, vt_ref, kn_ref, m_ref, l_ref, acc_ref, mt_ref, acct_ref, *, blk):
    qi = pl.program_id(2)
    seq = k_ref.shape[0]

    @pl.when(qi == 0)
    def _():
        v1_ref[:, :A_DV] = v_ref[...]
        v1_ref[:, A_DV:] = jnp.ones((seq, LANES), BF16)
        for c in range(seq // blk):
            vt_ref[:A_DV, c * blk:(c + 1) * blk] = jnp.transpose(v_ref[c * blk:(c + 1) * blk, :])
        vt_ref[A_DV:, :] = jnp.ones((vt_ref.shape[0] - A_DV, seq), BF16)
        k = k_ref[...]
        kn_ref[...] = jnp.max(jnp.dot(k * k, ms_ref[...], preferred_element_type=F32), axis=0, keepdims=True)

    q = q_ref[...].astype(BF16)
    qs = _stack_maps(q)
    part = 2 * blk // ATTN_CHAINS
    qs_parts = [qs[i * part:(i + 1) * part] for i in range(ATTN_CHAINS)]

    qn = lax.dot_general(mr_ref[...], q * q, (((1,), (1,)), ((), ())), preferred_element_type=F32)
    kn = kn_ref[...]
    kn0 = jnp.concatenate([kn[:, :LANES]] * (blk // LANES), axis=1)
    kn1 = jnp.concatenate([kn[:, LANES:]] * (blk // LANES), axis=1)
    bound = jnp.concatenate([jnp.sqrt(qn[:SUBLANES] * kn0), jnp.sqrt(qn[SUBLANES:] * kn1)], axis=1) * 1.02
    mt_ref[...] = bound
    bounded = jnp.max(bound) <= SCORE_BOUND_LIMIT

    def run(block, unroll):
        def body(t, carry):
            for u in range(unroll):
                block(unroll * t + u, False)
            return carry

        lax.fori_loop(0, qi // unroll, body, 0)
        for u in range(1, unroll):
            @pl.when(qi % unroll >= u)
            def _():
                block(qi - (qi % unroll) + u - 1, False)
        block(qi, True)

    lam = _lam_from_ref(lam_ref)

    @pl.when(bounded)
    def _():
        acct_ref[...] = jnp.zeros(acct_ref.shape, F32)
        row = lax.broadcasted_iota(jnp.int32, (blk, part), 0)
        col = lax.broadcasted_iota(jnp.int32, (blk, part), 1)
        diag_masks = [row <= col + (i * part) % blk for i in range(ATTN_CHAINS)]

        def block(kb, diagonal):
            start = pl.multiple_of(kb * blk, blk)
            _transposed_block_update(qs_parts, k_ref[pl.ds(start, blk), :], vt_ref[:, pl.ds(start, blk)],
                                     diag_masks if diagonal else None, mt_ref, acct_ref)

        run(block, 2)
        acc = acct_ref[...]
        out_t = _combine_maps(acc[A_DV:A_DV + 1, :blk], acc[:A_DV, :blk], acc[A_DV:A_DV + 1, blk:], acc[:A_DV, blk:], lam)
        o_ref[...] = jnp.transpose(out_t)

    @pl.when(jnp.logical_not(bounded))
    def _():
        m_ref[...] = jnp.full(m_ref.shape, -jnp.inf, F32)
        l_ref[...] = jnp.zeros(l_ref.shape, F32)
        acc_ref[...] = jnp.zeros(acc_ref.shape, F32)
        row = lax.broadcasted_iota(jnp.int32, (part, blk), 0)
        col = lax.broadcasted_iota(jnp.int32, (part, blk), 1)
        diag_masks = [col <= row + (i * part) % blk for i in range(ATTN_CHAINS)]

        def block(kb, diagonal):
            start = pl.multiple_of(kb * blk, blk)
            _softmax_block_update(qs_parts, k_ref[pl.ds(start, blk), :], v1_ref[pl.ds(start, blk), :],
                                  diag_masks if diagonal else None, m_ref, l_ref, acc_ref)

        run(block, 1)
        o_ref[...] = _combine_maps(l_ref[:blk], acc_ref[:blk], l_ref[blk:], acc_ref[blk:], lam)


def _prompt_attention(q, kb16, vb16, lam_params, batch, seq, blk):
    nq = seq // blk
    const = lambda b, h, i: (0, 0)
    return pl.pallas_call(
        functools.partial(_pattn_kernel, blk=blk),
        grid=(batch, A_HEADS, nq),
        in_specs=[
            pl.BlockSpec((blk, LANES), lambda b, h, i: (b * nq + i, h)),
            pl.BlockSpec((seq, LANES), lambda b, h, i: (b, h)),
            pl.BlockSpec((seq, LANES), lambda b, h, i: (b, h)),
            pl.BlockSpec((4, A_DQK), const),
            pl.BlockSpec((LANES, 2 * LANES), const),
            pl.BlockSpec((2 * SUBLANES, LANES), const),
        ],
        out_specs=pl.BlockSpec((blk, LANES), lambda b, h, i: (b * nq + i, h)),
        out_shape=jax.ShapeDtypeStruct((batch * seq, A_HEADS * A_DV), F32),
        scratch_shapes=[
            pltpu.VMEM((seq, 2 * LANES), BF16),
            pltpu.VMEM((A_DV + BF16_ROWS, seq), BF16),
            pltpu.VMEM((1, 2 * LANES), F32),
            pltpu.VMEM((2 * blk, LANES), F32),
            pltpu.VMEM((2 * blk, LANES), F32),
            pltpu.VMEM((2 * blk, A_DV), F32),
            pltpu.VMEM((SUBLANES, 2 * blk), F32),
            pltpu.VMEM((A_DV + BF16_ROWS, 2 * blk), F32),
        ],
        compiler_params=pltpu.CompilerParams(
            dimension_semantics=("parallel", "parallel", "arbitrary"), vmem_limit_bytes=VMEM_LIMIT),
        name="prompt_attn",
    )(q, kb16, vb16, lam_params, _map_sum_matrix(), _map_rows_matrix())


def _partial_softmax(qs, k, v, mask):
    s = lax.dot_general(qs, k, (((1,), (1,)), ((), ())), preferred_element_type=F32)
    s = jnp.where(mask, s, -jnp.inf)
    m = jnp.max(s, axis=1, keepdims=True)
    p = jnp.exp2(s - m)
    return m, jnp.sum(p, axis=1, keepdims=True), jnp.dot(p.astype(BF16), v, preferred_element_type=F32)


def _sattn_kernel(pt_ref, q_ref, kn_ref, vn_ref, lam_ref, *rest, n_pages, group, n_new):
    del pt_ref
    k_refs, v_refs, o_ref = rest[:n_pages], rest[n_pages:2 * n_pages], rest[2 * n_pages]
    q = q_ref[...].astype(BF16)
    qs = jnp.concatenate([_stack_maps(q[:, h * LANES:(h + 1) * LANES]) for h in range(A_HEADS)], axis=0)
    n_rows = 2 * n_new * A_HEADS

    def same_head(width):
        row = lax.broadcasted_iota(jnp.int32, (n_rows, width), 0)
        col = lax.broadcasted_iota(jnp.int32, (n_rows, width), 1)
        return row, col, (col % A_HEADS) == (row // (2 * n_new))

    n_kv = n_new * A_HEADS
    pad = jnp.zeros((LANES - n_kv, LANES), F32)
    row, col, ok = same_head(LANES)
    parts = [_partial_softmax(qs, jnp.concatenate([kn_ref[...], pad], axis=0).astype(BF16),
                              jnp.concatenate([vn_ref[...], pad], axis=0).astype(BF16),
                              ok & ((col // A_HEADS) <= (row % n_new)))]
    page_mask = same_head(group * k_refs[0].shape[0])[2]
    for g in range(0, n_pages, group):
        k = jnp.concatenate([r[...] for r in k_refs[g:g + group]], axis=0).astype(BF16)
        v = jnp.concatenate([r[...] for r in v_refs[g:g + group]], axis=0).astype(BF16)
        parts.append(_partial_softmax(qs, k, v, page_mask))
    m = functools.reduce(jnp.maximum, [p[0] for p in parts])
    weights = [jnp.exp2(p[0] - m) for p in parts]
    l = sum(w * p[1] for w, p in zip(weights, parts))
    acc = sum(w * p[2] for w, p in zip(weights, parts))
    lam = _lam_from_ref(lam_ref)
    for h in range(A_HEADS):
        r0 = h * 2 * n_new
        r1 = r0 + n_new
        o_ref[:, h * LANES:(h + 1) * LANES] = _combine_maps(
            l[r0:r1], acc[r0:r1], l[r1:r1 + n_new], acc[r1:r1 + n_new], lam)


def _sample_attention(q, k_new, v_new, cache_k, cache_v, page_table, lam_params, group):
    bs, n_pages = page_table.shape
    n_new = q.shape[0] // bs
    rows_per_page = cache_k.shape[1]
    row_blk = lambda b, pt: (b, 0)

    def page_spec(i):
        return pl.BlockSpec((None, rows_per_page, LANES), lambda b, pt: (pt[b * n_pages + i], 0, 0))

    grid_spec = pltpu.PrefetchScalarGridSpec(
        num_scalar_prefetch=1,
        grid=(bs,),
        in_specs=[
            pl.BlockSpec((n_new, q.shape[1]), row_blk),
            pl.BlockSpec((n_new * A_HEADS, LANES), row_blk),
            pl.BlockSpec((n_new * A_HEADS, LANES), row_blk),
            pl.BlockSpec((4, A_DQK), lambda b, pt: (0, 0)),
        ] + [page_spec(i) for i in range(n_pages)] * 2,
        out_specs=pl.BlockSpec((n_new, q.shape[1]), row_blk),
    )
    return pl.pallas_call(
        functools.partial(_sattn_kernel, n_pages=n_pages, group=group, n_new=n_new),
        grid_spec=grid_spec,
        out_shape=jax.ShapeDtypeStruct(q.shape, F32),
        compiler_params=pltpu.CompilerParams(dimension_semantics=("parallel",), vmem_limit_bytes=VMEM_LIMIT),
        name="sample_attn",
    )(page_table.reshape(-1), q, k_new, v_new, lam_params, *([cache_k] * n_pages), *([cache_v] * n_pages))


def _hgrn_kernel(q_ref, f_ref, v_ref, s0_ref, tri_ref, o_ref, s_ref, *, rows, seqs, chunks):
    @pl.when(pl.program_id(1) == 0)
    def _():
        s_ref[...] = s0_ref[...]

    n = HGRN_ROWS
    pad = n - rows
    r_idx = lax.broadcasted_iota(jnp.int32, (n, n), 0)
    c_idx = lax.broadcasted_iota(jnp.int32, (n, n), 1)
    causal = c_idx <= r_idx
    tri = tri_ref[...]
    units = [(e, h, c) for e in range(seqs) for h in range(H_HEADS) for c in range(chunks)]

    def rows_of(e, c):
        return slice((e * chunks + c) * rows, (e * chunks + c + 1) * rows)

    def lanes_of(h):
        return slice(h * LANES, (h + 1) * LANES)

    def padded(x, fill):
        return jnp.concatenate([x, jnp.full((pad, LANES), fill, F32)], axis=0) if pad else x

    logs = {}
    for (e, h, c) in units:
        hi, lo = _split_hi_lo(jnp.log(padded(f_ref[rows_of(e, c), lanes_of(h)], 1.0)))
        logs[e, h, c] = jnp.concatenate([hi, lo], axis=0)
    cums = {u: jnp.dot(tri, logs[u], preferred_element_type=F32) for u in units}

    q_end, k_end, vals, decay, att = {}, {}, {}, {}, {}
    for u in units:
        e, h, c = u
        b = cums[u]
        b_end = b[n - 1:n]
        b_mid = b[n // 2 - 1:n // 2]
        q = padded(q_ref[rows_of(e, c), lanes_of(h)], 0.0)
        k = 1.0 - padded(f_ref[rows_of(e, c), lanes_of(h)], 1.0)
        vals[u] = padded(v_ref[rows_of(e, c), lanes_of(h)], 0.0).astype(BF16)
        q_end[u] = (q * jnp.exp(b)).astype(BF16)
        k_end[u] = (k * jnp.exp(b_end - b)).astype(BF16)
        decay[u] = jnp.transpose(jnp.broadcast_to(jnp.exp(b_end), (8, LANES)))[:, 0:1]
        att[u] = lax.dot_general((q * jnp.exp(b - b_mid)).astype(BF16), (k * jnp.exp(b_mid - b)).astype(BF16),
                                 (((1,), (1,)), ((), ())), preferred_element_type=F32)
    kv = {u: lax.dot_general(k_end[u], vals[u], (((0,), (0,)), ((), ())), preferred_element_type=F32)
          for u in units}

    incoming = {}
    for e in range(seqs):
        for h in range(H_HEADS):
            state = s_ref[e, h]
            for c in range(chunks):
                incoming[e, h, c] = state.astype(BF16)
                state = decay[e, h, c] * state + kv[e, h, c]
            s_ref[e, h] = state
    for u in units:
        e, h, c = u
        lhs = jnp.concatenate([jnp.where(causal, att[u], 0.0).astype(BF16), q_end[u]], axis=1)
        o = jnp.dot(lhs, jnp.concatenate([vals[u], incoming[u]], axis=0), preferred_element_type=F32)
        o_ref[rows_of(e, c), lanes_of(h)] = o[:rows]


def _hgrn(hq, f, hi, s0, batch, rows, seqs, chunks):
    total = hq.shape[0]
    steps = total // (batch * rows * chunks)
    n = HGRN_ROWS
    tri = np.tril(np.ones((n, n), np.float32))
    tri = jnp.asarray(np.concatenate([tri, tri], axis=1), BF16)
    assert seqs == 1 or steps == 1
    blk = lambda b, c: (b * steps + c, 0)
    st = lambda b, c: (b, 0, 0, 0)
    return pl.pallas_call(
        functools.partial(_hgrn_kernel, rows=rows, seqs=seqs, chunks=chunks),
        grid=(batch // seqs, steps),
        in_specs=[
            pl.BlockSpec((seqs * chunks * rows, SEG), blk),
            pl.BlockSpec((seqs * chunks * rows, SEG), blk),
            pl.BlockSpec((seqs * chunks * rows, SEG), blk),
            pl.BlockSpec((seqs, H_HEADS, H_DK, H_DV), st),
            pl.BlockSpec((n, 2 * n), lambda b, c: (0, 0)),
        ],
        out_specs=[
            pl.BlockSpec((seqs * chunks * rows, SEG), blk),
            pl.BlockSpec((seqs, H_HEADS, H_DK, H_DV), st),
        ],
        out_shape=[
            jax.ShapeDtypeStruct((total, SEG), F32),
            jax.ShapeDtypeStruct((batch, H_HEADS, H_DK, H_DV), F32),
        ],
        compiler_params=pltpu.CompilerParams(
            dimension_semantics=("parallel", "arbitrary"), vmem_limit_bytes=VMEM_LIMIT),
        name="hgrn2",
    )(hq, f, hi, s0, tri)


def _merge_kernel(x_ref, oa_ref, ga_ref, oh_ref, gh_ref, sg_ref, hg_ref, w_ref, y_ref):
    def head_norm(o_ref, g_ref, h):
        o = o_ref[:, h * LANES:(h + 1) * LANES]
        return o * lax.rsqrt(jnp.mean(o * o, axis=-1, keepdims=True) + EPS) * g_ref[...]

    a = jnp.concatenate([head_norm(oa_ref, sg_ref, h) for h in range(A_HEADS)], axis=1)
    a = a * (1.0 - LAM_INIT) * ga_ref[...]
    r = jnp.concatenate([head_norm(oh_ref, hg_ref, h) for h in range(H_HEADS)], axis=1)
    r = r * gh_ref[...]
    mix = jnp.concatenate([a, r], axis=1).astype(BF16)
    y_ref[...] = x_ref[...] + jnp.dot(mix, w_ref[...], preferred_element_type=F32)


def _merge(x2d, o_attn, ga, o_hgrn, gh, subln_g, hgrn_norm_g, w_out_bf16, tm):
    rows, d_model = x2d.shape
    row_blk = lambda i: (i, 0)
    const = lambda i: (0, 0)
    return pl.pallas_call(
        _merge_kernel,
        grid=(rows // tm,),
        in_specs=[
            pl.BlockSpec((tm, d_model), row_blk),
            pl.BlockSpec((tm, SEG), row_blk),
            pl.BlockSpec((tm, SEG), row_blk),
            pl.BlockSpec((tm, SEG), row_blk),
            pl.BlockSpec((tm, SEG), row_blk),
            pl.BlockSpec((1, LANES), const),
            pl.BlockSpec((1, LANES), const),
            pl.BlockSpec(w_out_bf16.shape, const, pipeline_mode=pl.Buffered(1)),
        ],
        out_specs=pl.BlockSpec((tm, d_model), row_blk),
        out_shape=jax.ShapeDtypeStruct((rows, d_model), F32),
        compiler_params=pltpu.CompilerParams(dimension_semantics=("parallel",), vmem_limit_bytes=VMEM_LIMIT),
        name="merge",
    )(x2d, o_attn, ga, o_hgrn, gh, subln_g.reshape(1, LANES), hgrn_norm_g.reshape(1, LANES), w_out_bf16)


def kernel(x_prompt, x_sample, cache_k, cache_v, state_hgrn, page_table, norm_g, w_in, q_norm_g, k_norm_g,
           lambda_q1, lambda_k1, lambda_q2, lambda_k2, subln_g, hgrn_lb_logits, hgrn_norm_g, w_out):
    bp, tp, d_model = x_prompt.shape
    bs, ts, _ = x_sample.shape
    depth, n_pool, page_size = cache_k.shape[:3]
    assert depth == 1 and hgrn_lb_logits.shape[0] == 2
    assert (A_DV, H_DK, H_DV, 2 * A_DQK) == (LANES,) * 4
    past_len = page_table.shape[1] * page_size
    tm = 512
    attn_blk = 512

    w_in16 = w_in[0].astype(BF16)
    w_out16 = w_out[0].astype(BF16)
    lam_params = jnp.stack([lambda_q1[0], lambda_k1[0], lambda_q2[0], lambda_k2[0]])
    proj = functools.partial(_inproj, norm_g=norm_g[0], w_in_bf16=w_in16, q_norm_g=q_norm_g[0],
                             k_norm_g=k_norm_g[0], lb_logits=hgrn_lb_logits)
    fin = functools.partial(_merge, subln_g=subln_g[0], hgrn_norm_g=hgrn_norm_g[0], w_out_bf16=w_out16, tm=tm)

    xp = x_prompt.reshape(bp * tp, d_model)
    q, k, v, ga, hq, f, hi, gh, k16, v16 = proj(xp, jnp.arange(tp, dtype=jnp.int32), tp // tm, tm, True)
    o_attn = _prompt_attention(q, k16, v16, lam_params, bp, tp, attn_blk)
    o_hgrn, s_p = _hgrn(hq, f, hi, jnp.zeros((bp, H_HEADS, H_DK, H_DV), F32), bp, HGRN_ROWS, seqs=1, chunks=4)
    y_p = fin(xp, o_attn, ga, o_hgrn, gh)

    xs = x_sample.reshape(bs * ts, d_model)
    pos_s = past_len + jnp.tile(jnp.arange(ts, dtype=jnp.int32), tm // ts)
    sq, sk, sv, sga, shq, sf, shi, sgh = proj(xs, pos_s, 1, tm, False)
    ck = cache_k.reshape(n_pool, page_size * A_HEADS, 2 * A_DQK)
    cv = cache_v.reshape(n_pool, page_size * A_HEADS, A_DV)
    so_attn = _sample_attention(sq, sk, sv, ck, cv, page_table, lam_params, group=4)
    so_hgrn, s_s = _hgrn(shq, sf, shi, state_hgrn[0], bs, ts, seqs=8, chunks=1)
    y_s = fin(xs, so_attn, sga, so_hgrn, sgh)

    return (y_p.reshape(bp, tp, d_model), y_s.reshape(bs, ts, d_model),
            k.reshape(1, bp, tp, A_HEADS, 2 * A_DQK), v.reshape(1, bp, tp, A_HEADS, A_DV), s_p[None],
            sk.reshape(1, bs, ts, A_HEADS, 2 * A_DQK), sv.reshape(1, bs, ts, A_HEADS, A_DV), s_s[None])
```

```python
import functools
import math

import numpy as np
import jax
import jax.numpy as jnp
from jax import lax
from jax.experimental import pallas as pl
from jax.experimental.pallas import tpu as pltpu

F32 = jnp.float32
BF16 = jnp.bfloat16

LANES = 128
SUBLANES = 8
BF16_ROWS = 16
A_HEADS = 4
A_DQK = 64
A_DV = 128
H_HEADS = 4
H_DK = 128
H_DV = 128
SEG = 512
ROT_DIM = A_DQK // 4
ROPE_THETA = 500000.0
EPS = 1e-6
LAM_INIT = 0.8 - 0.6 * math.exp(-0.3 * 0)
Q_SCALE = A_DQK ** -0.5 * math.log2(math.e)
HGRN_ROWS = 128
SCORE_BOUND_LIMIT = 60.0
ATTN_CHAINS = 4
VMEM_LIMIT = 48 * 1024 * 1024


def _sigmoid(x):
    return 1.0 / (1.0 + jnp.exp(-x))


def _split_hi_lo(x):
    hi = x.astype(BF16)
    lo = (x - hi.astype(F32)).astype(BF16)
    return hi, lo


def _lam_from_ref(lam_ref):
    lp = lam_ref[...]
    s1 = jnp.sum(lp[0:1] * lp[1:2], axis=1, keepdims=True)
    s2 = jnp.sum(lp[2:3] * lp[3:4], axis=1, keepdims=True)
    return jnp.exp(s1) - jnp.exp(s2) + LAM_INIT


def _inproj_kernel(x_ref, ng_ref, w_ref, qg_ref, kg_ref, cos_ref, sa_ref, sb_ref, lbl_ref, gm_ref,
                   q_ref, k_ref, v_ref, ga_ref, hq_ref, f_ref, hi_ref, gh_ref, *maybe_bf16_refs):
    x = x_ref[...]
    ms = jnp.mean(x * x, axis=-1, keepdims=True)
    h = (x * lax.rsqrt(ms + EPS) * ng_ref[...]).astype(BF16)

    def proj(seg):
        return jnp.dot(h, w_ref[:, seg * SEG:(seg + 1) * SEG], preferred_element_type=F32)

    cos, sa, sb = cos_ref[...], sa_ref[...], sb_ref[...]

    def norm_rope(y, g_ref, hh):
        pair = y[:, (hh // 2) * 2 * LANES:(hh // 2 + 1) * 2 * LANES]
        gms = jnp.dot((pair * pair).astype(BF16), gm_ref[...], preferred_element_type=F32)
        own = slice((hh % 2) * LANES, (hh % 2 + 1) * LANES)
        yn = pair[:, own] * lax.rsqrt(gms[:, own] + EPS) * g_ref[...]
        return yn * cos + pltpu.roll(yn, LANES - ROT_DIM // 2, 1) * sa + pltpu.roll(yn, ROT_DIM // 2, 1) * sb

    tm = x.shape[0]
    aq = proj(0)
    for hh in range(A_HEADS):
        q_ref[:, hh * LANES:(hh + 1) * LANES] = norm_rope(aq, qg_ref, hh) * Q_SCALE
    ak = proj(1)
    for hh in range(A_HEADS):
        kh = norm_rope(ak, kg_ref, hh)
        k_ref[pl.ds(hh, tm, stride=A_HEADS), :] = kh
        if maybe_bf16_refs:
            maybe_bf16_refs[0][:, hh * LANES:(hh + 1) * LANES] = kh.astype(BF16)
    av = proj(2)
    for hh in range(A_HEADS):
        v_ref[pl.ds(hh, tm, stride=A_HEADS), :] = av[:, hh * LANES:(hh + 1) * LANES]
    if maybe_bf16_refs:
        maybe_bf16_refs[1][...] = av.astype(BF16)
    ag = proj(3)
    ga_ref[...] = ag * _sigmoid(ag)
    hq = proj(4)
    hq_ref[...] = hq * _sigmoid(hq)
    hf = proj(5)
    lbl = lbl_ref[...]
    l0, l1 = lbl[0:1], lbl[1:2]
    mx = jnp.maximum(l0, l1)
    e0, e1 = jnp.exp(l0 - mx), jnp.exp(l1 - mx)
    lb = e0 / (e0 + e1)
    f_ref[...] = lb + (1.0 - lb) * _sigmoid(hf)
    hi_ref[...] = proj(6)
    hg = proj(7)
    gh_ref[...] = hg * _sigmoid(hg)


def _rope_tables(pos):
    half = ROT_DIM // 2
    inv = ROPE_THETA ** (-jnp.arange(0, ROT_DIM, 2, dtype=F32) / ROT_DIM)
    lane = np.arange(LANES) % A_DQK
    ang = pos.astype(F32)[:, None] * inv[lane % half][None, :]
    cos = jnp.where((lane < ROT_DIM)[None, :], jnp.cos(ang), 1.0)
    sin = jnp.sin(ang)
    a = jnp.where((lane < half)[None, :], -sin, 0.0)
    b = jnp.where(((lane >= half) & (lane < ROT_DIM))[None, :], sin, 0.0)
    return cos, a, b


def _group_mean_matrix():
    g = np.zeros((2 * LANES, 2 * LANES), np.float32)
    for s in range(0, 2 * LANES, A_DQK):
        g[s:s + A_DQK, s:s + A_DQK] = 1.0 / A_DQK
    return jnp.asarray(g, BF16)


def _inproj(x2d, pos, n_pos_blocks, tm, with_bf16, norm_g, w_in_bf16, q_norm_g, k_norm_g, lb_logits):
    rows, d_model = x2d.shape
    d_in = w_in_bf16.shape[1]
    cos, sa, sb = _rope_tables(pos)
    qg = jnp.concatenate([q_norm_g, q_norm_g]).reshape(1, LANES)
    kg = jnp.concatenate([k_norm_g, k_norm_g]).reshape(1, LANES)
    const = lambda i: (0, 0)
    row_blk = lambda i: (i, 0)
    pos_blk = lambda i: (i % n_pos_blocks, 0)
    out_f32 = jax.ShapeDtypeStruct((rows, SEG), F32)
    out_shape = [out_f32] * 8
    out_specs = [pl.BlockSpec((tm, SEG), row_blk)] * 8
    for i in (1, 2):
        out_shape[i] = jax.ShapeDtypeStruct((rows * A_HEADS, LANES), F32)
        out_specs[i] = pl.BlockSpec((tm * A_HEADS, LANES), row_blk)
    if with_bf16:
        out_shape += [jax.ShapeDtypeStruct((rows, SEG), BF16)] * 2
        out_specs += [pl.BlockSpec((tm, SEG), row_blk)] * 2
    return pl.pallas_call(
        _inproj_kernel,
        grid=(rows // tm,),
        in_specs=[
            pl.BlockSpec((tm, d_model), row_blk),
            pl.BlockSpec((1, d_model), const),
            pl.BlockSpec((d_model, d_in), const, pipeline_mode=pl.Buffered(1)),
            pl.BlockSpec((1, LANES), const),
            pl.BlockSpec((1, LANES), const),
            pl.BlockSpec((tm, LANES), pos_blk),
            pl.BlockSpec((tm, LANES), pos_blk),
            pl.BlockSpec((tm, LANES), pos_blk),
            pl.BlockSpec((2, SEG), const),
            pl.BlockSpec((2 * LANES, 2 * LANES), const),
        ],
        out_specs=out_specs,
        out_shape=out_shape,
        compiler_params=pltpu.CompilerParams(dimension_semantics=("parallel",), vmem_limit_bytes=VMEM_LIMIT),
        name="inproj",
    )(x2d, norm_g.reshape(1, d_model), w_in_bf16, qg, kg, cos, sa, sb, lb_logits, _group_mean_matrix())


def _stack_maps(q_bf16):
    lane = lax.broadcasted_iota(jnp.int32, q_bf16.shape, 1)
    zero = jnp.zeros_like(q_bf16)
    return jnp.concatenate([jnp.where(lane < A_DQK, q_bf16, zero), jnp.where(lane >= A_DQK, q_bf16, zero)], axis=0)


def _softmax_block_update(qs_parts, k, v1, masks, m_ref, l_ref, acc_ref):
    width = k.shape[0]
    n = qs_parts[0].shape[0]
    scores = [lax.dot_general(qs, k, (((1,), (1,)), ((), ())), preferred_element_type=F32) for qs in qs_parts]
    probs, alphas, m_news = [], [], []
    for i, s in enumerate(scores):
        if masks is not None:
            s = jnp.where(masks[i], s, -jnp.inf)
        m_prev = m_ref[i * n:(i + 1) * n]
        m_new = jnp.maximum(m_prev, jnp.max(s, axis=1, keepdims=True))
        alphas.append(jnp.exp2(m_prev - m_new))
        probs.append(jnp.exp2(s - jnp.concatenate([m_new] * (width // LANES), axis=1)).astype(BF16))
        m_news.append(m_new)
    for i, p in enumerate(probs):
        rows = slice(i * n, (i + 1) * n)
        pv = jnp.dot(p, v1, preferred_element_type=F32)
        acc_ref[rows] = alphas[i] * acc_ref[rows] + pv[:, :A_DV]
        l_ref[rows] = alphas[i] * l_ref[rows] + pv[:, A_DV:]
        m_ref[rows] = m_news[i]


def _combine_maps(l0, acc0, l1, acc1, lam):
    return acc0 * (1.0 / l0) - lam * (acc1 * (1.0 / l1))


def _map_sum_matrix():
    g = np.zeros((LANES, 2 * LANES), np.float32)
    g[:A_DQK, :LANES] = 1.0
    g[A_DQK:, LANES:] = 1.0
    return jnp.asarray(g, BF16)


def _map_rows_matrix():
    g = np.zeros((2 * SUBLANES, LANES), np.float32)
    g[:SUBLANES, :A_DQK] = 1.0
    g[SUBLANES:, A_DQK:] = 1.0
    return jnp.asarray(g, BF16)


def _transposed_block_update(qs_parts, k, vt, masks, ref_ref, acc_ref):
    n = qs_parts[0].shape[0]
    scores = [lax.dot_general(k, qs, (((1,), (1,)), ((), ())), preferred_element_type=F32) for qs in qs_parts]
    for i, st in enumerate(scores):
        cols = slice(i * n, (i + 1) * n)
        if masks is not None:
            st = jnp.where(masks[i], st, -jnp.inf)
        p = jnp.exp2(st - ref_ref[0:1, cols]).astype(BF16)
        acc_ref[:, cols] = acc_ref[:, cols] + jnp.dot(vt, p, preferred_element_type=F32)


def _pattn_kernel(q_ref, k_ref, v_ref, lam_ref, ms_ref, mr_ref, o_ref,
                  v1_ref, vt_ref, kn_ref, m_ref, l_ref, acc_ref, mt_ref, acct_ref, *, blk):
    qi = pl.program_id(2)
    seq = k_ref.shape[0]

    @pl.when(qi == 0)
    def _():
        v1_ref[:, :A_DV] = v_ref[...]
        v1_ref[:, A_DV:] = jnp.ones((seq, LANES), BF16)
        for c in range(seq // blk):
            vt_ref[:A_DV, c * blk:(c + 1) * blk] = jnp.transpose(v_ref[c * blk:(c + 1) * blk, :])
        vt_ref[A_DV:, :] = jnp.ones((vt_ref.shape[0] - A_DV, seq), BF16)
        k = k_ref[...]
        kn_ref[...] = jnp.max(jnp.dot(k * k, ms_ref[...], preferred_element_type=F32), axis=0, keepdims=True)

    q = q_ref[...].astype(BF16)
    qs = _stack_maps(q)
    part = 2 * blk // ATTN_CHAINS
    qs_parts = [qs[i * part:(i + 1) * part] for i in range(ATTN_CHAINS)]

    qn = lax.dot_general(mr_ref[...], q * q, (((1,), (1,)), ((), ())), preferred_element_type=F32)
    kn = kn_ref[...]
    kn0 = jnp.concatenate([kn[:, :LANES]] * (blk // LANES), axis=1)
    kn1 = jnp.concatenate([kn[:, LANES:]] * (blk // LANES), axis=1)
    bound = jnp.concatenate([jnp.sqrt(qn[:SUBLANES] * kn0), jnp.sqrt(qn[SUBLANES:] * kn1)], axis=1) * 1.02
    mt_ref[...] = bound
    bounded = jnp.max(bound) <= SCORE_BOUND_LIMIT

    def run(block, unroll):
        def body(t, carry):
            for u in range(unroll):
                block(unroll * t + u, False)
            return carry

        lax.fori_loop(0, qi // unroll, body, 0)
        for u in range(1, unroll):
            @pl.when(qi % unroll >= u)
            def _():
                block(qi - (qi % unroll) + u - 1, False)
        block(qi, True)

    lam = _lam_from_ref(lam_ref)

    @pl.when(bounded)
    def _():
        acct_ref[...] = jnp.zeros(acct_ref.shape, F32)
        row = lax.broadcasted_iota(jnp.int32, (blk, part), 0)
        col = lax.broadcasted_iota(jnp.int32, (blk, part), 1)
        diag_masks = [row <= col + (i * part) % blk for i in range(ATTN_CHAINS)]

        def block(kb, diagonal):
            start = pl.multiple_of(kb * blk, blk)
            _transposed_block_update(qs_parts, k_ref[pl.ds(start, blk), :], vt_ref[:, pl.ds(start, blk)],
                                     diag_masks if diagonal else None, mt_ref, acct_ref)

        run(block, 2)
        acc = acct_ref[...]
        out_t = _combine_maps(acc[A_DV:A_DV + 1, :blk], acc[:A_DV, :blk], acc[A_DV:A_DV + 1, blk:], acc[:A_DV, blk:], lam)
        o_ref[...] = jnp.transpose(out_t)

    @pl.when(jnp.logical_not(bounded))
    def _():
        m_ref[...] = jnp.full(m_ref.shape, -jnp.inf, F32)
        l_ref[...] = jnp.zeros(l_ref.shape, F32)
        acc_ref[...] = jnp.zeros(acc_ref.shape, F32)
        row = lax.broadcasted_iota(jnp.int32, (part, blk), 0)
        col = lax.broadcasted_iota(jnp.int32, (part, blk), 1)
        diag_masks = [col <= row + (i * part) % blk for i in range(ATTN_CHAINS)]

        def block(kb, diagonal):
            start = pl.multiple_of(kb * blk, blk)
            _softmax_block_update(qs_parts, k_ref[pl.ds(start, blk), :], v1_ref[pl.ds(start, blk), :],
                                  diag_masks if diagonal else None, m_ref, l_ref, acc_ref)

        run(block, 1)
        o_ref[...] = _combine_maps(l_ref[:blk], acc_ref[:blk], l_ref[blk:], acc_ref[blk:], lam)


def _prompt_attention(q, kb16, vb16, lam_params, batch, seq, blk):
    nq = seq // blk
    const = lambda b, h, i: (0, 0)
    return pl.pallas_call(
        functools.partial(_pattn_kernel, blk=blk),
        grid=(batch, A_HEADS, nq),
        in_specs=[
            pl.BlockSpec((blk, LANES), lambda b, h, i: (b * nq + i, h)),
            pl.BlockSpec((seq, LANES), lambda b, h, i: (b, h)),
            pl.BlockSpec((seq, LANES), lambda b, h, i: (b, h)),
            pl.BlockSpec((4, A_DQK), const),
            pl.BlockSpec((LANES, 2 * LANES), const),
            pl.BlockSpec((2 * SUBLANES, LANES), const),
        ],
        out_specs=pl.BlockSpec((blk, LANES), lambda b, h, i: (b * nq + i, h)),
        out_shape=jax.ShapeDtypeStruct((batch * seq, A_HEADS * A_DV), F32),
        scratch_shapes=[
            pltpu.VMEM((seq, 2 * LANES), BF16),
            pltpu.VMEM((A_DV + BF16_ROWS, seq), BF16),
            pltpu.VMEM((1, 2 * LANES), F32),
            pltpu.VMEM((2 * blk, LANES), F32),
            pltpu.VMEM((2 * blk, LANES), F32),
            pltpu.VMEM((2 * blk, A_DV), F32),
            pltpu.VMEM((SUBLANES, 2 * blk), F32),
            pltpu.VMEM((A_DV + BF16_ROWS, 2 * blk), F32),
        ],
        compiler_params=pltpu.CompilerParams(
            dimension_semantics=("parallel", "parallel", "arbitrary"), vmem_limit_bytes=VMEM_LIMIT),
        name="prompt_attn",
    )(q, kb16, vb16, lam_params, _map_sum_matrix(), _map_rows_matrix())


def _partial_softmax(qs, k, v, mask):
    s = lax.dot_general(qs, k, (((1,), (1,)), ((), ())), preferred_element_type=F32)
    s = jnp.where(mask, s, -jnp.inf)
    m = jnp.max(s, axis=1, keepdims=True)
    p = jnp.exp2(s - m)
    return m, jnp.sum(p, axis=1, keepdims=True), jnp.dot(p.astype(BF16), v, preferred_element_type=F32)


def _sattn_kernel(pt_ref, q_ref, kn_ref, vn_ref, lam_ref, ck_hbm, cv_hbm, o_ref, kbuf, vbuf, sems,
                  *, n_pages, group, n_new):
    b = pl.program_id(0)

    def page_copies(elem, slot):
        copies = []
        for i in range(n_pages):
            page = pt_ref[elem * n_pages + i]
            copies.append(pltpu.make_async_copy(ck_hbm.at[page], kbuf.at[slot, i], sems.at[0, slot]))
            copies.append(pltpu.make_async_copy(cv_hbm.at[page], vbuf.at[slot, i], sems.at[1, slot]))
        return copies

    @pl.when(b == 0)
    def _():
        for cp in page_copies(0, 0):
            cp.start()

    @pl.when(b + 1 < pl.num_programs(0))
    def _():
        for cp in page_copies(b + 1, (b + 1) % 2):
            cp.start()

    q = q_ref[...].astype(BF16)
    qs = jnp.concatenate([_stack_maps(q[:, h * LANES:(h + 1) * LANES]) for h in range(A_HEADS)], axis=0)
    n_rows = 2 * n_new * A_HEADS

    def same_head(width):
        row = lax.broadcasted_iota(jnp.int32, (n_rows, width), 0)
        col = lax.broadcasted_iota(jnp.int32, (n_rows, width), 1)
        return row, col, (col % A_HEADS) == (row // (2 * n_new))

    n_kv = n_new * A_HEADS
    pad = jnp.zeros((LANES - n_kv, LANES), F32)
    row, col, ok = same_head(LANES)
    parts = [_partial_softmax(qs, jnp.concatenate([kn_ref[...], pad], axis=0).astype(BF16),
                              jnp.concatenate([vn_ref[...], pad], axis=0).astype(BF16),
                              ok & ((col // A_HEADS) <= (row % n_new)))]

    slot = b % 2
    for cp in page_copies(b, slot):
        cp.wait()
    rows_per_page = kbuf.shape[2]
    page_mask = same_head(group * rows_per_page)[2]
    for g in range(0, n_pages, group):
        k = kbuf[slot, g:g + group].reshape(group * rows_per_page, LANES).astype(BF16)
        v = vbuf[slot, g:g + group].reshape(group * rows_per_page, LANES).astype(BF16)
        parts.append(_partial_softmax(qs, k, v, page_mask))
    m = functools.reduce(jnp.maximum, [p[0] for p in parts])
    weights = [jnp.exp2(p[0] - m) for p in parts]
    l = sum(w * p[1] for w, p in zip(weights, parts))
    acc = sum(w * p[2] for w, p in zip(weights, parts))
    lam = _lam_from_ref(lam_ref)
    for h in range(A_HEADS):
        r0 = h * 2 * n_new
        r1 = r0 + n_new
        o_ref[:, h * LANES:(h + 1) * LANES] = _combine_maps(
            l[r0:r1], acc[r0:r1], l[r1:r1 + n_new], acc[r1:r1 + n_new], lam)


def _sample_attention(q, k_new, v_new, cache_k, cache_v, page_table, lam_params, group):
    bs, n_pages = page_table.shape
    n_new = q.shape[0] // bs
    rows_per_page = cache_k.shape[1]
    row_blk = lambda b, pt: (b, 0)
    grid_spec = pltpu.PrefetchScalarGridSpec(
        num_scalar_prefetch=1,
        grid=(bs,),
        in_specs=[
            pl.BlockSpec((n_new, q.shape[1]), row_blk),
            pl.BlockSpec((n_new * A_HEADS, LANES), row_blk),
            pl.BlockSpec((n_new * A_HEADS, LANES), row_blk),
            pl.BlockSpec((4, A_DQK), lambda b, pt: (0, 0)),
            pl.BlockSpec(memory_space=pl.ANY),
            pl.BlockSpec(memory_space=pl.ANY),
        ],
        out_specs=pl.BlockSpec((n_new, q.shape[1]), row_blk),
        scratch_shapes=[
            pltpu.VMEM((2, n_pages, rows_per_page, LANES), F32),
            pltpu.VMEM((2, n_pages, rows_per_page, LANES), F32),
            pltpu.SemaphoreType.DMA((2, 2)),
        ],
    )
    return pl.pallas_call(
        functools.partial(_sattn_kernel, n_pages=n_pages, group=group, n_new=n_new),
        grid_spec=grid_spec,
        out_shape=jax.ShapeDtypeStruct(q.shape, F32),
        compiler_params=pltpu.CompilerParams(dimension_semantics=("arbitrary",), vmem_limit_bytes=VMEM_LIMIT),
        name="sample_attn",
    )(page_table.reshape(-1), q, k_new, v_new, lam_params, cache_k, cache_v)


def _hgrn_kernel(q_ref, f_ref, v_ref, s0_ref, tri_ref, o_ref, s_ref, *, rows, seqs, chunks):
    @pl.when(pl.program_id(1) == 0)
    def _():
        s_ref[...] = s0_ref[...]

    n = HGRN_ROWS
    pad = n - rows
    r_idx = lax.broadcasted_iota(jnp.int32, (n, n), 0)
    c_idx = lax.broadcasted_iota(jnp.int32, (n, n), 1)
    causal = c_idx <= r_idx
    tri = tri_ref[...]
    units = [(e, h, c) for e in range(seqs) for h in range(H_HEADS) for c in range(chunks)]

    def rows_of(e, c):
        return slice((e * chunks + c) * rows, (e * chunks + c + 1) * rows)

    def lanes_of(h):
        return slice(h * LANES, (h + 1) * LANES)

    def padded(x, fill):
        return jnp.concatenate([x, jnp.full((pad, LANES), fill, F32)], axis=0) if pad else x

    logs = {}
    for (e, h, c) in units:
        hi, lo = _split_hi_lo(jnp.log(padded(f_ref[rows_of(e, c), lanes_of(h)], 1.0)))
        logs[e, h, c] = jnp.concatenate([hi, lo], axis=0)
    cums = {u: jnp.dot(tri, logs[u], preferred_element_type=F32) for u in units}

    q_end, k_end, vals, decay, att = {}, {}, {}, {}, {}
    for u in units:
        e, h, c = u
        b = cums[u]
        b_end = b[n - 1:n]
        b_mid = b[n // 2 - 1:n // 2]
        q = padded(q_ref[rows_of(e, c), lanes_of(h)], 0.0)
        k = 1.0 - padded(f_ref[rows_of(e, c), lanes_of(h)], 1.0)
        vals[u] = padded(v_ref[rows_of(e, c), lanes_of(h)], 0.0).astype(BF16)
        q_end[u] = (q * jnp.exp(b)).astype(BF16)
        k_end[u] = (k * jnp.exp(b_end - b)).astype(BF16)
        decay[u] = jnp.transpose(jnp.broadcast_to(jnp.exp(b_end), (8, LANES)))[:, 0:1]
        att[u] = lax.dot_general((q * jnp.exp(b - b_mid)).astype(BF16), (k * jnp.exp(b_mid - b)).astype(BF16),
                                 (((1,), (1,)), ((), ())), preferred_element_type=F32)
    kv = {u: lax.dot_general(k_end[u], vals[u], (((0,), (0,)), ((), ())), preferred_element_type=F32)
          for u in units}

    incoming = {}
    for e in range(seqs):
        for h in range(H_HEADS):
            state = s_ref[e, h]
            for c in range(chunks):
                incoming[e, h, c] = state.astype(BF16)
                state = decay[e, h, c] * state + kv[e, h, c]
            s_ref[e, h] = state
    for u in units:
        e, h, c = u
        lhs = jnp.concatenate([jnp.where(causal, att[u], 0.0).astype(BF16), q_end[u]], axis=1)
        o = jnp.dot(lhs, jnp.concatenate([vals[u], incoming[u]], axis=0), preferred_element_type=F32)
        o_ref[rows_of(e, c), lanes_of(h)] = o[:rows]


def _hgrn(hq, f, hi, s0, batch, rows, seqs, chunks):
    total = hq.shape[0]
    steps = total // (batch * rows * chunks)
    n = HGRN_ROWS
    tri = np.tril(np.ones((n, n), np.float32))
    tri = jnp.asarray(np.concatenate([tri, tri], axis=1), BF16)
    assert seqs == 1 or steps == 1
    blk = lambda b, c: (b * steps + c, 0)
    st = lambda b, c: (b, 0, 0, 0)
    return pl.pallas_call(
        functools.partial(_hgrn_kernel, rows=rows, seqs=seqs, chunks=chunks),
        grid=(batch // seqs, steps),
        in_specs=[
            pl.BlockSpec((seqs * chunks * rows, SEG), blk),
            pl.BlockSpec((seqs * chunks * rows, SEG), blk),
            pl.BlockSpec((seqs * chunks * rows, SEG), blk),
            pl.BlockSpec((seqs, H_HEADS, H_DK, H_DV), st),
            pl.BlockSpec((n, 2 * n), lambda b, c: (0, 0)),
        ],
        out_specs=[
            pl.BlockSpec((seqs * chunks * rows, SEG), blk),
            pl.BlockSpec((seqs, H_HEADS, H_DK, H_DV), st),
        ],
        out_shape=[
            jax.ShapeDtypeStruct((total, SEG), F32),
            jax.ShapeDtypeStruct((batch, H_HEADS, H_DK, H_DV), F32),
        ],
        compiler_params=pltpu.CompilerParams(
            dimension_semantics=("parallel", "arbitrary"), vmem_limit_bytes=VMEM_LIMIT),
        name="hgrn2",
    )(hq, f, hi, s0, tri)


def _merge_kernel(x_ref, oa_ref, ga_ref, oh_ref, gh_ref, sg_ref, hg_ref, w_ref, y_ref):
    def head_norm(o_ref, g_ref, h):
        o = o_ref[:, h * LANES:(h + 1) * LANES]
        return o * lax.rsqrt(jnp.mean(o * o, axis=-1, keepdims=True) + EPS) * g_ref[...]

    a = jnp.concatenate([head_norm(oa_ref, sg_ref, h) for h in range(A_HEADS)], axis=1)
    a = a * (1.0 - LAM_INIT) * ga_ref[...]
    r = jnp.concatenate([head_norm(oh_ref, hg_ref, h) for h in range(H_HEADS)], axis=1)
    r = r * gh_ref[...]
    mix = jnp.concatenate([a, r], axis=1).astype(BF16)
    y_ref[...] = x_ref[...] + jnp.dot(mix, w_ref[...], preferred_element_type=F32)


def _merge(x2d, o_attn, ga, o_hgrn, gh, subln_g, hgrn_norm_g, w_out_bf16, tm):
    rows, d_model = x2d.shape
    row_blk = lambda i: (i, 0)
    const = lambda i: (0, 0)
    return pl.pallas_call(
        _merge_kernel,
        grid=(rows // tm,),
        in_specs=[
            pl.BlockSpec((tm, d_model), row_blk),
            pl.BlockSpec((tm, SEG), row_blk),
            pl.BlockSpec((tm, SEG), row_blk),
            pl.BlockSpec((tm, SEG), row_blk),
            pl.BlockSpec((tm, SEG), row_blk),
            pl.BlockSpec((1, LANES), const),
            pl.BlockSpec((1, LANES), const),
            pl.BlockSpec(w_out_bf16.shape, const, pipeline_mode=pl.Buffered(1)),
        ],
        out_specs=pl.BlockSpec((tm, d_model), row_blk),
        out_shape=jax.ShapeDtypeStruct((rows, d_model), F32),
        compiler_params=pltpu.CompilerParams(dimension_semantics=("parallel",), vmem_limit_bytes=VMEM_LIMIT),
        name="merge",
    )(x2d, o_attn, ga, o_hgrn, gh, subln_g.reshape(1, LANES), hgrn_norm_g.reshape(1, LANES), w_out_bf16)


def kernel(x_prompt, x_sample, cache_k, cache_v, state_hgrn, page_table, norm_g, w_in, q_norm_g, k_norm_g,
           lambda_q1, lambda_k1, lambda_q2, lambda_k2, subln_g, hgrn_lb_logits, hgrn_norm_g, w_out):
    bp, tp, d_model = x_prompt.shape
    bs, ts, _ = x_sample.shape
    depth, n_pool, page_size = cache_k.shape[:3]
    assert depth == 1 and hgrn_lb_logits.shape[0] == 2
    assert (A_DV, H_DK, H_DV, 2 * A_DQK) == (LANES,) * 4
    past_len = page_table.shape[1] * page_size
    tm = 512
    attn_blk = 512

    w_in16 = w_in[0].astype(BF16)
    w_out16 = w_out[0].astype(BF16)
    lam_params = jnp.stack([lambda_q1[0], lambda_k1[0], lambda_q2[0], lambda_k2[0]])
    proj = functools.partial(_inproj, norm_g=norm_g[0], w_in_bf16=w_in16, q_norm_g=q_norm_g[0],
                             k_norm_g=k_norm_g[0], lb_logits=hgrn_lb_logits)
    fin = functools.partial(_merge, subln_g=subln_g[0], hgrn_norm_g=hgrn_norm_g[0], w_out_bf16=w_out16, tm=tm)

    xp = x_prompt.reshape(bp * tp, d_model)
    q, k, v, ga, hq, f, hi, gh, k16, v16 = proj(xp, jnp.arange(tp, dtype=jnp.int32), tp // tm, tm, True)
    o_attn = _prompt_attention(q, k16, v16, lam_params, bp, tp, attn_blk)
    o_hgrn, s_p = _hgrn(hq, f, hi, jnp.zeros((bp, H_HEADS, H_DK, H_DV), F32), bp, HGRN_ROWS, seqs=1, chunks=4)
    y_p = fin(xp, o_attn, ga, o_hgrn, gh)

    xs = x_sample.reshape(bs * ts, d_model)
    pos_s = past_len + jnp.tile(jnp.arange(ts, dtype=jnp.int32), tm // ts)
    sq, sk, sv, sga, shq, sf, shi, sgh = proj(xs, pos_s, 1, tm, False)
    ck = cache_k.reshape(n_pool, page_size * A_HEADS, 2 * A_DQK)
    cv = cache_v.reshape(n_pool, page_size * A_HEADS, A_DV)
    so_attn = _sample_attention(sq, sk, sv, ck, cv, page_table, lam_params, group=4)
    so_hgrn, s_s = _hgrn(shq, sf, shi, state_hgrn[0], bs, ts, seqs=8, chunks=1)
    y_s = fin(xs, so_attn, sga, so_hgrn, sgh)

    return (y_p.reshape(bp, tp, d_model), y_s.reshape(bs, ts, d_model),
            k.reshape(1, bp, tp, A_HEADS, 2 * A_DQK), v.reshape(1, bp, tp, A_HEADS, A_DV), s_p[None],
            sk.reshape(1, bs, ts, A_HEADS, 2 * A_DQK), sv.reshape(1, bs, ts, A_HEADS, A_DV), s_s[None])
```

```python
import functools
import math

import numpy as np
import jax
import jax.numpy as jnp
from jax import lax
from jax.experimental import pallas as pl
from jax.experimental.pallas import tpu as pltpu

F32 = jnp.float32
BF16 = jnp.bfloat16

LANES = 128
SUBLANES = 8
BF16_ROWS = 16
A_HEADS = 4
A_DQK = 64
A_DV = 128
H_HEADS = 4
H_DK = 128
H_DV = 128
SEG = 512
ROT_DIM = A_DQK // 4
ROPE_THETA = 500000.0
EPS = 1e-6
LAM_INIT = 0.8 - 0.6 * math.exp(-0.3 * 0)
Q_SCALE = A_DQK ** -0.5 * math.log2(math.e)
HGRN_ROWS = 128
SCORE_BOUND_LIMIT = 60.0
ATTN_GROUP = 256
VMEM_LIMIT = 48 * 1024 * 1024


def _sigmoid(x):
    return 1.0 / (1.0 + jnp.exp(-x))


def _split_hi_lo(x):
    hi = x.astype(BF16)
    lo = (x - hi.astype(F32)).astype(BF16)
    return hi, lo


def _lam_from_ref(lam_ref):
    lp = lam_ref[...]
    s1 = jnp.sum(lp[0:1] * lp[1:2], axis=1, keepdims=True)
    s2 = jnp.sum(lp[2:3] * lp[3:4], axis=1, keepdims=True)
    return jnp.exp(s1) - jnp.exp(s2) + LAM_INIT


def _inproj_kernel(x_ref, ng_ref, w_ref, qg_ref, kg_ref, cos_ref, sa_ref, sb_ref, lbl_ref, gm_ref,
                   q_ref, k_ref, v_ref, ga_ref, hq_ref, f_ref, hi_ref, gh_ref, *maybe_bf16_refs):
    x = x_ref[...]
    ms = jnp.mean(x * x, axis=-1, keepdims=True)
    h = (x * lax.rsqrt(ms + EPS) * ng_ref[...]).astype(BF16)

    def proj(seg):
        return jnp.dot(h, w_ref[:, seg * SEG:(seg + 1) * SEG], preferred_element_type=F32)

    cos, sa, sb = cos_ref[...], sa_ref[...], sb_ref[...]

    def norm_rope(y, g_ref, hh):
        pair = y[:, (hh // 2) * 2 * LANES:(hh // 2 + 1) * 2 * LANES]
        gms = jnp.dot((pair * pair).astype(BF16), gm_ref[...], preferred_element_type=F32)
        own = slice((hh % 2) * LANES, (hh % 2 + 1) * LANES)
        yn = pair[:, own] * lax.rsqrt(gms[:, own] + EPS) * g_ref[...]
        return yn * cos + pltpu.roll(yn, LANES - ROT_DIM // 2, 1) * sa + pltpu.roll(yn, ROT_DIM // 2, 1) * sb

    tm = x.shape[0]
    aq = proj(0)
    for hh in range(A_HEADS):
        q_ref[:, hh * LANES:(hh + 1) * LANES] = norm_rope(aq, qg_ref, hh) * Q_SCALE
    ak = proj(1)
    for hh in range(A_HEADS):
        kh = norm_rope(ak, kg_ref, hh)
        k_ref[pl.ds(hh, tm, stride=A_HEADS), :] = kh
        if maybe_bf16_refs:
            maybe_bf16_refs[0][:, hh * LANES:(hh + 1) * LANES] = kh.astype(BF16)
    av = proj(2)
    for hh in range(A_HEADS):
        v_ref[pl.ds(hh, tm, stride=A_HEADS), :] = av[:, hh * LANES:(hh + 1) * LANES]
    if maybe_bf16_refs:
        maybe_bf16_refs[1][...] = av.astype(BF16)
    ag = proj(3)
    ga_ref[...] = ag * _sigmoid(ag)
    hq = proj(4)
    hq_ref[...] = hq * _sigmoid(hq)
    hf = proj(5)
    lbl = lbl_ref[...]
    l0, l1 = lbl[0:1], lbl[1:2]
    mx = jnp.maximum(l0, l1)
    e0, e1 = jnp.exp(l0 - mx), jnp.exp(l1 - mx)
    lb = e0 / (e0 + e1)
    f_ref[...] = lb + (1.0 - lb) * _sigmoid(hf)
    hi_ref[...] = proj(6)
    hg = proj(7)
    gh_ref[...] = hg * _sigmoid(hg)


def _rope_tables(pos):
    half = ROT_DIM // 2
    inv = ROPE_THETA ** (-jnp.arange(0, ROT_DIM, 2, dtype=F32) / ROT_DIM)
    lane = np.arange(LANES) % A_DQK
    ang = pos.astype(F32)[:, None] * inv[lane % half][None, :]
    cos = jnp.where((lane < ROT_DIM)[None, :], jnp.cos(ang), 1.0)
    sin = jnp.sin(ang)
    a = jnp.where((lane < half)[None, :], -sin, 0.0)
    b = jnp.where(((lane >= half) & (lane < ROT_DIM))[None, :], sin, 0.0)
    return cos, a, b


def _group_mean_matrix():
    g = np.zeros((2 * LANES, 2 * LANES), np.float32)
    for s in range(0, 2 * LANES, A_DQK):
        g[s:s + A_DQK, s:s + A_DQK] = 1.0 / A_DQK
    return jnp.asarray(g, BF16)


def _inproj(x2d, pos, n_pos_blocks, tm, with_bf16, norm_g, w_in_bf16, q_norm_g, k_norm_g, lb_logits):
    rows, d_model = x2d.shape
    d_in = w_in_bf16.shape[1]
    cos, sa, sb = _rope_tables(pos)
    qg = jnp.concatenate([q_norm_g, q_norm_g]).reshape(1, LANES)
    kg = jnp.concatenate([k_norm_g, k_norm_g]).reshape(1, LANES)
    const = lambda i: (0, 0)
    row_blk = lambda i: (i, 0)
    pos_blk = lambda i: (i % n_pos_blocks, 0)
    out_f32 = jax.ShapeDtypeStruct((rows, SEG), F32)
    out_shape = [out_f32] * 8
    out_specs = [pl.BlockSpec((tm, SEG), row_blk)] * 8
    for i in (1, 2):
        out_shape[i] = jax.ShapeDtypeStruct((rows * A_HEADS, LANES), F32)
        out_specs[i] = pl.BlockSpec((tm * A_HEADS, LANES), row_blk)
    if with_bf16:
        out_shape += [jax.ShapeDtypeStruct((rows, SEG), BF16)] * 2
        out_specs += [pl.BlockSpec((tm, SEG), row_blk)] * 2
    return pl.pallas_call(
        _inproj_kernel,
        grid=(rows // tm,),
        in_specs=[
            pl.BlockSpec((tm, d_model), row_blk),
            pl.BlockSpec((1, d_model), const),
            pl.BlockSpec((d_model, d_in), const, pipeline_mode=pl.Buffered(1)),
            pl.BlockSpec((1, LANES), const),
            pl.BlockSpec((1, LANES), const),
            pl.BlockSpec((tm, LANES), pos_blk),
            pl.BlockSpec((tm, LANES), pos_blk),
            pl.BlockSpec((tm, LANES), pos_blk),
            pl.BlockSpec((2, SEG), const),
            pl.BlockSpec((2 * LANES, 2 * LANES), const),
        ],
        out_specs=out_specs,
        out_shape=out_shape,
        compiler_params=pltpu.CompilerParams(dimension_semantics=("parallel",), vmem_limit_bytes=VMEM_LIMIT),
        name="inproj",
    )(x2d, norm_g.reshape(1, d_model), w_in_bf16, qg, kg, cos, sa, sb, lb_logits, _group_mean_matrix())


def _stack_maps(q_bf16):
    lane = lax.broadcasted_iota(jnp.int32, q_bf16.shape, 1)
    zero = jnp.zeros_like(q_bf16)
    return jnp.concatenate([jnp.where(lane < A_DQK, q_bf16, zero), jnp.where(lane >= A_DQK, q_bf16, zero)], axis=0)


def _softmax_block_update(qs_parts, k, v1, plan, m_ref, l_ref, acc_ref):
    width = k.shape[0]
    n = qs_parts[0].shape[0]
    scores = [lax.dot_general(qs_parts[i], k, (((1,), (1,)), ((), ())), preferred_element_type=F32)
              for i, _ in plan]
    probs, alphas, m_news = [], [], []
    for (i, mask), s in zip(plan, scores):
        if mask is not None:
            s = jnp.where(mask, s, -jnp.inf)
        m_prev = m_ref[i * n:(i + 1) * n]
        m_new = jnp.maximum(m_prev, jnp.max(s, axis=1, keepdims=True))
        alphas.append(jnp.exp2(m_prev - m_new))
        probs.append(jnp.exp2(s - jnp.concatenate([m_new] * (width // LANES), axis=1)).astype(BF16))
        m_news.append(m_new)
    for (i, _), p, alpha, m_new in zip(plan, probs, alphas, m_news):
        rows = slice(i * n, (i + 1) * n)
        pv = jnp.dot(p, v1, preferred_element_type=F32)
        acc_ref[rows] = alpha * acc_ref[rows] + pv[:, :A_DV]
        l_ref[rows] = alpha * l_ref[rows] + pv[:, A_DV:]
        m_ref[rows] = m_new


def _combine_maps(l0, acc0, l1, acc1, lam):
    return acc0 * (1.0 / l0) - lam * (acc1 * (1.0 / l1))


def _map_sum_matrix():
    g = np.zeros((LANES, 2 * LANES), np.float32)
    g[:A_DQK, :LANES] = 1.0
    g[A_DQK:, LANES:] = 1.0
    return jnp.asarray(g, BF16)


def _map_rows_matrix():
    g = np.zeros((2 * SUBLANES, LANES), np.float32)
    g[:SUBLANES, :A_DQK] = 1.0
    g[SUBLANES:, A_DQK:] = 1.0
    return jnp.asarray(g, BF16)


def _transposed_block_update(qs_parts, k, vt, plan, ref_ref, acc_ref):
    n = qs_parts[0].shape[0]
    scores = [lax.dot_general(k, qs_parts[i], (((1,), (1,)), ((), ())), preferred_element_type=F32)
              for i, _ in plan]
    for (i, mask), st in zip(plan, scores):
        cols = slice(i * n, (i + 1) * n)
        if mask is not None:
            st = jnp.where(mask, st, -jnp.inf)
        p = jnp.exp2(st - ref_ref[0:1, cols]).astype(BF16)
        acc_ref[:, cols] = acc_ref[:, cols] + jnp.dot(vt, p, preferred_element_type=F32)


def _diagonal_plans(qblk, kblk, part, keys_on_rows):
    per_map = qblk // part
    shape = (kblk, part) if keys_on_rows else (part, kblk)
    key = lax.broadcasted_iota(jnp.int32, shape, 0 if keys_on_rows else 1)
    qry = lax.broadcasted_iota(jnp.int32, shape, 1 if keys_on_rows else 0)
    plans = []
    for j in range(qblk // kblk):
        plan = []
        for c in range(2 * per_map):
            q0 = (c % per_map) * part
            if (j + 1) * kblk - 1 <= q0:
                plan.append((c, None))
            elif j * kblk <= q0 + part - 1:
                plan.append((c, key + j * kblk <= qry + q0))
        plans.append(plan)
    return plans


def _pattn_kernel(q_ref, k_ref, v_ref, lam_ref, ms_ref, mr_ref, o_ref,
                  v1_ref, vt_ref, kn_ref, m_ref, l_ref, acc_ref, mt_ref, acct_ref, *, qblk, kblk):
    qi = pl.program_id(2)
    seq = k_ref.shape[0]
    ratio = qblk // kblk

    @pl.when(qi == 0)
    def _():
        v1_ref[:, :A_DV] = v_ref[...]
        v1_ref[:, A_DV:] = jnp.ones((seq, LANES), BF16)
        for c in range(seq // kblk):
            vt_ref[:A_DV, c * kblk:(c + 1) * kblk] = jnp.transpose(v_ref[c * kblk:(c + 1) * kblk, :])
        vt_ref[A_DV:, :] = jnp.ones((vt_ref.shape[0] - A_DV, seq), BF16)
        k = k_ref[...]
        kn_ref[...] = jnp.max(jnp.dot(k * k, ms_ref[...], preferred_element_type=F32), axis=0, keepdims=True)

    q = q_ref[...].astype(BF16)
    qs = _stack_maps(q)
    part = ATTN_GROUP
    n_groups = 2 * qblk // part
    qs_parts = [qs[i * part:(i + 1) * part] for i in range(n_groups)]
    every_group = [(i, None) for i in range(n_groups)]

    qn = lax.dot_general(mr_ref[...], q * q, (((1,), (1,)), ((), ())), preferred_element_type=F32)
    kn = kn_ref[...]
    kn0 = jnp.concatenate([kn[:, :LANES]] * (qblk // LANES), axis=1)
    kn1 = jnp.concatenate([kn[:, LANES:]] * (qblk // LANES), axis=1)
    bound = jnp.concatenate([jnp.sqrt(qn[:SUBLANES] * kn0), jnp.sqrt(qn[SUBLANES:] * kn1)], axis=1) * 1.02
    mt_ref[...] = bound
    bounded = jnp.max(bound) <= SCORE_BOUND_LIMIT

    def run(block, diag_plans):
        def body(t, carry):
            for u in range(ratio):
                block(ratio * t + u, every_group)
            return carry

        lax.fori_loop(0, qi, body, 0)
        for j, plan in enumerate(diag_plans):
            block(ratio * qi + j, plan)

    lam = _lam_from_ref(lam_ref)

    @pl.when(bounded)
    def _():
        acct_ref[...] = jnp.zeros(acct_ref.shape, F32)

        def block(kb, plan):
            start = pl.multiple_of(kb * kblk, kblk)
            _transposed_block_update(qs_parts, k_ref[pl.ds(start, kblk), :], vt_ref[:, pl.ds(start, kblk)],
                                     plan, mt_ref, acct_ref)

        run(block, _diagonal_plans(qblk, kblk, part, True))
        acc = acct_ref[...]
        out_t = _combine_maps(acc[A_DV:A_DV + 1, :qblk], acc[:A_DV, :qblk],
                              acc[A_DV:A_DV + 1, qblk:], acc[:A_DV, qblk:], lam)
        o_ref[...] = jnp.transpose(out_t)

    @pl.when(jnp.logical_not(bounded))
    def _():
        m_ref[...] = jnp.full(m_ref.shape, -jnp.inf, F32)
        l_ref[...] = jnp.zeros(l_ref.shape, F32)
        acc_ref[...] = jnp.zeros(acc_ref.shape, F32)

        def block(kb, plan):
            start = pl.multiple_of(kb * kblk, kblk)
            _softmax_block_update(qs_parts, k_ref[pl.ds(start, kblk), :], v1_ref[pl.ds(start, kblk), :],
                                  plan, m_ref, l_ref, acc_ref)

        run(block, _diagonal_plans(qblk, kblk, part, False))
        o_ref[...] = _combine_maps(l_ref[:qblk], acc_ref[:qblk], l_ref[qblk:], acc_ref[qblk:], lam)


def _prompt_attention(q, kb16, vb16, lam_params, batch, seq, qblk, kblk):
    nq = seq // qblk
    const = lambda b, h, i: (0, 0)
    return pl.pallas_call(
        functools.partial(_pattn_kernel, qblk=qblk, kblk=kblk),
        grid=(batch, A_HEADS, nq),
        in_specs=[
            pl.BlockSpec((qblk, LANES), lambda b, h, i: (b * nq + i, h)),
            pl.BlockSpec((seq, LANES), lambda b, h, i: (b, h)),
            pl.BlockSpec((seq, LANES), lambda b, h, i: (b, h)),
            pl.BlockSpec((4, A_DQK), const),
            pl.BlockSpec((LANES, 2 * LANES), const),
            pl.BlockSpec((2 * SUBLANES, LANES), const),
        ],
        out_specs=pl.BlockSpec((qblk, LANES), lambda b, h, i: (b * nq + i, h)),
        out_shape=jax.ShapeDtypeStruct((batch * seq, A_HEADS * A_DV), F32),
        scratch_shapes=[
            pltpu.VMEM((seq, 2 * LANES), BF16),
            pltpu.VMEM((A_DV + BF16_ROWS, seq), BF16),
            pltpu.VMEM((1, 2 * LANES), F32),
            pltpu.VMEM((2 * qblk, LANES), F32),
            pltpu.VMEM((2 * qblk, LANES), F32),
            pltpu.VMEM((2 * qblk, A_DV), F32),
            pltpu.VMEM((SUBLANES, 2 * qblk), F32),
            pltpu.VMEM((A_DV + BF16_ROWS, 2 * qblk), F32),
        ],
        compiler_params=pltpu.CompilerParams(
            dimension_semantics=("parallel", "parallel", "arbitrary"), vmem_limit_bytes=VMEM_LIMIT),
        name="prompt_attn",
    )(q, kb16, vb16, lam_params, _map_sum_matrix(), _map_rows_matrix())


def _partial_softmax(qs, k, v, mask):
    s = lax.dot_general(qs, k, (((1,), (1,)), ((), ())), preferred_element_type=F32)
    s = jnp.where(mask, s, -jnp.inf)
    m = jnp.max(s, axis=1, keepdims=True)
    p = jnp.exp2(s - m)
    return m, jnp.sum(p, axis=1, keepdims=True), jnp.dot(p.astype(BF16), v, preferred_element_type=F32)


def _sattn_kernel(pt_ref, q_ref, kn_ref, vn_ref, lam_ref, ck_hbm, cv_hbm, o_ref, kbuf, vbuf, sems,
                  *, n_pages, group, n_new):
    b = pl.program_id(0)

    def page_copies(elem, slot):
        copies = []
        for i in range(n_pages):
            page = pt_ref[elem * n_pages + i]
            copies.append(pltpu.make_async_copy(ck_hbm.at[page], kbuf.at[slot, i], sems.at[0, slot]))
            copies.append(pltpu.make_async_copy(cv_hbm.at[page], vbuf.at[slot, i], sems.at[1, slot]))
        return copies

    @pl.when(b == 0)
    def _():
        for cp in page_copies(0, 0):
            cp.start()

    @pl.when(b + 1 < pl.num_programs(0))
    def _():
        for cp in page_copies(b + 1, (b + 1) % 2):
            cp.start()

    q = q_ref[...].astype(BF16)
    qs = jnp.concatenate([_stack_maps(q[:, h * LANES:(h + 1) * LANES]) for h in range(A_HEADS)], axis=0)
    n_rows = 2 * n_new * A_HEADS

    def same_head(width):
        row = lax.broadcasted_iota(jnp.int32, (n_rows, width), 0)
        col = lax.broadcasted_iota(jnp.int32, (n_rows, width), 1)
        return row, col, (col % A_HEADS) == (row // (2 * n_new))

    n_kv = n_new * A_HEADS
    pad = jnp.zeros((LANES - n_kv, LANES), F32)
    row, col, ok = same_head(LANES)
    parts = [_partial_softmax(qs, jnp.concatenate([kn_ref[...], pad], axis=0).astype(BF16),
                              jnp.concatenate([vn_ref[...], pad], axis=0).astype(BF16),
                              ok & ((col // A_HEADS) <= (row % n_new)))]

    slot = b % 2
    for cp in page_copies(b, slot):
        cp.wait()
    rows_per_page = kbuf.shape[2]
    page_mask = same_head(group * rows_per_page)[2]
    for g in range(0, n_pages, group):
        k = kbuf[slot, g:g + group].reshape(group * rows_per_page, LANES).astype(BF16)
        v = vbuf[slot, g:g + group].reshape(group * rows_per_page, LANES).astype(BF16)
        parts.append(_partial_softmax(qs, k, v, page_mask))
    m = functools.reduce(jnp.maximum, [p[0] for p in parts])
    weights = [jnp.exp2(p[0] - m) for p in parts]
    l = sum(w * p[1] for w, p in zip(weights, parts))
    acc = sum(w * p[2] for w, p in zip(weights, parts))
    lam = _lam_from_ref(lam_ref)
    for h in range(A_HEADS):
        r0 = h * 2 * n_new
        r1 = r0 + n_new
        o_ref[:, h * LANES:(h + 1) * LANES] = _combine_maps(
            l[r0:r1], acc[r0:r1], l[r1:r1 + n_new], acc[r1:r1 + n_new], lam)


def _sample_attention(q, k_new, v_new, cache_k, cache_v, page_table, lam_params, group):
    bs, n_pages = page_table.shape
    n_new = q.shape[0] // bs
    rows_per_page = cache_k.shape[1]
    row_blk = lambda b, pt: (b, 0)
    grid_spec = pltpu.PrefetchScalarGridSpec(
        num_scalar_prefetch=1,
        grid=(bs,),
        in_specs=[
            pl.BlockSpec((n_new, q.shape[1]), row_blk),
            pl.BlockSpec((n_new * A_HEADS, LANES), row_blk),
            pl.BlockSpec((n_new * A_HEADS, LANES), row_blk),
            pl.BlockSpec((4, A_DQK), lambda b, pt: (0, 0)),
            pl.BlockSpec(memory_space=pl.ANY),
            pl.BlockSpec(memory_space=pl.ANY),
        ],
        out_specs=pl.BlockSpec((n_new, q.shape[1]), row_blk),
        scratch_shapes=[
            pltpu.VMEM((2, n_pages, rows_per_page, LANES), F32),
            pltpu.VMEM((2, n_pages, rows_per_page, LANES), F32),
            pltpu.SemaphoreType.DMA((2, 2)),
        ],
    )
    return pl.pallas_call(
        functools.partial(_sattn_kernel, n_pages=n_pages, group=group, n_new=n_new),
        grid_spec=grid_spec,
        out_shape=jax.ShapeDtypeStruct(q.shape, F32),
        compiler_params=pltpu.CompilerParams(dimension_semantics=("arbitrary",), vmem_limit_bytes=VMEM_LIMIT),
        name="sample_attn",
    )(page_table.reshape(-1), q, k_new, v_new, lam_params, cache_k, cache_v)


def _hgrn_kernel(q_ref, f_ref, v_ref, s0_ref, tri_ref, o_ref, s_ref, *, rows, seqs, chunks):
    @pl.when(pl.program_id(1) == 0)
    def _():
        s_ref[...] = s0_ref[...]

    n = HGRN_ROWS
    pad = n - rows
    r_idx = lax.broadcasted_iota(jnp.int32, (n, n), 0)
    c_idx = lax.broadcasted_iota(jnp.int32, (n, n), 1)
    causal = c_idx <= r_idx
    tri = tri_ref[...]
    units = [(e, h, c) for e in range(seqs) for h in range(H_HEADS) for c in range(chunks)]

    def rows_of(e, c):
        return slice((e * chunks + c) * rows, (e * chunks + c + 1) * rows)

    def lanes_of(h):
        return slice(h * LANES, (h + 1) * LANES)

    def padded(x, fill):
        return jnp.concatenate([x, jnp.full((pad, LANES), fill, F32)], axis=0) if pad else x

    logs = {}
    for (e, h, c) in units:
        hi, lo = _split_hi_lo(jnp.log(padded(f_ref[rows_of(e, c), lanes_of(h)], 1.0)))
        logs[e, h, c] = jnp.concatenate([hi, lo], axis=0)
    cums = {u: jnp.dot(tri, logs[u], preferred_element_type=F32) for u in units}

    q_end, k_end, vals, decay, att = {}, {}, {}, {}, {}
    for u in units:
        e, h, c = u
        b = cums[u]
        b_end = b[n - 1:n]
        b_mid = b[n // 2 - 1:n // 2]
        q = padded(q_ref[rows_of(e, c), lanes_of(h)], 0.0)
        k = 1.0 - padded(f_ref[rows_of(e, c), lanes_of(h)], 1.0)
        vals[u] = padded(v_ref[rows_of(e, c), lanes_of(h)], 0.0).astype(BF16)
        q_end[u] = (q * jnp.exp(b)).astype(BF16)
        k_end[u] = (k * jnp.exp(b_end - b)).astype(BF16)
        decay[u] = jnp.transpose(jnp.broadcast_to(jnp.exp(b_end), (8, LANES)))[:, 0:1]
        att[u] = lax.dot_general((q * jnp.exp(b - b_mid)).astype(BF16), (k * jnp.exp(b_mid - b)).astype(BF16),
                                 (((1,), (1,)), ((), ())), preferred_element_type=F32)
    kv = {u: lax.dot_general(k_end[u], vals[u], (((0,), (0,)), ((), ())), preferred_element_type=F32)
          for u in units}

    incoming = {}
    for e in range(seqs):
        for h in range(H_HEADS):
            state = s_ref[e, h]
            for c in range(chunks):
                incoming[e, h, c] = state.astype(BF16)
                state = decay[e, h, c] * state + kv[e, h, c]
            s_ref[e, h] = state
    for u in units:
        e, h, c = u
        lhs = jnp.concatenate([jnp.where(causal, att[u], 0.0).astype(BF16), q_end[u]], axis=1)
        o = jnp.dot(lhs, jnp.concatenate([vals[u], incoming[u]], axis=0), preferred_element_type=F32)
        o_ref[rows_of(e, c), lanes_of(h)] = o[:rows]


def _hgrn(hq, f, hi, s0, batch, rows, seqs, chunks):
    total = hq.shape[0]
    steps = total // (batch * rows * chunks)
    n = HGRN_ROWS
    tri = np.tril(np.ones((n, n), np.float32))
    tri = jnp.asarray(np.concatenate([tri, tri], axis=1), BF16)
    assert seqs == 1 or steps == 1
    blk = lambda b, c: (b * steps + c, 0)
    st = lambda b, c: (b, 0, 0, 0)
    return pl.pallas_call(
        functools.partial(_hgrn_kernel, rows=rows, seqs=seqs, chunks=chunks),
        grid=(batch // seqs, steps),
        in_specs=[
            pl.BlockSpec((seqs * chunks * rows, SEG), blk),
            pl.BlockSpec((seqs * chunks * rows, SEG), blk),
            pl.BlockSpec((seqs * chunks * rows, SEG), blk),
            pl.BlockSpec((seqs, H_HEADS, H_DK, H_DV), st),
            pl.BlockSpec((n, 2 * n), lambda b, c: (0, 0)),
        ],
        out_specs=[
            pl.BlockSpec((seqs * chunks * rows, SEG), blk),
            pl.BlockSpec((seqs, H_HEADS, H_DK, H_DV), st),
        ],
        out_shape=[
            jax.ShapeDtypeStruct((total, SEG), F32),
            jax.ShapeDtypeStruct((batch, H_HEADS, H_DK, H_DV), F32),
        ],
        compiler_params=pltpu.CompilerParams(
            dimension_semantics=("parallel", "arbitrary"), vmem_limit_bytes=VMEM_LIMIT),
        name="hgrn2",
    )(hq, f, hi, s0, tri)


def _merge_kernel(x_ref, oa_ref, ga_ref, oh_ref, gh_ref, sg_ref, hg_ref, w_ref, y_ref):
    def head_norm(o_ref, g_ref, h):
        o = o_ref[:, h * LANES:(h + 1) * LANES]
        return o * lax.rsqrt(jnp.mean(o * o, axis=-1, keepdims=True) + EPS) * g_ref[...]

    a = jnp.concatenate([head_norm(oa_ref, sg_ref, h) for h in range(A_HEADS)], axis=1)
    a = a * (1.0 - LAM_INIT) * ga_ref[...]
    r = jnp.concatenate([head_norm(oh_ref, hg_ref, h) for h in range(H_HEADS)], axis=1)
    r = r * gh_ref[...]
    mix = jnp.concatenate([a, r], axis=1).astype(BF16)
    y_ref[...] = x_ref[...] + jnp.dot(mix, w_ref[...], preferred_element_type=F32)


def _merge(x2d, o_attn, ga, o_hgrn, gh, subln_g, hgrn_norm_g, w_out_bf16, tm):
    rows, d_model = x2d.shape
    row_blk = lambda i: (i, 0)
    const = lambda i: (0, 0)
    return pl.pallas_call(
        _merge_kernel,
        grid=(rows // tm,),
        in_specs=[
            pl.BlockSpec((tm, d_model), row_blk),
            pl.BlockSpec((tm, SEG), row_blk),
            pl.BlockSpec((tm, SEG), row_blk),
            pl.BlockSpec((tm, SEG), row_blk),
            pl.BlockSpec((tm, SEG), row_blk),
            pl.BlockSpec((1, LANES), const),
            pl.BlockSpec((1, LANES), const),
            pl.BlockSpec(w_out_bf16.shape, const, pipeline_mode=pl.Buffered(1)),
        ],
        out_specs=pl.BlockSpec((tm, d_model), row_blk),
        out_shape=jax.ShapeDtypeStruct((rows, d_model), F32),
        compiler_params=pltpu.CompilerParams(dimension_semantics=("parallel",), vmem_limit_bytes=VMEM_LIMIT),
        name="merge",
    )(x2d, o_attn, ga, o_hgrn, gh, subln_g.reshape(1, LANES), hgrn_norm_g.reshape(1, LANES), w_out_bf16)


def kernel(x_prompt, x_sample, cache_k, cache_v, state_hgrn, page_table, norm_g, w_in, q_norm_g, k_norm_g,
           lambda_q1, lambda_k1, lambda_q2, lambda_k2, subln_g, hgrn_lb_logits, hgrn_norm_g, w_out):
    bp, tp, d_model = x_prompt.shape
    bs, ts, _ = x_sample.shape
    depth, n_pool, page_size = cache_k.shape[:3]
    assert depth == 1 and hgrn_lb_logits.shape[0] == 2
    assert (A_DV, H_DK, H_DV, 2 * A_DQK) == (LANES,) * 4
    past_len = page_table.shape[1] * page_size
    tm = 512
    attn_qblk, attn_kblk = 1024, 512

    w_in16 = w_in[0].astype(BF16)
    w_out16 = w_out[0].astype(BF16)
    lam_params = jnp.stack([lambda_q1[0], lambda_k1[0], lambda_q2[0], lambda_k2[0]])
    proj = functools.partial(_inproj, norm_g=norm_g[0], w_in_bf16=w_in16, q_norm_g=q_norm_g[0],
                             k_norm_g=k_norm_g[0], lb_logits=hgrn_lb_logits)
    fin = functools.partial(_merge, subln_g=subln_g[0], hgrn_norm_g=hgrn_norm_g[0], w_out_bf16=w_out16, tm=tm)

    xp = x_prompt.reshape(bp * tp, d_model)
    q, k, v, ga, hq, f, hi, gh, k16, v16 = proj(xp, jnp.arange(tp, dtype=jnp.int32), tp // tm, tm, True)
    o_attn = _prompt_attention(q, k16, v16, lam_params, bp, tp, attn_qblk, attn_kblk)
    o_hgrn, s_p = _hgrn(hq, f, hi, jnp.zeros((bp, H_HEADS, H_DK, H_DV), F32), bp, HGRN_ROWS, seqs=1, chunks=4)
    y_p = fin(xp, o_attn, ga, o_hgrn, gh)

    xs = x_sample.reshape(bs * ts, d_model)
    pos_s = past_len + jnp.tile(jnp.arange(ts, dtype=jnp.int32), tm // ts)
    sq, sk, sv, sga, shq, sf, shi, sgh = proj(xs, pos_s, 1, tm, False)
    ck = cache_k.reshape(n_pool, page_size * A_HEADS, 2 * A_DQK)
    cv = cache_v.reshape(n_pool, page_size * A_HEADS, A_DV)
    so_attn = _sample_attention(sq, sk, sv, ck, cv, page_table, lam_params, group=page_table.shape[1])
    so_hgrn, s_s = _hgrn(shq, sf, shi, state_hgrn[0], bs, ts, seqs=8, chunks=1)
    y_s = fin(xs, so_attn, sga, so_hgrn, sgh)

    return (y_p.reshape(bp, tp, d_model), y_s.reshape(bs, ts, d_model),
            k.reshape(1, bp, tp, A_HEADS, 2 * A_DQK), v.reshape(1, bp, tp, A_HEADS, A_DV), s_p[None],
            sk.reshape(1, bs, ts, A_HEADS, 2 * A_DQK), sv.reshape(1, bs, ts, A_HEADS, A_DV), s_s[None])
```

```python
import functools
import math

import numpy as np
import jax
import jax.numpy as jnp
from jax import lax
from jax.experimental import pallas as pl
from jax.experimental.pallas import tpu as pltpu

F32 = jnp.float32
BF16 = jnp.bfloat16

LANES = 128
SUBLANES = 8
BF16_ROWS = 16
A_HEADS = 4
A_DQK = 64
A_DV = 128
H_HEADS = 4
H_DK = 128
H_DV = 128
SEG = 512
ROT_DIM = A_DQK // 4
ROPE_THETA = 500000.0
EPS = 1e-6
LAM_INIT = 0.8 - 0.6 * math.exp(-0.3 * 0)
Q_SCALE = A_DQK ** -0.5 * math.log2(math.e)
HGRN_ROWS = 128
SCORE_BOUND_LIMIT = 60.0
PAGE_AHEAD = 2
ATTN_GROUP = 256
KEY_UNROLL = 2
VMEM_LIMIT = 48 * 1024 * 1024


def _sigmoid(x):
    return 1.0 / (1.0 + jnp.exp(-x))


def _split_hi_lo(x):
    hi = x.astype(BF16)
    lo = (x - hi.astype(F32)).astype(BF16)
    return hi, lo


def _lam_from_ref(lam_ref):
    lp = lam_ref[...]
    s1 = jnp.sum(lp[0:1] * lp[1:2], axis=1, keepdims=True)
    s2 = jnp.sum(lp[2:3] * lp[3:4], axis=1, keepdims=True)
    return jnp.exp(s1) - jnp.exp(s2) + LAM_INIT


def _inproj_kernel(x_ref, ng_ref, w_ref, qg_ref, kg_ref, cos_ref, sa_ref, sb_ref, lbl_ref, gm_ref,
                   q_ref, k_ref, v_ref, ga_ref, hq_ref, f_ref, hi_ref, gh_ref, *maybe_bf16_refs):
    x = x_ref[...]
    ms = jnp.mean(x * x, axis=-1, keepdims=True)
    h = (x * lax.rsqrt(ms + EPS) * ng_ref[...]).astype(BF16)

    def proj(seg):
        return jnp.dot(h, w_ref[:, seg * SEG:(seg + 1) * SEG], preferred_element_type=F32)

    cos, sa, sb = cos_ref[...], sa_ref[...], sb_ref[...]

    def norm_rope(y, g_ref, hh):
        pair = y[:, (hh // 2) * 2 * LANES:(hh // 2 + 1) * 2 * LANES]
        gms = jnp.dot((pair * pair).astype(BF16), gm_ref[...], preferred_element_type=F32)
        own = slice((hh % 2) * LANES, (hh % 2 + 1) * LANES)
        yn = pair[:, own] * lax.rsqrt(gms[:, own] + EPS) * g_ref[...]
        return yn * cos + pltpu.roll(yn, LANES - ROT_DIM // 2, 1) * sa + pltpu.roll(yn, ROT_DIM // 2, 1) * sb

    tm = x.shape[0]
    aq = proj(0)
    for hh in range(A_HEADS):
        q_ref[:, hh * LANES:(hh + 1) * LANES] = norm_rope(aq, qg_ref, hh) * Q_SCALE
    ak = proj(1)
    for hh in range(A_HEADS):
        kh = norm_rope(ak, kg_ref, hh)
        k_ref[pl.ds(hh, tm, stride=A_HEADS), :] = kh
        if maybe_bf16_refs:
            maybe_bf16_refs[0][:, hh * LANES:(hh + 1) * LANES] = kh.astype(BF16)
    av = proj(2)
    for hh in range(A_HEADS):
        v_ref[pl.ds(hh, tm, stride=A_HEADS), :] = av[:, hh * LANES:(hh + 1) * LANES]
    if maybe_bf16_refs:
        maybe_bf16_refs[1][...] = av.astype(BF16)
    ag = proj(3)
    ga_ref[...] = ag * _sigmoid(ag)
    hq = proj(4)
    hq_ref[...] = hq * _sigmoid(hq)
    hf = proj(5)
    lbl = lbl_ref[...]
    l0, l1 = lbl[0:1], lbl[1:2]
    mx = jnp.maximum(l0, l1)
    e0, e1 = jnp.exp(l0 - mx), jnp.exp(l1 - mx)
    lb = e0 / (e0 + e1)
    f_ref[...] = lb + (1.0 - lb) * _sigmoid(hf)
    hi_ref[...] = proj(6)
    hg = proj(7)
    gh_ref[...] = hg * _sigmoid(hg)


def _rope_tables(pos):
    half = ROT_DIM // 2
    inv = ROPE_THETA ** (-jnp.arange(0, ROT_DIM, 2, dtype=F32) / ROT_DIM)
    ang = inv[:, None] * pos.astype(F32)[None, :]
    lane = np.arange(LANES) % A_DQK
    freq = np.arange(half)[:, None] == (lane % half)[None, :]
    spread_c = jnp.asarray(freq & (lane < ROT_DIM)[None, :], F32)
    spread_a = jnp.asarray(-1.0 * (freq & (lane < half)[None, :]), F32)
    spread_b = jnp.asarray(freq & ((lane >= half) & (lane < ROT_DIM))[None, :], F32)
    expand = functools.partial(lax.dot_general, dimension_numbers=(((0,), (0,)), ((), ())),
                               precision=lax.Precision.HIGHEST)
    cos = expand(jnp.cos(ang), spread_c) + jnp.asarray(lane >= ROT_DIM, F32)[None, :]
    sin = jnp.sin(ang)
    return cos, expand(sin, spread_a), expand(sin, spread_b)


def _group_mean_matrix():
    g = np.zeros((2 * LANES, 2 * LANES), np.float32)
    for s in range(0, 2 * LANES, A_DQK):
        g[s:s + A_DQK, s:s + A_DQK] = 1.0 / A_DQK
    return jnp.asarray(g, BF16)


def _inproj(x2d, pos, n_pos_blocks, tm, with_bf16, norm_g, w_in_bf16, q_norm_g, k_norm_g, lb_logits):
    rows, d_model = x2d.shape
    d_in = w_in_bf16.shape[1]
    cos, sa, sb = _rope_tables(pos)
    qg = jnp.concatenate([q_norm_g, q_norm_g]).reshape(1, LANES)
    kg = jnp.concatenate([k_norm_g, k_norm_g]).reshape(1, LANES)
    const = lambda i: (0, 0)
    row_blk = lambda i: (i, 0)
    pos_blk = lambda i: (i % n_pos_blocks, 0)
    out_f32 = jax.ShapeDtypeStruct((rows, SEG), F32)
    out_shape = [out_f32] * 8
    out_specs = [pl.BlockSpec((tm, SEG), row_blk)] * 8
    for i in (1, 2):
        out_shape[i] = jax.ShapeDtypeStruct((rows * A_HEADS, LANES), F32)
        out_specs[i] = pl.BlockSpec((tm * A_HEADS, LANES), row_blk)
    if with_bf16:
        out_shape += [jax.ShapeDtypeStruct((rows, SEG), BF16)] * 2
        out_specs += [pl.BlockSpec((tm, SEG), row_blk)] * 2
    return pl.pallas_call(
        _inproj_kernel,
        grid=(rows // tm,),
        in_specs=[
            pl.BlockSpec((tm, d_model), row_blk),
            pl.BlockSpec((1, d_model), const),
            pl.BlockSpec((d_model, d_in), const, pipeline_mode=pl.Buffered(1)),
            pl.BlockSpec((1, LANES), const),
            pl.BlockSpec((1, LANES), const),
            pl.BlockSpec((tm, LANES), pos_blk),
            pl.BlockSpec((tm, LANES), pos_blk),
            pl.BlockSpec((tm, LANES), pos_blk),
            pl.BlockSpec((2, SEG), const),
            pl.BlockSpec((2 * LANES, 2 * LANES), const),
        ],
        out_specs=out_specs,
        out_shape=out_shape,
        compiler_params=pltpu.CompilerParams(dimension_semantics=("parallel",), vmem_limit_bytes=VMEM_LIMIT),
        name="inproj",
    )(x2d, norm_g.reshape(1, d_model), w_in_bf16, qg, kg, cos, sa, sb, lb_logits, _group_mean_matrix())


def _stack_maps(q_bf16):
    lane = lax.broadcasted_iota(jnp.int32, q_bf16.shape, 1)
    zero = jnp.zeros_like(q_bf16)
    return jnp.concatenate([jnp.where(lane < A_DQK, q_bf16, zero), jnp.where(lane >= A_DQK, q_bf16, zero)], axis=0)


def _softmax_block_update(qs_parts, k, v1, plan, m_ref, l_ref, acc_ref):
    width = k.shape[0]
    n = qs_parts[0].shape[0]
    scores = [lax.dot_general(qs_parts[i], k, (((1,), (1,)), ((), ())), preferred_element_type=F32)
              for i, _ in plan]
    probs, alphas, m_news = [], [], []
    for (i, mask), s in zip(plan, scores):
        if mask is not None:
            s = jnp.where(mask, s, -jnp.inf)
        m_prev = m_ref[i * n:(i + 1) * n]
        m_new = jnp.maximum(m_prev, jnp.max(s, axis=1, keepdims=True))
        alphas.append(jnp.exp2(m_prev - m_new))
        probs.append(jnp.exp2(s - jnp.concatenate([m_new] * (width // LANES), axis=1)).astype(BF16))
        m_news.append(m_new)
    for (i, _), p, alpha, m_new in zip(plan, probs, alphas, m_news):
        rows = slice(i * n, (i + 1) * n)
        pv = jnp.dot(p, v1, preferred_element_type=F32)
        acc_ref[rows] = alpha * acc_ref[rows] + pv[:, :A_DV]
        l_ref[rows] = alpha * l_ref[rows] + pv[:, A_DV:]
        m_ref[rows] = m_new


def _combine_maps(l0, acc0, l1, acc1, lam):
    return acc0 * (1.0 / l0) - lam * (acc1 * (1.0 / l1))


def _map_sum_matrix():
    g = np.zeros((LANES, 2 * LANES), np.float32)
    g[:A_DQK, :LANES] = 1.0
    g[A_DQK:, LANES:] = 1.0
    return jnp.asarray(g, BF16)


def _map_rows_matrix():
    g = np.zeros((2 * SUBLANES, LANES), np.float32)
    g[:SUBLANES, :A_DQK] = 1.0
    g[SUBLANES:, A_DQK:] = 1.0
    return jnp.asarray(g, BF16)


def _transposed_block_update(qs_parts, k, vt, plan, ref_ref, acc_ref):
    n = qs_parts[0].shape[0]
    scores = [lax.dot_general(k, qs_parts[i], (((1,), (1,)), ((), ())), preferred_element_type=F32)
              for i, _ in plan]
    for (i, mask), st in zip(plan, scores):
        cols = slice(i * n, (i + 1) * n)
        if mask is not None:
            st = jnp.where(mask, st, -jnp.inf)
        p = jnp.exp2(st - ref_ref[0:1, cols]).astype(BF16)
        acc_ref[:, cols] = acc_ref[:, cols] + jnp.dot(vt, p, preferred_element_type=F32)


def _diagonal_plans(qblk, kblk, part, keys_on_rows):
    per_map = qblk // part
    shape = (kblk, part) if keys_on_rows else (part, kblk)
    key = lax.broadcasted_iota(jnp.int32, shape, 0 if keys_on_rows else 1)
    qry = lax.broadcasted_iota(jnp.int32, shape, 1 if keys_on_rows else 0)
    plans = []
    for j in range(qblk // kblk):
        plan = []
        for c in range(2 * per_map):
            q0 = (c % per_map) * part
            if (j + 1) * kblk - 1 <= q0:
                plan.append((c, None))
            elif j * kblk <= q0 + part - 1:
                plan.append((c, key + j * kblk <= qry + q0))
        plans.append(plan)
    return plans


def _pattn_kernel(q_ref, k_ref, v_ref, lam_ref, ms_ref, mr_ref, o_ref,
                  v1_ref, vt_ref, kn_ref, m_ref, l_ref, acc_ref, mt_ref, acct_ref, *, qblk, kblk):
    qi = pl.program_id(2)
    seq = k_ref.shape[0]
    ratio = qblk // kblk

    @pl.when(qi == 0)
    def _():
        v1_ref[:, :A_DV] = v_ref[...]
        v1_ref[:, A_DV:] = jnp.ones((seq, LANES), BF16)
        for c in range(seq // kblk):
            vt_ref[:A_DV, c * kblk:(c + 1) * kblk] = jnp.transpose(v_ref[c * kblk:(c + 1) * kblk, :])
        vt_ref[A_DV:, :] = jnp.ones((vt_ref.shape[0] - A_DV, seq), BF16)
        k = k_ref[...]
        kn_ref[...] = jnp.max(jnp.dot(k * k, ms_ref[...], preferred_element_type=F32), axis=0, keepdims=True)

    q = q_ref[...].astype(BF16)
    qs = _stack_maps(q)
    part = ATTN_GROUP
    n_groups = 2 * qblk // part
    qs_parts = [qs[i * part:(i + 1) * part] for i in range(n_groups)]
    every_group = [(i, None) for i in range(n_groups)]

    qn = lax.dot_general(mr_ref[...], q * q, (((1,), (1,)), ((), ())), preferred_element_type=F32)
    kn = kn_ref[...]
    kn0 = jnp.concatenate([kn[:, :LANES]] * (qblk // LANES), axis=1)
    kn1 = jnp.concatenate([kn[:, LANES:]] * (qblk // LANES), axis=1)
    bound = jnp.concatenate([jnp.sqrt(qn[:SUBLANES] * kn0), jnp.sqrt(qn[SUBLANES:] * kn1)], axis=1) * 1.02
    mt_ref[...] = bound
    bounded = jnp.max(bound) <= SCORE_BOUND_LIMIT

    def run(block, diag_plans):
        def body(t, carry):
            for u in range(KEY_UNROLL):
                block(KEY_UNROLL * t + u, every_group)
            return carry

        lax.fori_loop(0, qi * (ratio // KEY_UNROLL), body, 0)
        for j, plan in enumerate(diag_plans):
            block(ratio * qi + j, plan)

    lam = _lam_from_ref(lam_ref)

    @pl.when(bounded)
    def _():
        acct_ref[...] = jnp.zeros(acct_ref.shape, F32)

        def block(kb, plan):
            start = pl.multiple_of(kb * kblk, kblk)
            _transposed_block_update(qs_parts, k_ref[pl.ds(start, kblk), :], vt_ref[:, pl.ds(start, kblk)],
                                     plan, mt_ref, acct_ref)

        run(block, _diagonal_plans(qblk, kblk, part, True))
        acc = acct_ref[...]
        out_t = _combine_maps(acc[A_DV:A_DV + 1, :qblk], acc[:A_DV, :qblk],
                              acc[A_DV:A_DV + 1, qblk:], acc[:A_DV, qblk:], lam)
        o_ref[...] = jnp.transpose(out_t)

    @pl.when(jnp.logical_not(bounded))
    def _():
        m_ref[...] = jnp.full(m_ref.shape, -jnp.inf, F32)
        l_ref[...] = jnp.zeros(l_ref.shape, F32)
        acc_ref[...] = jnp.zeros(acc_ref.shape, F32)

        def block(kb, plan):
            start = pl.multiple_of(kb * kblk, kblk)
            _softmax_block_update(qs_parts, k_ref[pl.ds(start, kblk), :], v1_ref[pl.ds(start, kblk), :],
                                  plan, m_ref, l_ref, acc_ref)

        run(block, _diagonal_plans(qblk, kblk, part, False))
        o_ref[...] = _combine_maps(l_ref[:qblk], acc_ref[:qblk], l_ref[qblk:], acc_ref[qblk:], lam)


def _prompt_attention(q, kb16, vb16, lam_params, batch, seq, qblk, kblk):
    nq = seq // qblk
    const = lambda b, h, i: (0, 0)
    return pl.pallas_call(
        functools.partial(_pattn_kernel, qblk=qblk, kblk=kblk),
        grid=(batch, A_HEADS, nq),
        in_specs=[
            pl.BlockSpec((qblk, LANES), lambda b, h, i: (b * nq + i, h)),
            pl.BlockSpec((seq, LANES), lambda b, h, i: (b, h)),
            pl.BlockSpec((seq, LANES), lambda b, h, i: (b, h)),
            pl.BlockSpec((4, A_DQK), const),
            pl.BlockSpec((LANES, 2 * LANES), const),
            pl.BlockSpec((2 * SUBLANES, LANES), const),
        ],
        out_specs=pl.BlockSpec((qblk, LANES), lambda b, h, i: (b * nq + i, h)),
        out_shape=jax.ShapeDtypeStruct((batch * seq, A_HEADS * A_DV), F32),
        scratch_shapes=[
            pltpu.VMEM((seq, 2 * LANES), BF16),
            pltpu.VMEM((A_DV + BF16_ROWS, seq), BF16),
            pltpu.VMEM((1, 2 * LANES), F32),
            pltpu.VMEM((2 * qblk, LANES), F32),
            pltpu.VMEM((2 * qblk, LANES), F32),
            pltpu.VMEM((2 * qblk, A_DV), F32),
            pltpu.VMEM((SUBLANES, 2 * qblk), F32),
            pltpu.VMEM((A_DV + BF16_ROWS, 2 * qblk), F32),
        ],
        compiler_params=pltpu.CompilerParams(
            dimension_semantics=("parallel", "parallel", "arbitrary"), vmem_limit_bytes=VMEM_LIMIT),
        name="prompt_attn",
    )(q, kb16, vb16, lam_params, _map_sum_matrix(), _map_rows_matrix())


def _partial_softmax(qs, k, v, mask):
    s = lax.dot_general(qs, k, (((1,), (1,)), ((), ())), preferred_element_type=F32)
    s = jnp.where(mask, s, -jnp.inf)
    m = jnp.max(s, axis=1, keepdims=True)
    p = jnp.exp2(s - m)
    return m, jnp.sum(p, axis=1, keepdims=True), jnp.dot(p.astype(BF16), v, preferred_element_type=F32)


def _sattn_kernel(pt_ref, q_ref, kn_ref, vn_ref, lam_ref, ck_hbm, cv_hbm, o_ref, kbuf, vbuf, sems,
                  *, n_pages, group, n_new):
    b = pl.program_id(0)
    n_slots = kbuf.shape[0]

    def page_copies(elem, slot):
        copies = []
        for i in range(n_pages):
            page = pt_ref[elem * n_pages + i]
            copies.append(pltpu.make_async_copy(ck_hbm.at[page], kbuf.at[slot, i], sems.at[0, slot]))
            copies.append(pltpu.make_async_copy(cv_hbm.at[page], vbuf.at[slot, i], sems.at[1, slot]))
        return copies

    @pl.when(b == 0)
    def _():
        for first in range(n_slots - 1):
            for cp in page_copies(first, first):
                cp.start()

    @pl.when(b + n_slots - 1 < pl.num_programs(0))
    def _():
        for cp in page_copies(b + n_slots - 1, (b + n_slots - 1) % n_slots):
            cp.start()

    q = q_ref[...].astype(BF16)
    qs = jnp.concatenate([_stack_maps(q[:, h * LANES:(h + 1) * LANES]) for h in range(A_HEADS)], axis=0)
    n_rows = 2 * n_new * A_HEADS

    def same_head(width):
        row = lax.broadcasted_iota(jnp.int32, (n_rows, width), 0)
        col = lax.broadcasted_iota(jnp.int32, (n_rows, width), 1)
        return row, col, (col % A_HEADS) == (row // (2 * n_new))

    n_kv = n_new * A_HEADS
    pad = jnp.zeros((LANES - n_kv, LANES), F32)
    row, col, ok = same_head(LANES)
    parts = [_partial_softmax(qs, jnp.concatenate([kn_ref[...], pad], axis=0).astype(BF16),
                              jnp.concatenate([vn_ref[...], pad], axis=0).astype(BF16),
                              ok & ((col // A_HEADS) <= (row % n_new)))]

    slot = b % n_slots
    for cp in page_copies(b, slot):
        cp.wait()
    rows_per_page = kbuf.shape[2]
    page_mask = same_head(group * rows_per_page)[2]
    for g in range(0, n_pages, group):
        k = kbuf[slot, g:g + group].reshape(group * rows_per_page, LANES).astype(BF16)
        v = vbuf[slot, g:g + group].reshape(group * rows_per_page, LANES).astype(BF16)
        parts.append(_partial_softmax(qs, k, v, page_mask))
    m = functools.reduce(jnp.maximum, [p[0] for p in parts])
    weights = [jnp.exp2(p[0] - m) for p in parts]
    l = sum(w * p[1] for w, p in zip(weights, parts))
    acc = sum(w * p[2] for w, p in zip(weights, parts))
    lam = _lam_from_ref(lam_ref)
    for h in range(A_HEADS):
        r0 = h * 2 * n_new
        r1 = r0 + n_new
        o_ref[:, h * LANES:(h + 1) * LANES] = _combine_maps(
            l[r0:r1], acc[r0:r1], l[r1:r1 + n_new], acc[r1:r1 + n_new], lam)


def _sample_attention(q, k_new, v_new, cache_k, cache_v, page_table, lam_params, group):
    bs, n_pages = page_table.shape
    n_new = q.shape[0] // bs
    rows_per_page = cache_k.shape[1]
    row_blk = lambda b, pt: (b, 0)
    grid_spec = pltpu.PrefetchScalarGridSpec(
        num_scalar_prefetch=1,
        grid=(bs,),
        in_specs=[
            pl.BlockSpec((n_new, q.shape[1]), row_blk),
            pl.BlockSpec((n_new * A_HEADS, LANES), row_blk),
            pl.BlockSpec((n_new * A_HEADS, LANES), row_blk),
            pl.BlockSpec((4, A_DQK), lambda b, pt: (0, 0)),
            pl.BlockSpec(memory_space=pl.ANY),
            pl.BlockSpec(memory_space=pl.ANY),
        ],
        out_specs=pl.BlockSpec((n_new, q.shape[1]), row_blk),
        scratch_shapes=[
            pltpu.VMEM((PAGE_AHEAD + 1, n_pages, rows_per_page, LANES), F32),
            pltpu.VMEM((PAGE_AHEAD + 1, n_pages, rows_per_page, LANES), F32),
            pltpu.SemaphoreType.DMA((2, PAGE_AHEAD + 1)),
        ],
    )
    return pl.pallas_call(
        functools.partial(_sattn_kernel, n_pages=n_pages, group=group, n_new=n_new),
        grid_spec=grid_spec,
        out_shape=jax.ShapeDtypeStruct(q.shape, F32),
        compiler_params=pltpu.CompilerParams(dimension_semantics=("arbitrary",), vmem_limit_bytes=VMEM_LIMIT),
        name="sample_attn",
    )(page_table.reshape(-1), q, k_new, v_new, lam_params, cache_k, cache_v)


def _hgrn_kernel(q_ref, f_ref, v_ref, s0_ref, tri_ref, o_ref, s_ref, *, rows, seqs, chunks):
    @pl.when(pl.program_id(1) == 0)
    def _():
        s_ref[...] = s0_ref[...]

    n = HGRN_ROWS
    pad = n - rows
    r_idx = lax.broadcasted_iota(jnp.int32, (n, n), 0)
    c_idx = lax.broadcasted_iota(jnp.int32, (n, n), 1)
    causal = c_idx <= r_idx
    tri = tri_ref[...]
    units = [(e, h, c) for e in range(seqs) for h in range(H_HEADS) for c in range(chunks)]

    def rows_of(e, c):
        return slice((e * chunks + c) * rows, (e * chunks + c + 1) * rows)

    def lanes_of(h):
        return slice(h * LANES, (h + 1) * LANES)

    def padded(x, fill):
        return jnp.concatenate([x, jnp.full((pad, LANES), fill, F32)], axis=0) if pad else x

    logs = {}
    for (e, h, c) in units:
        hi, lo = _split_hi_lo(jnp.log(padded(f_ref[rows_of(e, c), lanes_of(h)], 1.0)))
        logs[e, h, c] = jnp.concatenate([hi, lo], axis=0)
    cums = {u: jnp.dot(tri, logs[u], preferred_element_type=F32) for u in units}

    q_end, k_end, vals, decay, att = {}, {}, {}, {}, {}
    for u in units:
        e, h, c = u
        b = cums[u]
        b_end = b[n - 1:n]
        b_mid = b[n // 2 - 1:n // 2]
        q = padded(q_ref[rows_of(e, c), lanes_of(h)], 0.0)
        k = 1.0 - padded(f_ref[rows_of(e, c), lanes_of(h)], 1.0)
        vals[u] = padded(v_ref[rows_of(e, c), lanes_of(h)], 0.0).astype(BF16)
        q_end[u] = (q * jnp.exp(b)).astype(BF16)
        k_end[u] = (k * jnp.exp(b_end - b)).astype(BF16)
        decay[u] = jnp.transpose(jnp.broadcast_to(jnp.exp(b_end), (8, LANES)))[:, 0:1]
        att[u] = lax.dot_general((q * jnp.exp(b - b_mid)).astype(BF16), (k * jnp.exp(b_mid - b)).astype(BF16),
                                 (((1,), (1,)), ((), ())), preferred_element_type=F32)
    kv = {u: lax.dot_general(k_end[u], vals[u], (((0,), (0,)), ((), ())), preferred_element_type=F32)
          for u in units}

    incoming = {}
    for e in range(seqs):
        for h in range(H_HEADS):
            state = s_ref[e, h]
            for c in range(chunks):
                incoming[e, h, c] = state.astype(BF16)
                state = decay[e, h, c] * state + kv[e, h, c]
            s_ref[e, h] = state
    for u in units:
        e, h, c = u
        lhs = jnp.concatenate([jnp.where(causal, att[u], 0.0).astype(BF16), q_end[u]], axis=1)
        o = jnp.dot(lhs, jnp.concatenate([vals[u], incoming[u]], axis=0), preferred_element_type=F32)
        o_ref[rows_of(e, c), lanes_of(h)] = o[:rows]


def _hgrn(hq, f, hi, s0, batch, rows, seqs, chunks):
    total = hq.shape[0]
    steps = total // (batch * rows * chunks)
    n = HGRN_ROWS
    tri = np.tril(np.ones((n, n), np.float32))
    tri = jnp.asarray(np.concatenate([tri, tri], axis=1), BF16)
    assert seqs == 1 or steps == 1
    blk = lambda b, c: (b * steps + c, 0)
    st = lambda b, c: (b, 0, 0, 0)
    return pl.pallas_call(
        functools.partial(_hgrn_kernel, rows=rows, seqs=seqs, chunks=chunks),
        grid=(batch // seqs, steps),
        in_specs=[
            pl.BlockSpec((seqs * chunks * rows, SEG), blk),
            pl.BlockSpec((seqs * chunks * rows, SEG), blk),
            pl.BlockSpec((seqs * chunks * rows, SEG), blk),
            pl.BlockSpec((seqs, H_HEADS, H_DK, H_DV), st),
            pl.BlockSpec((n, 2 * n), lambda b, c: (0, 0)),
        ],
        out_specs=[
            pl.BlockSpec((seqs * chunks * rows, SEG), blk),
            pl.BlockSpec((seqs, H_HEADS, H_DK, H_DV), st),
        ],
        out_shape=[
            jax.ShapeDtypeStruct((total, SEG), F32),
            jax.ShapeDtypeStruct((batch, H_HEADS, H_DK, H_DV), F32),
        ],
        compiler_params=pltpu.CompilerParams(
            dimension_semantics=("parallel", "arbitrary"), vmem_limit_bytes=VMEM_LIMIT),
        name="hgrn2",
    )(hq, f, hi, s0, tri)


def _merge_kernel(x_ref, oa_ref, ga_ref, oh_ref, gh_ref, sg_ref, hg_ref, w_ref, y_ref):
    def head_norm(o_ref, g_ref, h):
        o = o_ref[:, h * LANES:(h + 1) * LANES]
        return o * lax.rsqrt(jnp.mean(o * o, axis=-1, keepdims=True) + EPS) * g_ref[...]

    a = jnp.concatenate([head_norm(oa_ref, sg_ref, h) for h in range(A_HEADS)], axis=1)
    a = a * (1.0 - LAM_INIT) * ga_ref[...]
    r = jnp.concatenate([head_norm(oh_ref, hg_ref, h) for h in range(H_HEADS)], axis=1)
    r = r * gh_ref[...]
    mix = jnp.concatenate([a, r], axis=1).astype(BF16)
    y_ref[...] = x_ref[...] + jnp.dot(mix, w_ref[...], preferred_element_type=F32)


def _merge(x2d, o_attn, ga, o_hgrn, gh, subln_g, hgrn_norm_g, w_out_bf16, tm):
    rows, d_model = x2d.shape
    row_blk = lambda i: (i, 0)
    const = lambda i: (0, 0)
    return pl.pallas_call(
        _merge_kernel,
        grid=(rows // tm,),
        in_specs=[
            pl.BlockSpec((tm, d_model), row_blk),
            pl.BlockSpec((tm, SEG), row_blk),
            pl.BlockSpec((tm, SEG), row_blk),
            pl.BlockSpec((tm, SEG), row_blk),
            pl.BlockSpec((tm, SEG), row_blk),
            pl.BlockSpec((1, LANES), const),
            pl.BlockSpec((1, LANES), const),
            pl.BlockSpec(w_out_bf16.shape, const, pipeline_mode=pl.Buffered(1)),
        ],
        out_specs=pl.BlockSpec((tm, d_model), row_blk),
        out_shape=jax.ShapeDtypeStruct((rows, d_model), F32),
        compiler_params=pltpu.CompilerParams(dimension_semantics=("parallel",), vmem_limit_bytes=VMEM_LIMIT),
        name="merge",
    )(x2d, o_attn, ga, o_hgrn, gh, subln_g.reshape(1, LANES), hgrn_norm_g.reshape(1, LANES), w_out_bf16)


def kernel(x_prompt, x_sample, cache_k, cache_v, state_hgrn, page_table, norm_g, w_in, q_norm_g, k_norm_g,
           lambda_q1, lambda_k1, lambda_q2, lambda_k2, subln_g, hgrn_lb_logits, hgrn_norm_g, w_out):
    bp, tp, d_model = x_prompt.shape
    bs, ts, _ = x_sample.shape
    depth, n_pool, page_size = cache_k.shape[:3]
    assert depth == 1 and hgrn_lb_logits.shape[0] == 2
    assert (A_DV, H_DK, H_DV, 2 * A_DQK) == (LANES,) * 4
    past_len = page_table.shape[1] * page_size
    tm = 512
    attn_qblk, attn_kblk = 2048, 512

    w_in16 = w_in[0].astype(BF16)
    w_out16 = w_out[0].astype(BF16)
    lam_params = jnp.stack([lambda_q1[0], lambda_k1[0], lambda_q2[0], lambda_k2[0]])
    proj = functools.partial(_inproj, norm_g=norm_g[0], w_in_bf16=w_in16, q_norm_g=q_norm_g[0],
                             k_norm_g=k_norm_g[0], lb_logits=hgrn_lb_logits)
    fin = functools.partial(_merge, subln_g=subln_g[0], hgrn_norm_g=hgrn_norm_g[0], w_out_bf16=w_out16, tm=tm)

    xp = x_prompt.reshape(bp * tp, d_model)
    q, k, v, ga, hq, f, hi, gh, k16, v16 = proj(xp, jnp.arange(tp, dtype=jnp.int32), tp // tm, tm, True)
    o_attn = _prompt_attention(q, k16, v16, lam_params, bp, tp, attn_qblk, attn_kblk)
    o_hgrn, s_p = _hgrn(hq, f, hi, jnp.zeros((bp, H_HEADS, H_DK, H_DV), F32), bp, HGRN_ROWS, seqs=1, chunks=4)
    y_p = fin(xp, o_attn, ga, o_hgrn, gh)

    xs = x_sample.reshape(bs * ts, d_model)
    pos_s = past_len + jnp.tile(jnp.arange(ts, dtype=jnp.int32), tm // ts)
    sq, sk, sv, sga, shq, sf, shi, sgh = proj(xs, pos_s, 1, tm, False)
    ck = cache_k.reshape(n_pool, page_size * A_HEADS, 2 * A_DQK)
    cv = cache_v.reshape(n_pool, page_size * A_HEADS, A_DV)
    so_attn = _sample_attention(sq, sk, sv, ck, cv, page_table, lam_params, group=page_table.shape[1])
    so_hgrn, s_s = _hgrn(shq, sf, shi, state_hgrn[0], bs, ts, seqs=8, chunks=1)
    y_s = fin(xs, so_attn, sga, so_hgrn, sgh)

    return (y_p.reshape(bp, tp, d_model), y_s.reshape(bs, ts, d_model),
            k.reshape(1, bp, tp, A_HEADS, 2 * A_DQK), v.reshape(1, bp, tp, A_HEADS, A_DV), s_p[None],
            sk.reshape(1, bs, ts, A_HEADS, 2 * A_DQK), sv.reshape(1, bs, ts, A_HEADS, A_DV), s_s[None])
```

```python
import functools
import math

import numpy as np
import jax
import jax.numpy as jnp
from jax import lax
from jax.experimental import pallas as pl
from jax.experimental.pallas import tpu as pltpu

F32 = jnp.float32
BF16 = jnp.bfloat16

LANES = 128
SUBLANES = 8
BF16_ROWS = 16
A_HEADS = 4
A_DQK = 64
A_DV = 128
H_HEADS = 4
H_DK = 128
H_DV = 128
SEG = 512
ROT_DIM = A_DQK // 4
ROPE_THETA = 500000.0
EPS = 1e-6
LAM_INIT = 0.8 - 0.6 * math.exp(-0.3 * 0)
Q_SCALE = A_DQK ** -0.5 * math.log2(math.e)
HGRN_ROWS = 128
SCORE_BOUND_LIMIT = 60.0
PAGE_AHEAD = 2
ATTN_GROUP = 256
KEY_UNROLL = 2
VMEM_LIMIT = 48 * 1024 * 1024


def _sigmoid(x):
    return 1.0 / (1.0 + jnp.exp(-x))


def _split_hi_lo(x):
    hi = x.astype(BF16)
    lo = (x - hi.astype(F32)).astype(BF16)
    return hi, lo


def _lam_from_ref(lam_ref):
    lp = lam_ref[...]
    s1 = jnp.sum(lp[0:1] * lp[1:2], axis=1, keepdims=True)
    s2 = jnp.sum(lp[2:3] * lp[3:4], axis=1, keepdims=True)
    return jnp.exp(s1) - jnp.exp(s2) + LAM_INIT


def _inproj_kernel(x_ref, ng_ref, w_ref, qg_ref, kg_ref, cos_ref, sa_ref, sb_ref, lbl_ref, gm_ref,
                   q_ref, k_ref, v_ref, ga_ref, hq_ref, f_ref, hi_ref, gh_ref, *maybe_bf16_refs):
    x = x_ref[...]
    ms = jnp.mean(x * x, axis=-1, keepdims=True)
    h = (x * lax.rsqrt(ms + EPS) * ng_ref[...]).astype(BF16)

    def proj(seg):
        return jnp.dot(h, w_ref[:, seg * SEG:(seg + 1) * SEG], preferred_element_type=F32)

    cos, sa, sb = cos_ref[...], sa_ref[...], sb_ref[...]

    def norm_rope(y, g_ref, hh):
        pair = y[:, (hh // 2) * 2 * LANES:(hh // 2 + 1) * 2 * LANES]
        gms = jnp.dot((pair * pair).astype(BF16), gm_ref[...], preferred_element_type=F32)
        own = slice((hh % 2) * LANES, (hh % 2 + 1) * LANES)
        yn = pair[:, own] * lax.rsqrt(gms[:, own] + EPS) * g_ref[...]
        return yn * cos + pltpu.roll(yn, LANES - ROT_DIM // 2, 1) * sa + pltpu.roll(yn, ROT_DIM // 2, 1) * sb

    tm = x.shape[0]
    aq = proj(0)
    for hh in range(A_HEADS):
        q_ref[:, hh * LANES:(hh + 1) * LANES] = norm_rope(aq, qg_ref, hh) * Q_SCALE
    ak = proj(1)
    for hh in range(A_HEADS):
        kh = norm_rope(ak, kg_ref, hh)
        k_ref[pl.ds(hh, tm, stride=A_HEADS), :] = kh
        if maybe_bf16_refs:
            maybe_bf16_refs[0][:, hh * LANES:(hh + 1) * LANES] = kh.astype(BF16)
    av = proj(2)
    for hh in range(A_HEADS):
        v_ref[pl.ds(hh, tm, stride=A_HEADS), :] = av[:, hh * LANES:(hh + 1) * LANES]
    if maybe_bf16_refs:
        maybe_bf16_refs[1][...] = av.astype(BF16)
    ag = proj(3)
    ga_ref[...] = (ag * _sigmoid(ag)).astype(ga_ref.dtype)
    hq = proj(4)
    hq_ref[...] = hq * _sigmoid(hq)
    hf = proj(5)
    lbl = lbl_ref[...]
    l0, l1 = lbl[0:1], lbl[1:2]
    mx = jnp.maximum(l0, l1)
    e0, e1 = jnp.exp(l0 - mx), jnp.exp(l1 - mx)
    lb = e0 / (e0 + e1)
    f_ref[...] = lb + (1.0 - lb) * _sigmoid(hf)
    hi_ref[...] = proj(6)
    hg = proj(7)
    gh_ref[...] = (hg * _sigmoid(hg)).astype(gh_ref.dtype)


def _rope_tables(positions):
    half = ROT_DIM // 2
    inv = np.float32(ROPE_THETA) ** (-np.arange(0, ROT_DIM, 2, dtype=np.float32) / np.float32(ROT_DIM))
    lane = np.arange(LANES) % A_DQK
    ang = positions.astype(np.float32)[:, None] * inv[lane % half][None, :].astype(np.float32)
    cos = np.where((lane < ROT_DIM)[None, :], np.cos(ang.astype(np.float64)), 1.0)
    sin = np.sin(ang.astype(np.float64))
    a = np.where((lane < half)[None, :], -sin, 0.0)
    b = np.where(((lane >= half) & (lane < ROT_DIM))[None, :], sin, 0.0)
    return tuple(jnp.asarray(t, F32) for t in (cos, a, b))


def _group_mean_matrix():
    g = np.zeros((2 * LANES, 2 * LANES), np.float32)
    for s in range(0, 2 * LANES, A_DQK):
        g[s:s + A_DQK, s:s + A_DQK] = 1.0 / A_DQK
    return jnp.asarray(g, BF16)


def _inproj(x2d, pos, n_pos_blocks, tm, with_bf16, norm_g, w_in_bf16, q_norm_g, k_norm_g, lb_logits):
    rows, d_model = x2d.shape
    d_in = w_in_bf16.shape[1]
    cos, sa, sb = _rope_tables(pos)
    qg = jnp.concatenate([q_norm_g, q_norm_g]).reshape(1, LANES)
    kg = jnp.concatenate([k_norm_g, k_norm_g]).reshape(1, LANES)
    const = lambda i: (0, 0)
    row_blk = lambda i: (i, 0)
    pos_blk = lambda i: (i % n_pos_blocks, 0)
    out_f32 = jax.ShapeDtypeStruct((rows, SEG), F32)
    out_shape = [out_f32] * 8
    out_specs = [pl.BlockSpec((tm, SEG), row_blk)] * 8
    for i in (3, 7):
        out_shape[i] = jax.ShapeDtypeStruct((rows, SEG), BF16)
    for i in (1, 2):
        out_shape[i] = jax.ShapeDtypeStruct((rows * A_HEADS, LANES), F32)
        out_specs[i] = pl.BlockSpec((tm * A_HEADS, LANES), row_blk)
    if with_bf16:
        out_shape += [jax.ShapeDtypeStruct((rows, SEG), BF16)] * 2
        out_specs += [pl.BlockSpec((tm, SEG), row_blk)] * 2
    return pl.pallas_call(
        _inproj_kernel,
        grid=(rows // tm,),
        in_specs=[
            pl.BlockSpec((tm, d_model), row_blk),
            pl.BlockSpec((1, d_model), const),
            pl.BlockSpec((d_model, d_in), const, pipeline_mode=pl.Buffered(1)),
            pl.BlockSpec((1, LANES), const),
            pl.BlockSpec((1, LANES), const),
            pl.BlockSpec((tm, LANES), pos_blk),
            pl.BlockSpec((tm, LANES), pos_blk),
            pl.BlockSpec((tm, LANES), pos_blk),
            pl.BlockSpec((2, SEG), const),
            pl.BlockSpec((2 * LANES, 2 * LANES), const),
        ],
        out_specs=out_specs,
        out_shape=out_shape,
        compiler_params=pltpu.CompilerParams(dimension_semantics=("parallel",), vmem_limit_bytes=VMEM_LIMIT),
        name="inproj",
    )(x2d, norm_g.reshape(1, d_model), w_in_bf16, qg, kg, cos, sa, sb, lb_logits, _group_mean_matrix())


def _stack_maps(q_bf16):
    lane = lax.broadcasted_iota(jnp.int32, q_bf16.shape, 1)
    zero = jnp.zeros_like(q_bf16)
    return jnp.concatenate([jnp.where(lane < A_DQK, q_bf16, zero), jnp.where(lane >= A_DQK, q_bf16, zero)], axis=0)


def _softmax_block_update(qs_parts, k, v1, plan, m_ref, l_ref, acc_ref):
    width = k.shape[0]
    n = qs_parts[0].shape[0]
    scores = [lax.dot_general(qs_parts[i], k, (((1,), (1,)), ((), ())), preferred_element_type=F32)
              for i, _ in plan]
    probs, alphas, m_news = [], [], []
    for (i, mask), s in zip(plan, scores):
        if mask is not None:
            s = jnp.where(mask, s, -jnp.inf)
        m_prev = m_ref[i * n:(i + 1) * n]
        m_new = jnp.maximum(m_prev, jnp.max(s, axis=1, keepdims=True))
        alphas.append(jnp.exp2(m_prev - m_new))
        probs.append(jnp.exp2(s - jnp.concatenate([m_new] * (width // LANES), axis=1)).astype(BF16))
        m_news.append(m_new)
    for (i, _), p, alpha, m_new in zip(plan, probs, alphas, m_news):
        rows = slice(i * n, (i + 1) * n)
        pv = jnp.dot(p, v1, preferred_element_type=F32)
        acc_ref[rows] = alpha * acc_ref[rows] + pv[:, :A_DV]
        l_ref[rows] = alpha * l_ref[rows] + pv[:, A_DV:]
        m_ref[rows] = m_new


def _combine_maps(l0, acc0, l1, acc1, lam):
    return acc0 * (1.0 / l0) - lam * (acc1 * (1.0 / l1))


def _map_sum_matrix():
    g = np.zeros((LANES, 2 * LANES), np.float32)
    g[:A_DQK, :LANES] = 1.0
    g[A_DQK:, LANES:] = 1.0
    return jnp.asarray(g, BF16)


def _map_rows_matrix():
    g = np.zeros((2 * SUBLANES, LANES), np.float32)
    g[:SUBLANES, :A_DQK] = 1.0
    g[SUBLANES:, A_DQK:] = 1.0
    return jnp.asarray(g, BF16)


def _transposed_block_update(qs_parts, k, vt, plan, ref_ref, acc_ref):
    n = qs_parts[0].shape[0]
    scores = [lax.dot_general(k, qs_parts[i], (((1,), (1,)), ((), ())), preferred_element_type=F32)
              for i, _ in plan]
    for (i, mask), st in zip(plan, scores):
        cols = slice(i * n, (i + 1) * n)
        if mask is not None:
            st = jnp.where(mask, st, -jnp.inf)
        p = jnp.exp2(st - ref_ref[0:1, cols]).astype(BF16)
        acc_ref[:, cols] = acc_ref[:, cols] + jnp.dot(vt, p, preferred_element_type=F32)


def _diagonal_plans(qblk, kblk, part, keys_on_rows):
    per_map = qblk // part
    shape = (kblk, part) if keys_on_rows else (part, kblk)
    key = lax.broadcasted_iota(jnp.int32, shape, 0 if keys_on_rows else 1)
    qry = lax.broadcasted_iota(jnp.int32, shape, 1 if keys_on_rows else 0)
    plans = []
    for j in range(qblk // kblk):
        plan = []
        for c in range(2 * per_map):
            q0 = (c % per_map) * part
            if (j + 1) * kblk - 1 <= q0:
                plan.append((c, None))
            elif j * kblk <= q0 + part - 1:
                plan.append((c, key + j * kblk <= qry + q0))
        plans.append(plan)
    return plans


def _pattn_kernel(q_ref, k_ref, v_ref, lam_ref, ms_ref, mr_ref, o_ref,
                  v1_ref, vt_ref, kn_ref, m_ref, l_ref, acc_ref, mt_ref, acct_ref, *, qblk, kblk):
    qi = pl.program_id(2)
    seq = k_ref.shape[0]
    ratio = qblk // kblk

    @pl.when(qi == 0)
    def _():
        v1_ref[:, :A_DV] = v_ref[...]
        v1_ref[:, A_DV:] = jnp.ones((seq, LANES), BF16)
        for c in range(seq // kblk):
            vt_ref[:A_DV, c * kblk:(c + 1) * kblk] = jnp.transpose(v_ref[c * kblk:(c + 1) * kblk, :])
        vt_ref[A_DV:, :] = jnp.ones((vt_ref.shape[0] - A_DV, seq), BF16)
        k = k_ref[...]
        kn_ref[...] = jnp.max(jnp.dot(k * k, ms_ref[...], preferred_element_type=F32), axis=0, keepdims=True)

    q = q_ref[...].astype(BF16)
    qs = _stack_maps(q)
    part = ATTN_GROUP
    n_groups = 2 * qblk // part
    qs_parts = [qs[i * part:(i + 1) * part] for i in range(n_groups)]
    every_group = [(i, None) for i in range(n_groups)]

    qn = lax.dot_general(mr_ref[...], q * q, (((1,), (1,)), ((), ())), preferred_element_type=F32)
    kn = kn_ref[...]
    kn0 = jnp.concatenate([kn[:, :LANES]] * (qblk // LANES), axis=1)
    kn1 = jnp.concatenate([kn[:, LANES:]] * (qblk // LANES), axis=1)
    bound = jnp.concatenate([jnp.sqrt(qn[:SUBLANES] * kn0), jnp.sqrt(qn[SUBLANES:] * kn1)], axis=1) * 1.02
    mt_ref[...] = bound
    bounded = jnp.max(bound) <= SCORE_BOUND_LIMIT

    def run(block, diag_plans):
        def body(t, carry):
            for u in range(KEY_UNROLL):
                block(KEY_UNROLL * t + u, every_group)
            return carry

        lax.fori_loop(0, qi * (ratio // KEY_UNROLL), body, 0)
        for j, plan in enumerate(diag_plans):
            block(ratio * qi + j, plan)

    lam = _lam_from_ref(lam_ref)

    @pl.when(bounded)
    def _():
        acct_ref[...] = jnp.zeros(acct_ref.shape, F32)

        def block(kb, plan):
            start = pl.multiple_of(kb * kblk, kblk)
            _transposed_block_update(qs_parts, k_ref[pl.ds(start, kblk), :], vt_ref[:, pl.ds(start, kblk)],
                                     plan, mt_ref, acct_ref)

        run(block, _diagonal_plans(qblk, kblk, part, True))
        acc = acct_ref[...]
        out_t = _combine_maps(acc[A_DV:A_DV + 1, :qblk], acc[:A_DV, :qblk],
                              acc[A_DV:A_DV + 1, qblk:], acc[:A_DV, qblk:], lam)
        o_ref[...] = jnp.transpose(out_t).astype(o_ref.dtype)

    @pl.when(jnp.logical_not(bounded))
    def _():
        m_ref[...] = jnp.full(m_ref.shape, -jnp.inf, F32)
        l_ref[...] = jnp.zeros(l_ref.shape, F32)
        acc_ref[...] = jnp.zeros(acc_ref.shape, F32)

        def block(kb, plan):
            start = pl.multiple_of(kb * kblk, kblk)
            _softmax_block_update(qs_parts, k_ref[pl.ds(start, kblk), :], v1_ref[pl.ds(start, kblk), :],
                                  plan, m_ref, l_ref, acc_ref)

        run(block, _diagonal_plans(qblk, kblk, part, False))
        o_ref[...] = _combine_maps(l_ref[:qblk], acc_ref[:qblk], l_ref[qblk:], acc_ref[qblk:], lam).astype(o_ref.dtype)


def _prompt_attention(q, kb16, vb16, lam_params, batch, seq, qblk, kblk):
    nq = seq // qblk
    const = lambda b, h, i: (0, 0)
    return pl.pallas_call(
        functools.partial(_pattn_kernel, qblk=qblk, kblk=kblk),
        grid=(batch, A_HEADS, nq),
        in_specs=[
            pl.BlockSpec((qblk, LANES), lambda b, h, i: (b * nq + i, h)),
            pl.BlockSpec((seq, LANES), lambda b, h, i: (b, h)),
            pl.BlockSpec((seq, LANES), lambda b, h, i: (b, h)),
            pl.BlockSpec((4, A_DQK), const),
            pl.BlockSpec((LANES, 2 * LANES), const),
            pl.BlockSpec((2 * SUBLANES, LANES), const),
        ],
        out_specs=pl.BlockSpec((qblk, LANES), lambda b, h, i: (b * nq + i, h)),
        out_shape=jax.ShapeDtypeStruct((batch * seq, A_HEADS * A_DV), BF16),
        scratch_shapes=[
            pltpu.VMEM((seq, 2 * LANES), BF16),
            pltpu.VMEM((A_DV + BF16_ROWS, seq), BF16),
            pltpu.VMEM((1, 2 * LANES), F32),
            pltpu.VMEM((2 * qblk, LANES), F32),
            pltpu.VMEM((2 * qblk, LANES), F32),
            pltpu.VMEM((2 * qblk, A_DV), F32),
            pltpu.VMEM((SUBLANES, 2 * qblk), F32),
            pltpu.VMEM((A_DV + BF16_ROWS, 2 * qblk), F32),
        ],
        compiler_params=pltpu.CompilerParams(
            dimension_semantics=("parallel", "parallel", "arbitrary"), vmem_limit_bytes=VMEM_LIMIT),
        name="prompt_attn",
    )(q, kb16, vb16, lam_params, _map_sum_matrix(), _map_rows_matrix())


def _partial_softmax(qs, k, v, mask):
    s = lax.dot_general(qs, k, (((1,), (1,)), ((), ())), preferred_element_type=F32)
    s = jnp.where(mask, s, -jnp.inf)
    m = jnp.max(s, axis=1, keepdims=True)
    p = jnp.exp2(s - m)
    return m, jnp.sum(p, axis=1, keepdims=True), jnp.dot(p.astype(BF16), v, preferred_element_type=F32)


def _sattn_kernel(pt_ref, q_ref, kn_ref, vn_ref, lam_ref, ck_hbm, cv_hbm, o_ref, kbuf, vbuf, sems,
                  *, n_pages, group, n_new):
    b = pl.program_id(0)
    n_slots = kbuf.shape[0]

    def page_copies(elem, slot):
        copies = []
        for i in range(n_pages):
            page = pt_ref[elem * n_pages + i]
            copies.append(pltpu.make_async_copy(ck_hbm.at[page], kbuf.at[slot, i], sems.at[0, slot]))
            copies.append(pltpu.make_async_copy(cv_hbm.at[page], vbuf.at[slot, i], sems.at[1, slot]))
        return copies

    @pl.when(b == 0)
    def _():
        for first in range(n_slots - 1):
            for cp in page_copies(first, first):
                cp.start()

    @pl.when(b + n_slots - 1 < pl.num_programs(0))
    def _():
        for cp in page_copies(b + n_slots - 1, (b + n_slots - 1) % n_slots):
            cp.start()

    q = q_ref[...].astype(BF16)
    qs = jnp.concatenate([_stack_maps(q[:, h * LANES:(h + 1) * LANES]) for h in range(A_HEADS)], axis=0)
    n_rows = 2 * n_new * A_HEADS

    def same_head(width):
        row = lax.broadcasted_iota(jnp.int32, (n_rows, width), 0)
        col = lax.broadcasted_iota(jnp.int32, (n_rows, width), 1)
        return row, col, (col % A_HEADS) == (row // (2 * n_new))

    n_kv = n_new * A_HEADS
    pad = jnp.zeros((LANES - n_kv, LANES), F32)
    row, col, ok = same_head(LANES)
    parts = [_partial_softmax(qs, jnp.concatenate([kn_ref[...], pad], axis=0).astype(BF16),
                              jnp.concatenate([vn_ref[...], pad], axis=0).astype(BF16),
                              ok & ((col // A_HEADS) <= (row % n_new)))]

    slot = b % n_slots
    for cp in page_copies(b, slot):
        cp.wait()
    rows_per_page = kbuf.shape[2]
    page_mask = same_head(group * rows_per_page)[2]
    for g in range(0, n_pages, group):
        k = kbuf[slot, g:g + group].reshape(group * rows_per_page, LANES).astype(BF16)
        v = vbuf[slot, g:g + group].reshape(group * rows_per_page, LANES).astype(BF16)
        parts.append(_partial_softmax(qs, k, v, page_mask))
    m = functools.reduce(jnp.maximum, [p[0] for p in parts])
    weights = [jnp.exp2(p[0] - m) for p in parts]
    l = sum(w * p[1] for w, p in zip(weights, parts))
    acc = sum(w * p[2] for w, p in zip(weights, parts))
    lam = _lam_from_ref(lam_ref)
    for h in range(A_HEADS):
        r0 = h * 2 * n_new
        r1 = r0 + n_new
        o_ref[:, h * LANES:(h + 1) * LANES] = _combine_maps(
            l[r0:r1], acc[r0:r1], l[r1:r1 + n_new], acc[r1:r1 + n_new], lam)


def _sample_attention(q, k_new, v_new, cache_k, cache_v, page_table, lam_params, group):
    bs, n_pages = page_table.shape
    n_new = q.shape[0] // bs
    rows_per_page = cache_k.shape[1]
    row_blk = lambda b, pt: (b, 0)
    grid_spec = pltpu.PrefetchScalarGridSpec(
        num_scalar_prefetch=1,
        grid=(bs,),
        in_specs=[
            pl.BlockSpec((n_new, q.shape[1]), row_blk),
            pl.BlockSpec((n_new * A_HEADS, LANES), row_blk),
            pl.BlockSpec((n_new * A_HEADS, LANES), row_blk),
            pl.BlockSpec((4, A_DQK), lambda b, pt: (0, 0)),
            pl.BlockSpec(memory_space=pl.ANY),
            pl.BlockSpec(memory_space=pl.ANY),
        ],
        out_specs=pl.BlockSpec((n_new, q.shape[1]), row_blk),
        scratch_shapes=[
            pltpu.VMEM((PAGE_AHEAD + 1, n_pages, rows_per_page, LANES), F32),
            pltpu.VMEM((PAGE_AHEAD + 1, n_pages, rows_per_page, LANES), F32),
            pltpu.SemaphoreType.DMA((2, PAGE_AHEAD + 1)),
        ],
    )
    return pl.pallas_call(
        functools.partial(_sattn_kernel, n_pages=n_pages, group=group, n_new=n_new),
        grid_spec=grid_spec,
        out_shape=jax.ShapeDtypeStruct(q.shape, F32),
        compiler_params=pltpu.CompilerParams(dimension_semantics=("arbitrary",), vmem_limit_bytes=VMEM_LIMIT),
        name="sample_attn",
    )(page_table.reshape(-1), q, k_new, v_new, lam_params, cache_k, cache_v)


def _hgrn_kernel(q_ref, f_ref, v_ref, s0_ref, tri_ref, o_ref, s_ref, *, rows, seqs, chunks):
    @pl.when(pl.program_id(1) == 0)
    def _():
        s_ref[...] = s0_ref[...]

    n = HGRN_ROWS
    pad = n - rows
    r_idx = lax.broadcasted_iota(jnp.int32, (n, n), 0)
    c_idx = lax.broadcasted_iota(jnp.int32, (n, n), 1)
    causal = c_idx <= r_idx
    tri = tri_ref[...]
    units = [(e, h, c) for e in range(seqs) for h in range(H_HEADS) for c in range(chunks)]

    def rows_of(e, c):
        return slice((e * chunks + c) * rows, (e * chunks + c + 1) * rows)

    def lanes_of(h):
        return slice(h * LANES, (h + 1) * LANES)

    def padded(x, fill):
        return jnp.concatenate([x, jnp.full((pad, LANES), fill, F32)], axis=0) if pad else x

    logs = {}
    for (e, h, c) in units:
        hi, lo = _split_hi_lo(jnp.log2(padded(f_ref[rows_of(e, c), lanes_of(h)], 1.0)))
        logs[e, h, c] = jnp.concatenate([hi, lo], axis=0)
    cums = {u: jnp.dot(tri, logs[u], preferred_element_type=F32) for u in units}

    q_end, k_end, vals, decay, att = {}, {}, {}, {}, {}
    for u in units:
        e, h, c = u
        b = cums[u]
        b_end = b[n - 1:n]
        b_mid = b[n // 2 - 1:n // 2]
        q = padded(q_ref[rows_of(e, c), lanes_of(h)], 0.0)
        k = 1.0 - padded(f_ref[rows_of(e, c), lanes_of(h)], 1.0)
        vals[u] = padded(v_ref[rows_of(e, c), lanes_of(h)], 0.0).astype(BF16)
        q_end[u] = (q * jnp.exp2(b)).astype(BF16)
        k_end[u] = (k * jnp.exp2(b_end - b)).astype(BF16)
        decay[u] = jnp.transpose(jnp.broadcast_to(jnp.exp2(b_end), (8, LANES)))[:, 0:1]
        att[u] = lax.dot_general((q * jnp.exp2(b - b_mid)).astype(BF16), (k * jnp.exp2(b_mid - b)).astype(BF16),
                                 (((1,), (1,)), ((), ())), preferred_element_type=F32)
    kv = {u: lax.dot_general(k_end[u], vals[u], (((0,), (0,)), ((), ())), preferred_element_type=F32)
          for u in units}

    incoming = {}
    for e in range(seqs):
        for h in range(H_HEADS):
            state = s_ref[e, h]
            for c in range(chunks):
                incoming[e, h, c] = state.astype(BF16)
                state = decay[e, h, c] * state + kv[e, h, c]
            s_ref[e, h] = state
    for u in units:
        e, h, c = u
        lhs = jnp.concatenate([jnp.where(causal, att[u], 0.0).astype(BF16), q_end[u]], axis=1)
        o = jnp.dot(lhs, jnp.concatenate([vals[u], incoming[u]], axis=0), preferred_element_type=F32)
        o_ref[rows_of(e, c), lanes_of(h)] = o[:rows].astype(o_ref.dtype)


def _hgrn(hq, f, hi, s0, batch, rows, seqs, chunks):
    total = hq.shape[0]
    steps = total // (batch * rows * chunks)
    n = HGRN_ROWS
    tri = np.tril(np.ones((n, n), np.float32))
    tri = jnp.asarray(np.concatenate([tri, tri], axis=1), BF16)
    assert seqs == 1 or steps == 1
    blk = lambda b, c: (b * steps + c, 0)
    st = lambda b, c: (b, 0, 0, 0)
    return pl.pallas_call(
        functools.partial(_hgrn_kernel, rows=rows, seqs=seqs, chunks=chunks),
        grid=(batch // seqs, steps),
        in_specs=[
            pl.BlockSpec((seqs * chunks * rows, SEG), blk),
            pl.BlockSpec((seqs * chunks * rows, SEG), blk),
            pl.BlockSpec((seqs * chunks * rows, SEG), blk),
            pl.BlockSpec((seqs, H_HEADS, H_DK, H_DV), st),
            pl.BlockSpec((n, 2 * n), lambda b, c: (0, 0)),
        ],
        out_specs=[
            pl.BlockSpec((seqs * chunks * rows, SEG), blk),
            pl.BlockSpec((seqs, H_HEADS, H_DK, H_DV), st),
        ],
        out_shape=[
            jax.ShapeDtypeStruct((total, SEG), BF16),
            jax.ShapeDtypeStruct((batch, H_HEADS, H_DK, H_DV), F32),
        ],
        compiler_params=pltpu.CompilerParams(
            dimension_semantics=("parallel", "arbitrary"), vmem_limit_bytes=VMEM_LIMIT),
        name="hgrn2",
    )(hq, f, hi, s0, tri)


def _merge_kernel(x_ref, oa_ref, ga_ref, oh_ref, gh_ref, sg_ref, hg_ref, w_ref, y_ref):
    def head_norm(o_ref, g_ref, h):
        o = o_ref[:, h * LANES:(h + 1) * LANES].astype(F32)
        return o * lax.rsqrt(jnp.mean(o * o, axis=-1, keepdims=True) + EPS) * g_ref[...]

    a = jnp.concatenate([head_norm(oa_ref, sg_ref, h) for h in range(A_HEADS)], axis=1)
    a = a * (1.0 - LAM_INIT) * ga_ref[...]
    r = jnp.concatenate([head_norm(oh_ref, hg_ref, h) for h in range(H_HEADS)], axis=1)
    r = r * gh_ref[...]
    mix = jnp.concatenate([a, r], axis=1).astype(BF16)
    y_ref[...] = x_ref[...] + jnp.dot(mix, w_ref[...], preferred_element_type=F32)


def _merge(x2d, o_attn, ga, o_hgrn, gh, subln_g, hgrn_norm_g, w_out_bf16, tm):
    rows, d_model = x2d.shape
    row_blk = lambda i: (i, 0)
    const = lambda i: (0, 0)
    return pl.pallas_call(
        _merge_kernel,
        grid=(rows // tm,),
        in_specs=[
            pl.BlockSpec((tm, d_model), row_blk),
            pl.BlockSpec((tm, SEG), row_blk),
            pl.BlockSpec((tm, SEG), row_blk),
            pl.BlockSpec((tm, SEG), row_blk),
            pl.BlockSpec((tm, SEG), row_blk),
            pl.BlockSpec((1, LANES), const),
            pl.BlockSpec((1, LANES), const),
            pl.BlockSpec(w_out_bf16.shape, const, pipeline_mode=pl.Buffered(1)),
        ],
        out_specs=pl.BlockSpec((tm, d_model), row_blk),
        out_shape=jax.ShapeDtypeStruct((rows, d_model), F32),
        compiler_params=pltpu.CompilerParams(dimension_semantics=("parallel",), vmem_limit_bytes=VMEM_LIMIT),
        name="merge",
    )(x2d, o_attn, ga, o_hgrn, gh, subln_g.reshape(1, LANES), hgrn_norm_g.reshape(1, LANES), w_out_bf16)


def kernel(x_prompt, x_sample, cache_k, cache_v, state_hgrn, page_table, norm_g, w_in, q_norm_g, k_norm_g,
           lambda_q1, lambda_k1, lambda_q2, lambda_k2, subln_g, hgrn_lb_logits, hgrn_norm_g, w_out):
    bp, tp, d_model = x_prompt.shape
    bs, ts, _ = x_sample.shape
    depth, n_pool, page_size = cache_k.shape[:3]
    assert depth == 1 and hgrn_lb_logits.shape[0] == 2
    assert (A_DV, H_DK, H_DV, 2 * A_DQK) == (LANES,) * 4
    past_len = page_table.shape[1] * page_size
    tm = 512
    attn_qblk, attn_kblk = 2048, 512
    assert tp % attn_qblk == 0 and tp % (4 * HGRN_ROWS) == 0 and (bp * tp) % tm == 0
    assert (bs * ts) % tm == 0 and tm % ts == 0 and bs % 8 == 0

    w_in16 = w_in[0].astype(BF16)
    w_out16 = w_out[0].astype(BF16)
    lam_params = jnp.stack([lambda_q1[0], lambda_k1[0], lambda_q2[0], lambda_k2[0]])
    proj = functools.partial(_inproj, norm_g=norm_g[0], w_in_bf16=w_in16, q_norm_g=q_norm_g[0],
                             k_norm_g=k_norm_g[0], lb_logits=hgrn_lb_logits)
    fin = functools.partial(_merge, subln_g=subln_g[0], hgrn_norm_g=hgrn_norm_g[0], w_out_bf16=w_out16, tm=tm)

    xp = x_prompt.reshape(bp * tp, d_model)
    q, k, v, ga, hq, f, hi, gh, k16, v16 = proj(xp, np.arange(tp), tp // tm, tm, True)
    o_attn = _prompt_attention(q, k16, v16, lam_params, bp, tp, attn_qblk, attn_kblk)
    o_hgrn, s_p = _hgrn(hq, f, hi, jnp.zeros((bp, H_HEADS, H_DK, H_DV), F32), bp, HGRN_ROWS, seqs=1, chunks=4)
    y_p = fin(xp, o_attn, ga, o_hgrn, gh)

    xs = x_sample.reshape(bs * ts, d_model)
    pos_s = past_len + np.tile(np.arange(ts), tm // ts)
    sq, sk, sv, sga, shq, sf, shi, sgh = proj(xs, pos_s, 1, tm, False)
    ck = cache_k.reshape(n_pool, page_size * A_HEADS, 2 * A_DQK)
    cv = cache_v.reshape(n_pool, page_size * A_HEADS, A_DV)
    so_attn = _sample_attention(sq, sk, sv, ck, cv, page_table, lam_params, group=page_table.shape[1])
    so_hgrn, s_s = _hgrn(shq, sf, shi, state_hgrn[0], bs, ts, seqs=8, chunks=1)
    y_s = fin(xs, so_attn, sga, so_hgrn, sgh)

    return (y_p.reshape(bp, tp, d_model), y_s.reshape(bs, ts, d_model),
            k.reshape(1, bp, tp, A_HEADS, 2 * A_DQK), v.reshape(1, bp, tp, A_HEADS, A_DV), s_p[None],
            sk.reshape(1, bs, ts, A_HEADS, 2 * A_DQK), sv.reshape(1, bs, ts, A_HEADS, A_DV), s_s[None])
```

```python
import functools
import math

import numpy as np
import jax
import jax.numpy as jnp
from jax import lax
from jax.experimental import pallas as pl
from jax.experimental.pallas import tpu as pltpu

F32 = jnp.float32
BF16 = jnp.bfloat16

LANES = 128
SUBLANES = 8
A_HEADS = 4
A_DQK = 64
A_DV = 128
H_HEADS = 4
H_DK = 128
H_DV = 128
SEG = 512
ROT_DIM = A_DQK // 4
ROPE_THETA = 500000.0
EPS = 1e-6
LAM_INIT = 0.8 - 0.6 * math.exp(-0.3 * 0)
Q_SCALE = A_DQK ** -0.5 * math.log2(math.e)
HGRN_ROWS = 128
SCORE_BOUND_LIMIT = 60.0
PAGE_AHEAD = 2
ATTN_GROUP = 256
KEY_UNROLL = 2
VMEM_LIMIT = 48 * 1024 * 1024


def _sigmoid(x):
    return 1.0 / (1.0 + jnp.exp(-x))


def _split_hi_lo(x):
    hi = x.astype(BF16)
    lo = (x - hi.astype(F32)).astype(BF16)
    return hi, lo


def _lam_from_ref(lam_ref):
    lp = lam_ref[...]
    s1 = jnp.sum(lp[0:1] * lp[1:2], axis=1, keepdims=True)
    s2 = jnp.sum(lp[2:3] * lp[3:4], axis=1, keepdims=True)
    return jnp.exp(s1) - jnp.exp(s2) + LAM_INIT


def _inproj_kernel(x_ref, ng_ref, w_ref, qg_ref, kg_ref, cos_ref, sa_ref, sb_ref, lbl_ref, gm_ref,
                   q_ref, k_ref, v_ref, ga_ref, hq_ref, f_ref, hi_ref, gh_ref, *maybe_bf16_refs):
    x = x_ref[...]
    ms = jnp.mean(x * x, axis=-1, keepdims=True)
    h = (x * lax.rsqrt(ms + EPS) * ng_ref[...]).astype(BF16)

    def proj(seg):
        return jnp.dot(h, w_ref[:, seg * SEG:(seg + 1) * SEG], preferred_element_type=F32)

    cos, sa, sb = cos_ref[...], sa_ref[...], sb_ref[...]

    def norm_rope(y, g_ref, hh):
        pair = y[:, (hh // 2) * 2 * LANES:(hh // 2 + 1) * 2 * LANES]
        gms = jnp.dot((pair * pair).astype(BF16), gm_ref[...], preferred_element_type=F32)
        own = slice((hh % 2) * LANES, (hh % 2 + 1) * LANES)
        yn = pair[:, own] * lax.rsqrt(gms[:, own] + EPS) * g_ref[...]
        return yn * cos + pltpu.roll(yn, LANES - ROT_DIM // 2, 1) * sa + pltpu.roll(yn, ROT_DIM // 2, 1) * sb

    tm = x.shape[0]
    aq = proj(0)
    for hh in range(A_HEADS):
        q_ref[:, hh * LANES:(hh + 1) * LANES] = norm_rope(aq, qg_ref, hh) * Q_SCALE
    ak = proj(1)
    for hh in range(A_HEADS):
        kh = norm_rope(ak, kg_ref, hh)
        k_ref[pl.ds(hh, tm, stride=A_HEADS), :] = kh
        if maybe_bf16_refs:
            maybe_bf16_refs[0][:, hh * LANES:(hh + 1) * LANES] = kh.astype(BF16)
    av = proj(2)
    for hh in range(A_HEADS):
        v_ref[pl.ds(hh, tm, stride=A_HEADS), :] = av[:, hh * LANES:(hh + 1) * LANES]
    if maybe_bf16_refs:
        maybe_bf16_refs[1][...] = av.astype(BF16)
    ag = proj(3)
    ga_ref[...] = (ag * _sigmoid(ag)).astype(ga_ref.dtype)
    hq = proj(4)
    hq_ref[...] = hq * _sigmoid(hq)
    hf = proj(5)
    lbl = lbl_ref[...]
    l0, l1 = lbl[0:1], lbl[1:2]
    mx = jnp.maximum(l0, l1)
    e0, e1 = jnp.exp(l0 - mx), jnp.exp(l1 - mx)
    lb = e0 / (e0 + e1)
    f_ref[...] = lb + (1.0 - lb) * _sigmoid(hf)
    hi_ref[...] = proj(6)
    hg = proj(7)
    gh_ref[...] = (hg * _sigmoid(hg)).astype(gh_ref.dtype)


def _rope_tables(positions):
    half = ROT_DIM // 2
    inv = np.float32(ROPE_THETA) ** (-np.arange(0, ROT_DIM, 2, dtype=np.float32) / np.float32(ROT_DIM))
    lane = np.arange(LANES) % A_DQK
    ang = positions.astype(np.float32)[:, None] * inv[lane % half][None, :].astype(np.float32)
    cos = np.where((lane < ROT_DIM)[None, :], np.cos(ang.astype(np.float64)), 1.0)
    sin = np.sin(ang.astype(np.float64))
    a = np.where((lane < half)[None, :], -sin, 0.0)
    b = np.where(((lane >= half) & (lane < ROT_DIM))[None, :], sin, 0.0)
    return tuple(jnp.asarray(t, F32) for t in (cos, a, b))


def _group_mean_matrix():
    g = np.zeros((2 * LANES, 2 * LANES), np.float32)
    for s in range(0, 2 * LANES, A_DQK):
        g[s:s + A_DQK, s:s + A_DQK] = 1.0 / A_DQK
    return jnp.asarray(g, BF16)


def _inproj(x2d, pos, n_pos_blocks, tm, with_bf16, norm_g, w_in_bf16, q_norm_g, k_norm_g, lb_logits):
    rows, d_model = x2d.shape
    d_in = w_in_bf16.shape[1]
    cos, sa, sb = _rope_tables(pos)
    qg = jnp.concatenate([q_norm_g, q_norm_g]).reshape(1, LANES)
    kg = jnp.concatenate([k_norm_g, k_norm_g]).reshape(1, LANES)
    const = lambda i: (0, 0)
    row_blk = lambda i: (i, 0)
    pos_blk = lambda i: (i % n_pos_blocks, 0)
    out_f32 = jax.ShapeDtypeStruct((rows, SEG), F32)
    out_shape = [out_f32] * 8
    out_specs = [pl.BlockSpec((tm, SEG), row_blk)] * 8
    for i in (3, 7):
        out_shape[i] = jax.ShapeDtypeStruct((rows, SEG), BF16)
    for i in (1, 2):
        out_shape[i] = jax.ShapeDtypeStruct((rows * A_HEADS, LANES), F32)
        out_specs[i] = pl.BlockSpec((tm * A_HEADS, LANES), row_blk)
    if with_bf16:
        out_shape += [jax.ShapeDtypeStruct((rows, SEG), BF16)] * 2
        out_specs += [pl.BlockSpec((tm, SEG), row_blk)] * 2
    return pl.pallas_call(
        _inproj_kernel,
        grid=(rows // tm,),
        in_specs=[
            pl.BlockSpec((tm, d_model), row_blk),
            pl.BlockSpec((1, d_model), const),
            pl.BlockSpec((d_model, d_in), const, pipeline_mode=pl.Buffered(1)),
            pl.BlockSpec((1, LANES), const),
            pl.BlockSpec((1, LANES), const),
            pl.BlockSpec((tm, LANES), pos_blk),
            pl.BlockSpec((tm, LANES), pos_blk),
            pl.BlockSpec((tm, LANES), pos_blk),
            pl.BlockSpec((2, SEG), const),
            pl.BlockSpec((2 * LANES, 2 * LANES), const),
        ],
        out_specs=out_specs,
        out_shape=out_shape,
        compiler_params=pltpu.CompilerParams(dimension_semantics=("parallel",), vmem_limit_bytes=VMEM_LIMIT),
        name="inproj",
    )(x2d, norm_g.reshape(1, d_model), w_in_bf16, qg, kg, cos, sa, sb, lb_logits, _group_mean_matrix())


def _stack_maps(q_bf16):
    lane = lax.broadcasted_iota(jnp.int32, q_bf16.shape, 1)
    zero = jnp.zeros_like(q_bf16)
    return jnp.concatenate([jnp.where(lane < A_DQK, q_bf16, zero), jnp.where(lane >= A_DQK, q_bf16, zero)], axis=0)


def _softmax_block_update(qs_parts, k, v1, plan, m_ref, l_ref, acc_ref):
    width = k.shape[0]
    n = qs_parts[0].shape[0]
    scores = [lax.dot_general(qs_parts[i], k, (((1,), (1,)), ((), ())), preferred_element_type=F32)
              for i, _ in plan]
    probs, alphas, m_news = [], [], []
    for (i, mask), s in zip(plan, scores):
        if mask is not None:
            s = jnp.where(mask, s, -jnp.inf)
        m_prev = m_ref[i * n:(i + 1) * n]
        m_new = jnp.maximum(m_prev, jnp.max(s, axis=1, keepdims=True))
        alphas.append(jnp.exp2(m_prev - m_new))
        probs.append(jnp.exp2(s - jnp.concatenate([m_new] * (width // LANES), axis=1)).astype(BF16))
        m_news.append(m_new)
    for (i, _), p, alpha, m_new in zip(plan, probs, alphas, m_news):
        rows = slice(i * n, (i + 1) * n)
        pv = jnp.dot(p, v1, preferred_element_type=F32)
        acc_ref[rows] = alpha * acc_ref[rows] + pv[:, :A_DV]
        l_ref[rows] = alpha * l_ref[rows] + pv[:, A_DV:]
        m_ref[rows] = m_new


def _combine_maps(l0, acc0, l1, acc1, lam):
    return acc0 * (1.0 / l0) - lam * (acc1 * (1.0 / l1))


def _map_sum_matrix():
    g = np.zeros((LANES, 2 * LANES), np.float32)
    g[:A_DQK, :LANES] = 1.0
    g[A_DQK:, LANES:] = 1.0
    return jnp.asarray(g, BF16)


def _map_rows_matrix():
    g = np.zeros((2 * SUBLANES, LANES), np.float32)
    g[:SUBLANES, :A_DQK] = 1.0
    g[SUBLANES:, A_DQK:] = 1.0
    return jnp.asarray(g, BF16)


def _transposed_block_update(qs_parts, k, vt, plan, ref_ref, sum_ref, acc_ref):
    n = qs_parts[0].shape[0]
    scores = [lax.dot_general(k, qs_parts[i], (((1,), (1,)), ((), ())), preferred_element_type=F32)
              for i, _ in plan]
    for (i, mask), st in zip(plan, scores):
        cols = slice(i * n, (i + 1) * n)
        if mask is not None:
            st = jnp.where(mask, st, -jnp.inf)
        p = jnp.exp2(st - ref_ref[0:1, cols])
        sum_ref[:, cols] = sum_ref[:, cols] + jnp.sum(p.reshape(p.shape[0] // SUBLANES, SUBLANES, n), axis=0)
        acc_ref[:, cols] = acc_ref[:, cols] + jnp.dot(vt, p.astype(BF16), preferred_element_type=F32)


def _diagonal_plans(qblk, kblk, part, keys_on_rows):
    per_map = qblk // part
    shape = (kblk, part) if keys_on_rows else (part, kblk)
    key = lax.broadcasted_iota(jnp.int32, shape, 0 if keys_on_rows else 1)
    qry = lax.broadcasted_iota(jnp.int32, shape, 1 if keys_on_rows else 0)
    plans = []
    for j in range(qblk // kblk):
        plan = []
        for c in range(2 * per_map):
            q0 = (c % per_map) * part
            if (j + 1) * kblk - 1 <= q0:
                plan.append((c, None))
            elif j * kblk <= q0 + part - 1:
                plan.append((c, key + j * kblk <= qry + q0))
        plans.append(plan)
    return plans


def _pattn_kernel(q_ref, k_ref, v_ref, lam_ref, ms_ref, mr_ref, o_ref,
                  v1_ref, vt_ref, kn_ref, m_ref, l_ref, acc_ref, mt_ref, lt_ref, acct_ref, *, qblk, kblk):
    qi = pl.program_id(2)
    seq = k_ref.shape[0]
    ratio = qblk // kblk

    @pl.when(qi == 0)
    def _():
        v1_ref[:, :A_DV] = v_ref[...]
        v1_ref[:, A_DV:] = jnp.ones((seq, LANES), BF16)
        for c in range(seq // kblk):
            vt_ref[:, c * kblk:(c + 1) * kblk] = jnp.transpose(v_ref[c * kblk:(c + 1) * kblk, :])
        k = k_ref[...]
        kn_ref[...] = jnp.max(jnp.dot(k * k, ms_ref[...], preferred_element_type=F32), axis=0, keepdims=True)

    q = q_ref[...].astype(BF16)
    qs = _stack_maps(q)
    part = ATTN_GROUP
    n_groups = 2 * qblk // part
    qs_parts = [qs[i * part:(i + 1) * part] for i in range(n_groups)]
    every_group = [(i, None) for i in range(n_groups)]

    qn = lax.dot_general(mr_ref[...], q * q, (((1,), (1,)), ((), ())), preferred_element_type=F32)
    kn = kn_ref[...]
    kn0 = jnp.concatenate([kn[:, :LANES]] * (qblk // LANES), axis=1)
    kn1 = jnp.concatenate([kn[:, LANES:]] * (qblk // LANES), axis=1)
    bound = jnp.concatenate([jnp.sqrt(qn[:SUBLANES] * kn0), jnp.sqrt(qn[SUBLANES:] * kn1)], axis=1) * 1.02
    mt_ref[...] = bound
    bounded = jnp.max(bound) <= SCORE_BOUND_LIMIT

    def run(block, diag_plans):
        def body(t, carry):
            for u in range(KEY_UNROLL):
                block(KEY_UNROLL * t + u, every_group)
            return carry

        lax.fori_loop(0, qi * (ratio // KEY_UNROLL), body, 0)
        for j, plan in enumerate(diag_plans):
            block(ratio * qi + j, plan)

    lam = _lam_from_ref(lam_ref)

    @pl.when(bounded)
    def _():
        acct_ref[...] = jnp.zeros(acct_ref.shape, F32)
        lt_ref[...] = jnp.zeros(lt_ref.shape, F32)

        def block(kb, plan):
            start = pl.multiple_of(kb * kblk, kblk)
            _transposed_block_update(qs_parts, k_ref[pl.ds(start, kblk), :], vt_ref[:, pl.ds(start, kblk)],
                                     plan, mt_ref, lt_ref, acct_ref)

        run(block, _diagonal_plans(qblk, kblk, part, True))
        acc = acct_ref[...]
        l = jnp.sum(lt_ref[...], axis=0, keepdims=True)
        out_t = _combine_maps(l[:, :qblk], acc[:, :qblk], l[:, qblk:], acc[:, qblk:], lam)
        o_ref[...] = jnp.transpose(out_t).astype(o_ref.dtype)

    @pl.when(jnp.logical_not(bounded))
    def _():
        m_ref[...] = jnp.full(m_ref.shape, -jnp.inf, F32)
        l_ref[...] = jnp.zeros(l_ref.shape, F32)
        acc_ref[...] = jnp.zeros(acc_ref.shape, F32)

        def block(kb, plan):
            start = pl.multiple_of(kb * kblk, kblk)
            _softmax_block_update(qs_parts, k_ref[pl.ds(start, kblk), :], v1_ref[pl.ds(start, kblk), :],
                                  plan, m_ref, l_ref, acc_ref)

        run(block, _diagonal_plans(qblk, kblk, part, False))
        o_ref[...] = _combine_maps(l_ref[:qblk], acc_ref[:qblk], l_ref[qblk:], acc_ref[qblk:], lam).astype(o_ref.dtype)


def _prompt_attention(q, kb16, vb16, lam_params, batch, seq, qblk, kblk):
    nq = seq // qblk
    const = lambda b, h, i: (0, 0)
    return pl.pallas_call(
        functools.partial(_pattn_kernel, qblk=qblk, kblk=kblk),
        grid=(batch, A_HEADS, nq),
        in_specs=[
            pl.BlockSpec((qblk, LANES), lambda b, h, i: (b * nq + i, h)),
            pl.BlockSpec((seq, LANES), lambda b, h, i: (b, h)),
            pl.BlockSpec((seq, LANES), lambda b, h, i: (b, h)),
            pl.BlockSpec((4, A_DQK), const),
            pl.BlockSpec((LANES, 2 * LANES), const),
            pl.BlockSpec((2 * SUBLANES, LANES), const),
        ],
        out_specs=pl.BlockSpec((qblk, LANES), lambda b, h, i: (b * nq + i, h)),
        out_shape=jax.ShapeDtypeStruct((batch * seq, A_HEADS * A_DV), BF16),
        scratch_shapes=[
            pltpu.VMEM((seq, 2 * LANES), BF16),
            pltpu.VMEM((A_DV, seq), BF16),
            pltpu.VMEM((1, 2 * LANES), F32),
            pltpu.VMEM((2 * qblk, LANES), F32),
            pltpu.VMEM((2 * qblk, LANES), F32),
            pltpu.VMEM((2 * qblk, A_DV), F32),
            pltpu.VMEM((SUBLANES, 2 * qblk), F32),
            pltpu.VMEM((SUBLANES, 2 * qblk), F32),
            pltpu.VMEM((A_DV, 2 * qblk), F32),
        ],
        compiler_params=pltpu.CompilerParams(
            dimension_semantics=("parallel", "parallel", "arbitrary"), vmem_limit_bytes=VMEM_LIMIT),
        name="prompt_attn",
    )(q, kb16, vb16, lam_params, _map_sum_matrix(), _map_rows_matrix())


def _partial_softmax(qs, k, v, mask):
    s = lax.dot_general(qs, k, (((1,), (1,)), ((), ())), preferred_element_type=F32)
    s = jnp.where(mask, s, -jnp.inf)
    m = jnp.max(s, axis=1, keepdims=True)
    p = jnp.exp2(s - m)
    return m, jnp.sum(p, axis=1, keepdims=True), jnp.dot(p.astype(BF16), v, preferred_element_type=F32)


def _sattn_kernel(pt_ref, q_ref, kn_ref, vn_ref, lam_ref, ck_hbm, cv_hbm, o_ref, kbuf, vbuf, sems,
                  *, n_pages, group, n_new):
    b = pl.program_id(0)
    n_slots = kbuf.shape[0]

    def page_copies(elem, slot):
        copies = []
        for i in range(n_pages):
            page = pt_ref[elem * n_pages + i]
            copies.append(pltpu.make_async_copy(ck_hbm.at[page], kbuf.at[slot, i], sems.at[0, slot]))
            copies.append(pltpu.make_async_copy(cv_hbm.at[page], vbuf.at[slot, i], sems.at[1, slot]))
        return copies

    @pl.when(b == 0)
    def _():
        for first in range(n_slots - 1):
            for cp in page_copies(first, first):
                cp.start()

    @pl.when(b + n_slots - 1 < pl.num_programs(0))
    def _():
        for cp in page_copies(b + n_slots - 1, (b + n_slots - 1) % n_slots):
            cp.start()

    q = q_ref[...].astype(BF16)
    qs = jnp.concatenate([_stack_maps(q[:, h * LANES:(h + 1) * LANES]) for h in range(A_HEADS)], axis=0)
    n_rows = 2 * n_new * A_HEADS

    def same_head(width):
        row = lax.broadcasted_iota(jnp.int32, (n_rows, width), 0)
        col = lax.broadcasted_iota(jnp.int32, (n_rows, width), 1)
        return row, col, (col % A_HEADS) == (row // (2 * n_new))

    n_kv = n_new * A_HEADS
    pad = jnp.zeros((LANES - n_kv, LANES), F32)
    row, col, ok = same_head(LANES)
    parts = [_partial_softmax(qs, jnp.concatenate([kn_ref[...], pad], axis=0).astype(BF16),
                              jnp.concatenate([vn_ref[...], pad], axis=0).astype(BF16),
                              ok & ((col // A_HEADS) <= (row % n_new)))]

    slot = b % n_slots
    for cp in page_copies(b, slot):
        cp.wait()
    rows_per_page = kbuf.shape[2]
    page_mask = same_head(group * rows_per_page)[2]
    for g in range(0, n_pages, group):
        k = kbuf[slot, g:g + group].reshape(group * rows_per_page, LANES).astype(BF16)
        v = vbuf[slot, g:g + group].reshape(group * rows_per_page, LANES).astype(BF16)
        parts.append(_partial_softmax(qs, k, v, page_mask))
    m = functools.reduce(jnp.maximum, [p[0] for p in parts])
    weights = [jnp.exp2(p[0] - m) for p in parts]
    l = sum(w * p[1] for w, p in zip(weights, parts))
    acc = sum(w * p[2] for w, p in zip(weights, parts))
    lam = _lam_from_ref(lam_ref)
    for h in range(A_HEADS):
        r0 = h * 2 * n_new
        r1 = r0 + n_new
        o_ref[:, h * LANES:(h + 1) * LANES] = _combine_maps(
            l[r0:r1], acc[r0:r1], l[r1:r1 + n_new], acc[r1:r1 + n_new], lam)


def _sample_attention(q, k_new, v_new, cache_k, cache_v, page_table, lam_params, group):
    bs, n_pages = page_table.shape
    n_new = q.shape[0] // bs
    rows_per_page = cache_k.shape[1]
    row_blk = lambda b, pt: (b, 0)
    grid_spec = pltpu.PrefetchScalarGridSpec(
        num_scalar_prefetch=1,
        grid=(bs,),
        in_specs=[
            pl.BlockSpec((n_new, q.shape[1]), row_blk),
            pl.BlockSpec((n_new * A_HEADS, LANES), row_blk),
            pl.BlockSpec((n_new * A_HEADS, LANES), row_blk),
            pl.BlockSpec((4, A_DQK), lambda b, pt: (0, 0)),
            pl.BlockSpec(memory_space=pl.ANY),
            pl.BlockSpec(memory_space=pl.ANY),
        ],
        out_specs=pl.BlockSpec((n_new, q.shape[1]), row_blk),
        scratch_shapes=[
            pltpu.VMEM((PAGE_AHEAD + 1, n_pages, rows_per_page, LANES), F32),
            pltpu.VMEM((PAGE_AHEAD + 1, n_pages, rows_per_page, LANES), F32),
            pltpu.SemaphoreType.DMA((2, PAGE_AHEAD + 1)),
        ],
    )
    return pl.pallas_call(
        functools.partial(_sattn_kernel, n_pages=n_pages, group=group, n_new=n_new),
        grid_spec=grid_spec,
        out_shape=jax.ShapeDtypeStruct(q.shape, F32),
        compiler_params=pltpu.CompilerParams(dimension_semantics=("arbitrary",), vmem_limit_bytes=VMEM_LIMIT),
        name="sample_attn",
    )(page_table.reshape(-1), q, k_new, v_new, lam_params, cache_k, cache_v)


def _hgrn_kernel(q_ref, f_ref, v_ref, s0_ref, tri_ref, o_ref, s_ref, *, rows, seqs, chunks):
    @pl.when(pl.program_id(1) == 0)
    def _():
        s_ref[...] = s0_ref[...]

    n = HGRN_ROWS
    pad = n - rows
    r_idx = lax.broadcasted_iota(jnp.int32, (n, n), 0)
    c_idx = lax.broadcasted_iota(jnp.int32, (n, n), 1)
    causal = c_idx <= r_idx
    tri = tri_ref[...]
    units = [(e, h, c) for e in range(seqs) for h in range(H_HEADS) for c in range(chunks)]

    def rows_of(e, c):
        return slice((e * chunks + c) * rows, (e * chunks + c + 1) * rows)

    def lanes_of(h):
        return slice(h * LANES, (h + 1) * LANES)

    def padded(x, fill):
        return jnp.concatenate([x, jnp.full((pad, LANES), fill, F32)], axis=0) if pad else x

    logs = {}
    for (e, h, c) in units:
        hi, lo = _split_hi_lo(jnp.log2(padded(f_ref[rows_of(e, c), lanes_of(h)], 1.0)))
        logs[e, h, c] = jnp.concatenate([hi, lo], axis=0)
    cums = {u: jnp.dot(tri, logs[u], preferred_element_type=F32) for u in units}

    q_end, k_end, vals, decay, att = {}, {}, {}, {}, {}
    for u in units:
        e, h, c = u
        b = cums[u]
        b_end = b[n - 1:n]
        b_mid = b[n // 2 - 1:n // 2]
        q = padded(q_ref[rows_of(e, c), lanes_of(h)], 0.0)
        k = 1.0 - padded(f_ref[rows_of(e, c), lanes_of(h)], 1.0)
        vals[u] = padded(v_ref[rows_of(e, c), lanes_of(h)], 0.0).astype(BF16)
        q_end[u] = (q * jnp.exp2(b)).astype(BF16)
        k_end[u] = (k * jnp.exp2(b_end - b)).astype(BF16)
        decay[u] = jnp.transpose(jnp.broadcast_to(jnp.exp2(b_end), (8, LANES)))[:, 0:1]
        att[u] = lax.dot_general((q * jnp.exp2(b - b_mid)).astype(BF16), (k * jnp.exp2(b_mid - b)).astype(BF16),
                                 (((1,), (1,)), ((), ())), preferred_element_type=F32)
    kv = {u: lax.dot_general(k_end[u], vals[u], (((0,), (0,)), ((), ())), preferred_element_type=F32)
          for u in units}

    incoming = {}
    for e in range(seqs):
        for h in range(H_HEADS):
            state = s_ref[e, h]
            for c in range(chunks):
                incoming[e, h, c] = state.astype(BF16)
                state = decay[e, h, c] * state + kv[e, h, c]
            s_ref[e, h] = state
    for u in units:
        e, h, c = u
        lhs = jnp.concatenate([jnp.where(causal, att[u], 0.0).astype(BF16), q_end[u]], axis=1)
        o = jnp.dot(lhs, jnp.concatenate([vals[u], incoming[u]], axis=0), preferred_element_type=F32)
        o_ref[rows_of(e, c), lanes_of(h)] = o[:rows].astype(o_ref.dtype)


def _hgrn(hq, f, hi, s0, batch, rows, seqs, chunks):
    total = hq.shape[0]
    steps = total // (batch * rows * chunks)
    n = HGRN_ROWS
    tri = np.tril(np.ones((n, n), np.float32))
    tri = jnp.asarray(np.concatenate([tri, tri], axis=1), BF16)
    assert seqs == 1 or steps == 1
    blk = lambda b, c: (b * steps + c, 0)
    st = lambda b, c: (b, 0, 0, 0)
    return pl.pallas_call(
        functools.partial(_hgrn_kernel, rows=rows, seqs=seqs, chunks=chunks),
        grid=(batch // seqs, steps),
        in_specs=[
            pl.BlockSpec((seqs * chunks * rows, SEG), blk),
            pl.BlockSpec((seqs * chunks * rows, SEG), blk),
            pl.BlockSpec((seqs * chunks * rows, SEG), blk),
            pl.BlockSpec((seqs, H_HEADS, H_DK, H_DV), st),
            pl.BlockSpec((n, 2 * n), lambda b, c: (0, 0)),
        ],
        out_specs=[
            pl.BlockSpec((seqs * chunks * rows, SEG), blk),
            pl.BlockSpec((seqs, H_HEADS, H_DK, H_DV), st),
        ],
        out_shape=[
            jax.ShapeDtypeStruct((total, SEG), BF16),
            jax.ShapeDtypeStruct((batch, H_HEADS, H_DK, H_DV), F32),
        ],
        compiler_params=pltpu.CompilerParams(
            dimension_semantics=("parallel", "arbitrary"), vmem_limit_bytes=VMEM_LIMIT),
        name="hgrn2",
    )(hq, f, hi, s0, tri)


def _merge_kernel(x_ref, oa_ref, ga_ref, oh_ref, gh_ref, sg_ref, hg_ref, w_ref, y_ref):
    def head_norm(o_ref, g_ref, h):
        o = o_ref[:, h * LANES:(h + 1) * LANES].astype(F32)
        return o * lax.rsqrt(jnp.mean(o * o, axis=-1, keepdims=True) + EPS) * g_ref[...]

    a = jnp.concatenate([head_norm(oa_ref, sg_ref, h) for h in range(A_HEADS)], axis=1)
    a = a * (1.0 - LAM_INIT) * ga_ref[...]
    r = jnp.concatenate([head_norm(oh_ref, hg_ref, h) for h in range(H_HEADS)], axis=1)
    r = r * gh_ref[...]
    mix = jnp.concatenate([a, r], axis=1).astype(BF16)
    y_ref[...] = x_ref[...] + jnp.dot(mix, w_ref[...], preferred_element_type=F32)


def _merge(x2d, o_attn, ga, o_hgrn, gh, subln_g, hgrn_norm_g, w_out_bf16, tm):
    rows, d_model = x2d.shape
    row_blk = lambda i: (i, 0)
    const = lambda i: (0, 0)
    return pl.pallas_call(
        _merge_kernel,
        grid=(rows // tm,),
        in_specs=[
            pl.BlockSpec((tm, d_model), row_blk),
            pl.BlockSpec((tm, SEG), row_blk),
            pl.BlockSpec((tm, SEG), row_blk),
            pl.BlockSpec((tm, SEG), row_blk),
            pl.BlockSpec((tm, SEG), row_blk),
            pl.BlockSpec((1, LANES), const),
            pl.BlockSpec((1, LANES), const),
            pl.BlockSpec(w_out_bf16.shape, const, pipeline_mode=pl.Buffered(1)),
        ],
        out_specs=pl.BlockSpec((tm, d_model), row_blk),
        out_shape=jax.ShapeDtypeStruct((rows, d_model), F32),
        compiler_params=pltpu.CompilerParams(dimension_semantics=("parallel",), vmem_limit_bytes=VMEM_LIMIT),
        name="merge",
    )(x2d, o_attn, ga, o_hgrn, gh, subln_g.reshape(1, LANES), hgrn_norm_g.reshape(1, LANES), w_out_bf16)


def kernel(x_prompt, x_sample, cache_k, cache_v, state_hgrn, page_table, norm_g, w_in, q_norm_g, k_norm_g,
           lambda_q1, lambda_k1, lambda_q2, lambda_k2, subln_g, hgrn_lb_logits, hgrn_norm_g, w_out):
    bp, tp, d_model = x_prompt.shape
    bs, ts, _ = x_sample.shape
    depth, n_pool, page_size = cache_k.shape[:3]
    assert depth == 1 and hgrn_lb_logits.shape[0] == 2
    assert (A_DV, H_DK, H_DV, 2 * A_DQK) == (LANES,) * 4
    past_len = page_table.shape[1] * page_size
    tm = 512
    attn_qblk, attn_kblk = 2048, 512
    assert tp % attn_qblk == 0 and tp % (8 * HGRN_ROWS) == 0 and (bp * tp) % tm == 0
    assert (bs * ts) % tm == 0 and tm % ts == 0 and bs % 8 == 0

    w_in16 = w_in[0].astype(BF16)
    w_out16 = w_out[0].astype(BF16)
    lam_params = jnp.stack([lambda_q1[0], lambda_k1[0], lambda_q2[0], lambda_k2[0]])
    proj = functools.partial(_inproj, norm_g=norm_g[0], w_in_bf16=w_in16, q_norm_g=q_norm_g[0],
                             k_norm_g=k_norm_g[0], lb_logits=hgrn_lb_logits)
    fin = functools.partial(_merge, subln_g=subln_g[0], hgrn_norm_g=hgrn_norm_g[0], w_out_bf16=w_out16, tm=tm)

    xp = x_prompt.reshape(bp * tp, d_model)
    q, k, v, ga, hq, f, hi, gh, k16, v16 = proj(xp, np.arange(tp), tp // tm, tm, True)
    o_attn = _prompt_attention(q, k16, v16, lam_params, bp, tp, attn_qblk, attn_kblk)
    o_hgrn, s_p = _hgrn(hq, f, hi, jnp.zeros((bp, H_HEADS, H_DK, H_DV), F32), bp, HGRN_ROWS, seqs=1, chunks=8)
    y_p = fin(xp, o_attn, ga, o_hgrn, gh)

    xs = x_sample.reshape(bs * ts, d_model)
    pos_s = past_len + np.tile(np.arange(ts), tm // ts)
    sq, sk, sv, sga, shq, sf, shi, sgh = proj(xs, pos_s, 1, tm, False)
    ck = cache_k.reshape(n_pool, page_size * A_HEADS, 2 * A_DQK)
    cv = cache_v.reshape(n_pool, page_size * A_HEADS, A_DV)
    so_attn = _sample_attention(sq, sk, sv, ck, cv, page_table, lam_params, group=page_table.shape[1])
    so_hgrn, s_s = _hgrn(shq, sf, shi, state_hgrn[0], bs, ts, seqs=8, chunks=1)
    y_s = fin(xs, so_attn, sga, so_hgrn, sgh)

    return (y_p.reshape(bp, tp, d_model), y_s.reshape(bs, ts, d_model),
            k.reshape(1, bp, tp, A_HEADS, 2 * A_DQK), v.reshape(1, bp, tp, A_HEADS, A_DV), s_p[None],
            sk.reshape(1, bs, ts, A_HEADS, 2 * A_DQK), sv.reshape(1, bs, ts, A_HEADS, A_DV), s_s[None])
```

```python
import functools
import math

import numpy as np
import jax
import jax.numpy as jnp
from jax import lax
from jax.experimental import pallas as pl
from jax.experimental.pallas import tpu as pltpu

F32 = jnp.float32
BF16 = jnp.bfloat16

LANES = 128
SUBLANES = 8
A_HEADS = 4
A_DQK = 64
A_DV = 128
H_HEADS = 4
H_DK = 128
H_DV = 128
SEG = 512
ROT_DIM = A_DQK // 4
ROPE_THETA = 500000.0
EPS = 1e-6
LAM_INIT = 0.8 - 0.6 * math.exp(-0.3 * 0)
Q_SCALE = A_DQK ** -0.5 * math.log2(math.e)
HGRN_ROWS = 128
SCORE_BOUND_LIMIT = 60.0
PAGE_AHEAD = 2
ATTN_GROUP = 256
KEY_UNROLL = 2
VMEM_LIMIT = 48 * 1024 * 1024


def _sigmoid(x):
    return 1.0 / (1.0 + jnp.exp(-x))


def _split_hi_lo(x):
    hi = x.astype(BF16)
    lo = (x - hi.astype(F32)).astype(BF16)
    return hi, lo


def _lam_from_ref(lam_ref):
    lp = lam_ref[...]
    s1 = jnp.sum(lp[0:1] * lp[1:2], axis=1, keepdims=True)
    s2 = jnp.sum(lp[2:3] * lp[3:4], axis=1, keepdims=True)
    return jnp.exp(s1) - jnp.exp(s2) + LAM_INIT


def _inproj_kernel(x_ref, ng_ref, w_ref, qg_ref, kg_ref, cos_ref, sa_ref, sb_ref, lbl_ref, gm_ref,
                   q_ref, k_ref, v_ref, ga_ref, hq_ref, f_ref, hi_ref, gh_ref, *maybe_bf16_refs):
    x = x_ref[...]
    ms = jnp.mean(x * x, axis=-1, keepdims=True)
    h = (x * lax.rsqrt(ms + EPS) * ng_ref[...]).astype(BF16)

    def proj(seg):
        return jnp.dot(h, w_ref[:, seg * SEG:(seg + 1) * SEG], preferred_element_type=F32)

    cos, sa, sb = cos_ref[...], sa_ref[...], sb_ref[...]

    def norm_rope(y, g_ref, hh):
        pair = y[:, (hh // 2) * 2 * LANES:(hh // 2 + 1) * 2 * LANES]
        gms = jnp.dot((pair * pair).astype(BF16), gm_ref[...], preferred_element_type=F32)
        own = slice((hh % 2) * LANES, (hh % 2 + 1) * LANES)
        yn = pair[:, own] * lax.rsqrt(gms[:, own] + EPS) * g_ref[...]
        return yn * cos + pltpu.roll(yn, LANES - ROT_DIM // 2, 1) * sa + pltpu.roll(yn, ROT_DIM // 2, 1) * sb

    tm = x.shape[0]
    aq = proj(0)
    for hh in range(A_HEADS):
        q_ref[:, hh * LANES:(hh + 1) * LANES] = norm_rope(aq, qg_ref, hh) * Q_SCALE
    ak = proj(1)
    for hh in range(A_HEADS):
        kh = norm_rope(ak, kg_ref, hh)
        k_ref[pl.ds(hh, tm, stride=A_HEADS), :] = kh
        if maybe_bf16_refs:
            maybe_bf16_refs[0][:, hh * LANES:(hh + 1) * LANES] = kh.astype(BF16)
    av = proj(2)
    for hh in range(A_HEADS):
        v_ref[pl.ds(hh, tm, stride=A_HEADS), :] = av[:, hh * LANES:(hh + 1) * LANES]
    if maybe_bf16_refs:
        maybe_bf16_refs[1][...] = av.astype(BF16)
    ag = proj(3)
    ga_ref[...] = (ag * _sigmoid(ag)).astype(ga_ref.dtype)
    hq = proj(4)
    hq_ref[...] = hq * _sigmoid(hq)
    hf = proj(5)
    lbl = lbl_ref[...]
    l0, l1 = lbl[0:1], lbl[1:2]
    mx = jnp.maximum(l0, l1)
    e0, e1 = jnp.exp(l0 - mx), jnp.exp(l1 - mx)
    lb = e0 / (e0 + e1)
    f_ref[...] = lb + (1.0 - lb) * _sigmoid(hf)
    hi_ref[...] = proj(6)
    hg = proj(7)
    gh_ref[...] = (hg * _sigmoid(hg)).astype(gh_ref.dtype)


def _rope_tables(positions):
    half = ROT_DIM // 2
    inv = np.float32(ROPE_THETA) ** (-np.arange(0, ROT_DIM, 2, dtype=np.float32) / np.float32(ROT_DIM))
    lane = np.arange(LANES) % A_DQK
    ang = positions.astype(np.float32)[:, None] * inv[lane % half][None, :].astype(np.float32)
    cos = np.where((lane < ROT_DIM)[None, :], np.cos(ang.astype(np.float64)), 1.0)
    sin = np.sin(ang.astype(np.float64))
    a = np.where((lane < half)[None, :], -sin, 0.0)
    b = np.where(((lane >= half) & (lane < ROT_DIM))[None, :], sin, 0.0)
    return tuple(jnp.asarray(t, F32) for t in (cos, a, b))


def _group_mean_matrix():
    g = np.zeros((2 * LANES, 2 * LANES), np.float32)
    for s in range(0, 2 * LANES, A_DQK):
        g[s:s + A_DQK, s:s + A_DQK] = 1.0 / A_DQK
    return jnp.asarray(g, BF16)


def _inproj(x2d, pos, n_pos_blocks, tm, with_bf16, norm_g, w_in_bf16, q_norm_g, k_norm_g, lb_logits):
    rows, d_model = x2d.shape
    d_in = w_in_bf16.shape[1]
    cos, sa, sb = _rope_tables(pos)
    qg = jnp.concatenate([q_norm_g, q_norm_g]).reshape(1, LANES)
    kg = jnp.concatenate([k_norm_g, k_norm_g]).reshape(1, LANES)
    const = lambda i: (0, 0)
    row_blk = lambda i: (i, 0)
    pos_blk = lambda i: (i % n_pos_blocks, 0)
    out_f32 = jax.ShapeDtypeStruct((rows, SEG), F32)
    out_shape = [out_f32] * 8
    out_specs = [pl.BlockSpec((tm, SEG), row_blk)] * 8
    for i in (3, 7):
        out_shape[i] = jax.ShapeDtypeStruct((rows, SEG), BF16)
    for i in (1, 2):
        out_shape[i] = jax.ShapeDtypeStruct((rows * A_HEADS, LANES), F32)
        out_specs[i] = pl.BlockSpec((tm * A_HEADS, LANES), row_blk)
    if with_bf16:
        out_shape += [jax.ShapeDtypeStruct((rows, SEG), BF16)] * 2
        out_specs += [pl.BlockSpec((tm, SEG), row_blk)] * 2
    return pl.pallas_call(
        _inproj_kernel,
        grid=(rows // tm,),
        in_specs=[
            pl.BlockSpec((tm, d_model), row_blk),
            pl.BlockSpec((1, d_model), const),
            pl.BlockSpec((d_model, d_in), const, pipeline_mode=pl.Buffered(1)),
            pl.BlockSpec((1, LANES), const),
            pl.BlockSpec((1, LANES), const),
            pl.BlockSpec((tm, LANES), pos_blk),
            pl.BlockSpec((tm, LANES), pos_blk),
            pl.BlockSpec((tm, LANES), pos_blk),
            pl.BlockSpec((2, SEG), const),
            pl.BlockSpec((2 * LANES, 2 * LANES), const),
        ],
        out_specs=out_specs,
        out_shape=out_shape,
        compiler_params=pltpu.CompilerParams(dimension_semantics=("parallel",), vmem_limit_bytes=VMEM_LIMIT),
        name="inproj",
    )(x2d, norm_g.reshape(1, d_model), w_in_bf16, qg, kg, cos, sa, sb, lb_logits, _group_mean_matrix())


def _stack_maps(q_bf16):
    lane = lax.broadcasted_iota(jnp.int32, q_bf16.shape, 1)
    zero = jnp.zeros_like(q_bf16)
    return jnp.concatenate([jnp.where(lane < A_DQK, q_bf16, zero), jnp.where(lane >= A_DQK, q_bf16, zero)], axis=0)


def _softmax_block_update(qs_parts, k, v1, plan, m_ref, l_ref, acc_ref):
    width = k.shape[0]
    n = qs_parts[0].shape[0]
    scores = [lax.dot_general(qs_parts[i], k, (((1,), (1,)), ((), ())), preferred_element_type=F32)
              for i, _ in plan]
    probs, alphas, m_news = [], [], []
    for (i, mask), s in zip(plan, scores):
        if mask is not None:
            s = jnp.where(mask, s, -jnp.inf)
        m_prev = m_ref[i * n:(i + 1) * n]
        m_new = jnp.maximum(m_prev, jnp.max(s, axis=1, keepdims=True))
        alphas.append(jnp.exp2(m_prev - m_new))
        probs.append(jnp.exp2(s - jnp.concatenate([m_new] * (width // LANES), axis=1)).astype(BF16))
        m_news.append(m_new)
    for (i, _), p, alpha, m_new in zip(plan, probs, alphas, m_news):
        rows = slice(i * n, (i + 1) * n)
        pv = jnp.dot(p, v1, preferred_element_type=F32)
        acc_ref[rows] = alpha * acc_ref[rows] + pv[:, :A_DV]
        l_ref[rows] = alpha * l_ref[rows] + pv[:, A_DV:]
        m_ref[rows] = m_new


def _combine_maps(l0, acc0, l1, acc1, lam):
    return acc0 * (1.0 / l0) - lam * (acc1 * (1.0 / l1))


def _map_sum_matrix():
    g = np.zeros((LANES, 2 * LANES), np.float32)
    g[:A_DQK, :LANES] = 1.0
    g[A_DQK:, LANES:] = 1.0
    return jnp.asarray(g, BF16)


def _map_rows_matrix():
    g = np.zeros((2 * SUBLANES, LANES), np.float32)
    g[:SUBLANES, :A_DQK] = 1.0
    g[SUBLANES:, A_DQK:] = 1.0
    return jnp.asarray(g, BF16)


def _transposed_block_update(qs_parts, k, vt, plan, ref_ref, sum_ref, acc_ref):
    n = qs_parts[0].shape[0]
    scores = [lax.dot_general(k, qs_parts[i], (((1,), (1,)), ((), ())), preferred_element_type=F32)
              for i, _ in plan]
    for (i, mask), st in zip(plan, scores):
        cols = slice(i * n, (i + 1) * n)
        if mask is not None:
            st = jnp.where(mask, st, -jnp.inf)
        p = jnp.exp2(st - ref_ref[0:1, cols])
        sum_ref[:, cols] = sum_ref[:, cols] + jnp.sum(p.reshape(p.shape[0] // SUBLANES, SUBLANES, n), axis=0)
        acc_ref[:, cols] = acc_ref[:, cols] + jnp.dot(vt, p.astype(BF16), preferred_element_type=F32)


def _diagonal_plans(qblk, kblk, part, keys_on_rows):
    per_map = qblk // part
    shape = (kblk, part) if keys_on_rows else (part, kblk)
    key = lax.broadcasted_iota(jnp.int32, shape, 0 if keys_on_rows else 1)
    qry = lax.broadcasted_iota(jnp.int32, shape, 1 if keys_on_rows else 0)
    plans = []
    for j in range(qblk // kblk):
        plan = []
        for c in range(2 * per_map):
            q0 = (c % per_map) * part
            if (j + 1) * kblk - 1 <= q0:
                plan.append((c, None))
            elif j * kblk <= q0 + part - 1:
                plan.append((c, key + j * kblk <= qry + q0))
        plans.append(plan)
    return plans


def _pattn_kernel(q_ref, k_ref, v_ref, lam_ref, ms_ref, mr_ref, o_ref,
                  v1_ref, vt_ref, kn_ref, m_ref, l_ref, acc_ref, mt_ref, lt_ref, acct_ref, *, qblk, kblk):
    qi = pl.program_id(2)
    seq = k_ref.shape[0]
    ratio = qblk // kblk

    @pl.when(qi == 0)
    def _():
        v1_ref[:, :A_DV] = v_ref[...]
        v1_ref[:, A_DV:] = jnp.ones((seq, LANES), BF16)
        for c in range(seq // kblk):
            vt_ref[:, c * kblk:(c + 1) * kblk] = jnp.transpose(v_ref[c * kblk:(c + 1) * kblk, :])
        k = k_ref[...]
        kn_ref[...] = jnp.max(jnp.dot(k * k, ms_ref[...], preferred_element_type=F32), axis=0, keepdims=True)

    q = q_ref[...].astype(BF16)
    qs = _stack_maps(q)
    part = ATTN_GROUP
    n_groups = 2 * qblk // part
    qs_parts = [qs[i * part:(i + 1) * part] for i in range(n_groups)]
    every_group = [(i, None) for i in range(n_groups)]

    qn = lax.dot_general(mr_ref[...], q * q, (((1,), (1,)), ((), ())), preferred_element_type=F32)
    kn = kn_ref[...]
    kn0 = jnp.concatenate([kn[:, :LANES]] * (qblk // LANES), axis=1)
    kn1 = jnp.concatenate([kn[:, LANES:]] * (qblk // LANES), axis=1)
    bound = jnp.concatenate([jnp.sqrt(qn[:SUBLANES] * kn0), jnp.sqrt(qn[SUBLANES:] * kn1)], axis=1) * 1.02
    mt_ref[...] = bound
    bounded = jnp.max(bound) <= SCORE_BOUND_LIMIT

    def run(block, diag_plans):
        def body(t, carry):
            for u in range(KEY_UNROLL):
                block(KEY_UNROLL * t + u, every_group)
            return carry

        lax.fori_loop(0, qi * (ratio // KEY_UNROLL), body, 0)
        for j, plan in enumerate(diag_plans):
            block(ratio * qi + j, plan)

    lam = _lam_from_ref(lam_ref)

    @pl.when(bounded)
    def _():
        acct_ref[...] = jnp.zeros(acct_ref.shape, F32)
        lt_ref[...] = jnp.zeros(lt_ref.shape, F32)

        def block(kb, plan):
            start = pl.multiple_of(kb * kblk, kblk)
            _transposed_block_update(qs_parts, k_ref[pl.ds(start, kblk), :], vt_ref[:, pl.ds(start, kblk)],
                                     plan, mt_ref, lt_ref, acct_ref)

        run(block, _diagonal_plans(qblk, kblk, part, True))
        acc = acct_ref[...]
        l = jnp.sum(lt_ref[...], axis=0, keepdims=True)
        out_t = _combine_maps(l[:, :qblk], acc[:, :qblk], l[:, qblk:], acc[:, qblk:], lam)
        o_ref[...] = jnp.transpose(out_t).astype(o_ref.dtype)

    @pl.when(jnp.logical_not(bounded))
    def _():
        m_ref[...] = jnp.full(m_ref.shape, -jnp.inf, F32)
        l_ref[...] = jnp.zeros(l_ref.shape, F32)
        acc_ref[...] = jnp.zeros(acc_ref.shape, F32)

        def block(kb, plan):
            start = pl.multiple_of(kb * kblk, kblk)
            _softmax_block_update(qs_parts, k_ref[pl.ds(start, kblk), :], v1_ref[pl.ds(start, kblk), :],
                                  plan, m_ref, l_ref, acc_ref)

        run(block, _diagonal_plans(qblk, kblk, part, False))
        o_ref[...] = _combine_maps(l_ref[:qblk], acc_ref[:qblk], l_ref[qblk:], acc_ref[qblk:], lam).astype(o_ref.dtype)


def _prompt_attention(q, kb16, vb16, lam_params, batch, seq, qblk, kblk):
    nq = seq // qblk
    const = lambda b, h, i: (0, 0)
    return pl.pallas_call(
        functools.partial(_pattn_kernel, qblk=qblk, kblk=kblk),
        grid=(batch, A_HEADS, nq),
        in_specs=[
            pl.BlockSpec((qblk, LANES), lambda b, h, i: (b * nq + i, h)),
            pl.BlockSpec((seq, LANES), lambda b, h, i: (b, h)),
            pl.BlockSpec((seq, LANES), lambda b, h, i: (b, h)),
            pl.BlockSpec((4, A_DQK), const),
            pl.BlockSpec((LANES, 2 * LANES), const),
            pl.BlockSpec((2 * SUBLANES, LANES), const),
        ],
        out_specs=pl.BlockSpec((qblk, LANES), lambda b, h, i: (b * nq + i, h)),
        out_shape=jax.ShapeDtypeStruct((batch * seq, A_HEADS * A_DV), BF16),
        scratch_shapes=[
            pltpu.VMEM((seq, 2 * LANES), BF16),
            pltpu.VMEM((A_DV, seq), BF16),
            pltpu.VMEM((1, 2 * LANES), F32),
            pltpu.VMEM((2 * qblk, LANES), F32),
            pltpu.VMEM((2 * qblk, LANES), F32),
            pltpu.VMEM((2 * qblk, A_DV), F32),
            pltpu.VMEM((SUBLANES, 2 * qblk), F32),
            pltpu.VMEM((SUBLANES, 2 * qblk), F32),
            pltpu.VMEM((A_DV, 2 * qblk), F32),
        ],
        compiler_params=pltpu.CompilerParams(
            dimension_semantics=("parallel", "parallel", "arbitrary"), vmem_limit_bytes=VMEM_LIMIT),
        name="prompt_attn",
    )(q, kb16, vb16, lam_params, _map_sum_matrix(), _map_rows_matrix())


def _partial_softmax(qs, k, v, mask):
    s = lax.dot_general(qs, k, (((1,), (1,)), ((), ())), preferred_element_type=F32)
    s = jnp.where(mask, s, -jnp.inf)
    m = jnp.max(s, axis=1, keepdims=True)
    p = jnp.exp2(s - m)
    return m, jnp.sum(p, axis=1, keepdims=True), jnp.dot(p.astype(BF16), v, preferred_element_type=F32)


def _sattn_kernel(pt_ref, q_ref, kn_ref, vn_ref, lam_ref, ck_hbm, cv_hbm, o_ref, kbuf, vbuf, sems,
                  *, n_pages, group, n_new):
    b = pl.program_id(0)
    n_slots = kbuf.shape[0]

    def page_copies(elem, slot):
        copies = []
        for i in range(n_pages):
            page = pt_ref[elem * n_pages + i]
            copies.append(pltpu.make_async_copy(ck_hbm.at[page], kbuf.at[slot, i], sems.at[0, slot]))
            copies.append(pltpu.make_async_copy(cv_hbm.at[page], vbuf.at[slot, i], sems.at[1, slot]))
        return copies

    @pl.when(b == 0)
    def _():
        for first in range(n_slots - 1):
            for cp in page_copies(first, first):
                cp.start()

    @pl.when(b + n_slots - 1 < pl.num_programs(0))
    def _():
        for cp in page_copies(b + n_slots - 1, (b + n_slots - 1) % n_slots):
            cp.start()

    q = q_ref[...].astype(BF16)
    qs = jnp.concatenate([_stack_maps(q[:, h * LANES:(h + 1) * LANES]) for h in range(A_HEADS)], axis=0)
    n_rows = 2 * n_new * A_HEADS

    def same_head(width):
        row = lax.broadcasted_iota(jnp.int32, (n_rows, width), 0)
        col = lax.broadcasted_iota(jnp.int32, (n_rows, width), 1)
        return row, col, (col % A_HEADS) == (row // (2 * n_new))

    n_kv = n_new * A_HEADS
    pad = jnp.zeros((LANES - n_kv, LANES), F32)
    row, col, ok = same_head(LANES)
    parts = [_partial_softmax(qs, jnp.concatenate([kn_ref[...], pad], axis=0).astype(BF16),
                              jnp.concatenate([vn_ref[...], pad], axis=0).astype(BF16),
                              ok & ((col // A_HEADS) <= (row % n_new)))]

    slot = b % n_slots
    for cp in page_copies(b, slot):
        cp.wait()
    rows_per_page = kbuf.shape[2]
    page_mask = same_head(group * rows_per_page)[2]
    for g in range(0, n_pages, group):
        k = kbuf[slot, g:g + group].reshape(group * rows_per_page, LANES).astype(BF16)
        v = vbuf[slot, g:g + group].reshape(group * rows_per_page, LANES).astype(BF16)
        parts.append(_partial_softmax(qs, k, v, page_mask))
    m = functools.reduce(jnp.maximum, [p[0] for p in parts])
    weights = [jnp.exp2(p[0] - m) for p in parts]
    l = sum(w * p[1] for w, p in zip(weights, parts))
    acc = sum(w * p[2] for w, p in zip(weights, parts))
    lam = _lam_from_ref(lam_ref)
    for h in range(A_HEADS):
        r0 = h * 2 * n_new
        r1 = r0 + n_new
        o_ref[:, h * LANES:(h + 1) * LANES] = _combine_maps(
            l[r0:r1], acc[r0:r1], l[r1:r1 + n_new], acc[r1:r1 + n_new], lam)


def _sample_attention(q, k_new, v_new, cache_k, cache_v, page_table, lam_params, group):
    bs, n_pages = page_table.shape
    n_new = q.shape[0] // bs
    rows_per_page = cache_k.shape[1]
    row_blk = lambda b, pt: (b, 0)
    grid_spec = pltpu.PrefetchScalarGridSpec(
        num_scalar_prefetch=1,
        grid=(bs,),
        in_specs=[
            pl.BlockSpec((n_new, q.shape[1]), row_blk),
            pl.BlockSpec((n_new * A_HEADS, LANES), row_blk),
            pl.BlockSpec((n_new * A_HEADS, LANES), row_blk),
            pl.BlockSpec((4, A_DQK), lambda b, pt: (0, 0)),
            pl.BlockSpec(memory_space=pl.ANY),
            pl.BlockSpec(memory_space=pl.ANY),
        ],
        out_specs=pl.BlockSpec((n_new, q.shape[1]), row_blk),
        scratch_shapes=[
            pltpu.VMEM((PAGE_AHEAD + 1, n_pages, rows_per_page, LANES), F32),
            pltpu.VMEM((PAGE_AHEAD + 1, n_pages, rows_per_page, LANES), F32),
            pltpu.SemaphoreType.DMA((2, PAGE_AHEAD + 1)),
        ],
    )
    return pl.pallas_call(
        functools.partial(_sattn_kernel, n_pages=n_pages, group=group, n_new=n_new),
        grid_spec=grid_spec,
        out_shape=jax.ShapeDtypeStruct(q.shape, F32),
        compiler_params=pltpu.CompilerParams(dimension_semantics=("arbitrary",), vmem_limit_bytes=VMEM_LIMIT),
        name="sample_attn",
    )(page_table.reshape(-1), q, k_new, v_new, lam_params, cache_k, cache_v)


def _hgrn_kernel(q_ref, f_ref, v_ref, s0_ref, tri_ref, o_ref, s_ref, *, rows, seqs, chunks):
    @pl.when(pl.program_id(1) == 0)
    def _():
        s_ref[...] = s0_ref[...]

    n = HGRN_ROWS
    pad = n - rows
    r_idx = lax.broadcasted_iota(jnp.int32, (n, n), 0)
    c_idx = lax.broadcasted_iota(jnp.int32, (n, n), 1)
    causal = c_idx <= r_idx
    tri = tri_ref[...]
    units = [(e, h, c) for e in range(seqs) for h in range(H_HEADS) for c in range(chunks)]

    def rows_of(e, c):
        return slice((e * chunks + c) * rows, (e * chunks + c + 1) * rows)

    def lanes_of(h):
        return slice(h * LANES, (h + 1) * LANES)

    def padded(x, fill):
        return jnp.concatenate([x, jnp.full((pad, LANES), fill, F32)], axis=0) if pad else x

    logs = {}
    for (e, h, c) in units:
        hi, lo = _split_hi_lo(jnp.log2(padded(f_ref[rows_of(e, c), lanes_of(h)], 1.0)))
        logs[e, h, c] = jnp.concatenate([hi, lo], axis=0)
    cums = {u: jnp.dot(tri, logs[u], preferred_element_type=F32) for u in units}

    q_end, k_end, vals, decay, att = {}, {}, {}, {}, {}
    for u in units:
        e, h, c = u
        b = cums[u]
        b_end = b[n - 1:n]
        b_mid = b[n // 2 - 1:n // 2]
        q = padded(q_ref[rows_of(e, c), lanes_of(h)], 0.0)
        k = 1.0 - padded(f_ref[rows_of(e, c), lanes_of(h)], 1.0)
        vals[u] = padded(v_ref[rows_of(e, c), lanes_of(h)], 0.0).astype(BF16)
        q_end[u] = (q * jnp.exp2(b)).astype(BF16)
        k_end[u] = (k * jnp.exp2(b_end - b)).astype(BF16)
        decay[u] = jnp.transpose(jnp.broadcast_to(jnp.exp2(b_end), (8, LANES)))[:, 0:1]
        att[u] = lax.dot_general((q * jnp.exp2(b - b_mid)).astype(BF16), (k * jnp.exp2(b_mid - b)).astype(BF16),
                                 (((1,), (1,)), ((), ())), preferred_element_type=F32)
    kv = {u: lax.dot_general(k_end[u], vals[u], (((0,), (0,)), ((), ())), preferred_element_type=F32)
          for u in units}

    incoming = {}
    for e in range(seqs):
        for h in range(H_HEADS):
            state = s_ref[e, h]
            for c in range(chunks):
                incoming[e, h, c] = state.astype(BF16)
                state = decay[e, h, c] * state + kv[e, h, c]
            s_ref[e, h] = state
    for u in units:
        e, h, c = u
        lhs = jnp.concatenate([jnp.where(causal, att[u], 0.0).astype(BF16), q_end[u]], axis=1)
        o = jnp.dot(lhs, jnp.concatenate([vals[u], incoming[u]], axis=0), preferred_element_type=F32)
        o_ref[rows_of(e, c), lanes_of(h)] = o[:rows].astype(o_ref.dtype)


def _hgrn(hq, f, hi, s0, batch, rows, seqs, chunks):
    total = hq.shape[0]
    steps = total // (batch * rows * chunks)
    n = HGRN_ROWS
    tri = np.tril(np.ones((n, n), np.float32))
    tri = jnp.asarray(np.concatenate([tri, tri], axis=1), BF16)
    assert seqs == 1 or steps == 1
    blk = lambda b, c: (b * steps + c, 0)
    st = lambda b, c: (b, 0, 0, 0)
    return pl.pallas_call(
        functools.partial(_hgrn_kernel, rows=rows, seqs=seqs, chunks=chunks),
        grid=(batch // seqs, steps),
        in_specs=[
            pl.BlockSpec((seqs * chunks * rows, SEG), blk),
            pl.BlockSpec((seqs * chunks * rows, SEG), blk),
            pl.BlockSpec((seqs * chunks * rows, SEG), blk),
            pl.BlockSpec((seqs, H_HEADS, H_DK, H_DV), st),
            pl.BlockSpec((n, 2 * n), lambda b, c: (0, 0)),
        ],
        out_specs=[
            pl.BlockSpec((seqs * chunks * rows, SEG), blk),
            pl.BlockSpec((seqs, H_HEADS, H_DK, H_DV), st),
        ],
        out_shape=[
            jax.ShapeDtypeStruct((total, SEG), BF16),
            jax.ShapeDtypeStruct((batch, H_HEADS, H_DK, H_DV), F32),
        ],
        compiler_params=pltpu.CompilerParams(
            dimension_semantics=("parallel", "arbitrary"), vmem_limit_bytes=VMEM_LIMIT),
        name="hgrn2",
    )(hq, f, hi, s0, tri)


def _merge_kernel(x_ref, oa_ref, ga_ref, oh_ref, gh_ref, sg_ref, hg_ref, w_ref, y_ref):
    def head_norm(o_ref, g_ref, h):
        o = o_ref[:, h * LANES:(h + 1) * LANES].astype(F32)
        return o * lax.rsqrt(jnp.mean(o * o, axis=-1, keepdims=True) + EPS) * g_ref[...]

    a = jnp.concatenate([head_norm(oa_ref, sg_ref, h) for h in range(A_HEADS)], axis=1)
    a = a * (1.0 - LAM_INIT) * ga_ref[...]
    r = jnp.concatenate([head_norm(oh_ref, hg_ref, h) for h in range(H_HEADS)], axis=1)
    r = r * gh_ref[...]
    mix = jnp.concatenate([a, r], axis=1).astype(BF16)
    y_ref[...] = x_ref[...] + jnp.dot(mix, w_ref[...], preferred_element_type=F32)


def _merge(x2d, o_attn, ga, o_hgrn, gh, subln_g, hgrn_norm_g, w_out_bf16, tm):
    rows, d_model = x2d.shape
    row_blk = lambda i: (i, 0)
    const = lambda i: (0, 0)
    return pl.pallas_call(
        _merge_kernel,
        grid=(rows // tm,),
        in_specs=[
            pl.BlockSpec((tm, d_model), row_blk),
            pl.BlockSpec((tm, SEG), row_blk),
            pl.BlockSpec((tm, SEG), row_blk),
            pl.BlockSpec((tm, SEG), row_blk),
            pl.BlockSpec((tm, SEG), row_blk),
            pl.BlockSpec((1, LANES), const),
            pl.BlockSpec((1, LANES), const),
            pl.BlockSpec(w_out_bf16.shape, const, pipeline_mode=pl.Buffered(1)),
        ],
        out_specs=pl.BlockSpec((tm, d_model), row_blk),
        out_shape=jax.ShapeDtypeStruct((rows, d_model), F32),
        compiler_params=pltpu.CompilerParams(dimension_semantics=("parallel",), vmem_limit_bytes=VMEM_LIMIT),
        name="merge",
    )(x2d, o_attn, ga, o_hgrn, gh, subln_g.reshape(1, LANES), hgrn_norm_g.reshape(1, LANES), w_out_bf16)


def kernel(x_prompt, x_sample, cache_k, cache_v, state_hgrn, page_table, norm_g, w_in, q_norm_g, k_norm_g,
           lambda_q1, lambda_k1, lambda_q2, lambda_k2, subln_g, hgrn_lb_logits, hgrn_norm_g, w_out):
    bp, tp, d_model = x_prompt.shape
    bs, ts, _ = x_sample.shape
    depth, n_pool, page_size = cache_k.shape[:3]
    assert depth == 1 and hgrn_lb_logits.shape[0] == 2
    assert (A_DV, H_DK, H_DV, 2 * A_DQK) == (LANES,) * 4
    past_len = page_table.shape[1] * page_size
    tm = 512
    attn_qblk, attn_kblk = 2048, 512
    assert tp % attn_qblk == 0 and tp % (8 * HGRN_ROWS) == 0 and (bp * tp) % tm == 0
    assert (bs * ts) % (2 * tm) == 0 and (bp * tp) % (2 * tm) == 0 and tm % ts == 0 and bs % 8 == 0

    w_in16 = w_in[0].astype(BF16)
    w_out16 = w_out[0].astype(BF16)
    lam_params = jnp.stack([lambda_q1[0], lambda_k1[0], lambda_q2[0], lambda_k2[0]])
    proj = functools.partial(_inproj, norm_g=norm_g[0], w_in_bf16=w_in16, q_norm_g=q_norm_g[0],
                             k_norm_g=k_norm_g[0], lb_logits=hgrn_lb_logits)
    fin = functools.partial(_merge, subln_g=subln_g[0], hgrn_norm_g=hgrn_norm_g[0], w_out_bf16=w_out16, tm=2 * tm)

    xp = x_prompt.reshape(bp * tp, d_model)
    q, k, v, ga, hq, f, hi, gh, k16, v16 = proj(xp, np.arange(tp), tp // tm, tm, True)
    o_attn = _prompt_attention(q, k16, v16, lam_params, bp, tp, attn_qblk, attn_kblk)
    o_hgrn, s_p = _hgrn(hq, f, hi, jnp.zeros((bp, H_HEADS, H_DK, H_DV), F32), bp, HGRN_ROWS, seqs=1, chunks=8)
    y_p = fin(xp, o_attn, ga, o_hgrn, gh)

    xs = x_sample.reshape(bs * ts, d_model)
    pos_s = past_len + np.tile(np.arange(ts), tm // ts)
    sq, sk, sv, sga, shq, sf, shi, sgh = proj(xs, pos_s, 1, tm, False)
    ck = cache_k.reshape(n_pool, page_size * A_HEADS, 2 * A_DQK)
    cv = cache_v.reshape(n_pool, page_size * A_HEADS, A_DV)
    so_attn = _sample_attention(sq, sk, sv, ck, cv, page_table, lam_params, group=page_table.shape[1])
    so_hgrn, s_s = _hgrn(shq, sf, shi, state_hgrn[0], bs, ts, seqs=8, chunks=1)
    y_s = fin(xs, so_attn, sga, so_hgrn, sgh)

    return (y_p.reshape(bp, tp, d_model), y_s.reshape(bs, ts, d_model),
            k.reshape(1, bp, tp, A_HEADS, 2 * A_DQK), v.reshape(1, bp, tp, A_HEADS, A_DV), s_p[None],
            sk.reshape(1, bs, ts, A_HEADS, 2 * A_DQK), sv.reshape(1, bs, ts, A_HEADS, A_DV), s_s[None])
```

```python
import functools
import math

import numpy as np
import jax
import jax.numpy as jnp
from jax import lax
from jax.experimental import pallas as pl
from jax.experimental.pallas import tpu as pltpu

F32 = jnp.float32
BF16 = jnp.bfloat16

LANES = 128
SUBLANES = 8
A_HEADS = 4
A_DQK = 64
A_DV = 128
H_HEADS = 4
H_DK = 128
H_DV = 128
SEG = 512
ROT_DIM = A_DQK // 4
ROPE_THETA = 500000.0
EPS = 1e-6
LAM_INIT = 0.8 - 0.6 * math.exp(-0.3 * 0)
Q_SCALE = A_DQK ** -0.5 * math.log2(math.e)
HGRN_ROWS = 128
SCORE_BOUND_LIMIT = 60.0
PAGE_AHEAD = 2
ATTN_GROUP = 256
KEY_UNROLL = 2
VMEM_LIMIT = 48 * 1024 * 1024


def _sigmoid(x):
    return 1.0 / (1.0 + jnp.exp(-x))


def _split_hi_lo(x):
    hi = x.astype(BF16)
    lo = (x - hi.astype(F32)).astype(BF16)
    return hi, lo


def _lam_from_ref(lam_ref):
    lp = lam_ref[...]
    s1 = jnp.sum(lp[0:1] * lp[1:2], axis=1, keepdims=True)
    s2 = jnp.sum(lp[2:3] * lp[3:4], axis=1, keepdims=True)
    return jnp.exp(s1) - jnp.exp(s2) + LAM_INIT


def _inproj_kernel(x_ref, ng_ref, w_ref, qg_ref, kg_ref, cos_ref, sa_ref, sb_ref, lbl_ref, gm_ref,
                   q_ref, k_ref, v_ref, ga_ref, hq_ref, f_ref, hi_ref, gh_ref, *maybe_bf16_refs):
    x = x_ref[...]
    ms = jnp.mean(x * x, axis=-1, keepdims=True)
    h = (x * lax.rsqrt(ms + EPS) * ng_ref[...]).astype(BF16)

    def proj(seg):
        return jnp.dot(h, w_ref[:, seg * SEG:(seg + 1) * SEG], preferred_element_type=F32)

    cos, sa, sb = cos_ref[...], sa_ref[...], sb_ref[...]

    def norm_rope(y, g_ref, hh):
        pair = y[:, (hh // 2) * 2 * LANES:(hh // 2 + 1) * 2 * LANES]
        gms = jnp.dot((pair * pair).astype(BF16), gm_ref[...], preferred_element_type=F32)
        own = slice((hh % 2) * LANES, (hh % 2 + 1) * LANES)
        yn = pair[:, own] * lax.rsqrt(gms[:, own] + EPS) * g_ref[...]
        return yn * cos + pltpu.roll(yn, LANES - ROT_DIM // 2, 1) * sa + pltpu.roll(yn, ROT_DIM // 2, 1) * sb

    tm = x.shape[0]
    aq = proj(0)
    for hh in range(A_HEADS):
        q_ref[:, hh * LANES:(hh + 1) * LANES] = norm_rope(aq, qg_ref, hh) * Q_SCALE
    ak = proj(1)
    for hh in range(A_HEADS):
        kh = norm_rope(ak, kg_ref, hh)
        k_ref[pl.ds(hh, tm, stride=A_HEADS), :] = kh
        if maybe_bf16_refs:
            maybe_bf16_refs[0][:, hh * LANES:(hh + 1) * LANES] = kh.astype(BF16)
    av = proj(2)
    for hh in range(A_HEADS):
        v_ref[pl.ds(hh, tm, stride=A_HEADS), :] = av[:, hh * LANES:(hh + 1) * LANES]
    if maybe_bf16_refs:
        maybe_bf16_refs[1][...] = av.astype(BF16)
    ag = proj(3)
    ga_ref[...] = (ag * _sigmoid(ag)).astype(ga_ref.dtype)
    hq = proj(4)
    hq_ref[...] = hq * _sigmoid(hq)
    hf = proj(5)
    lbl = lbl_ref[...]
    l0, l1 = lbl[0:1], lbl[1:2]
    mx = jnp.maximum(l0, l1)
    e0, e1 = jnp.exp(l0 - mx), jnp.exp(l1 - mx)
    lb = e0 / (e0 + e1)
    f_ref[...] = lb + (1.0 - lb) * _sigmoid(hf)
    hi_ref[...] = proj(6)
    hg = proj(7)
    gh_ref[...] = (hg * _sigmoid(hg)).astype(gh_ref.dtype)


def _rope_tables(positions):
    half = ROT_DIM // 2
    inv = np.float32(ROPE_THETA) ** (-np.arange(0, ROT_DIM, 2, dtype=np.float32) / np.float32(ROT_DIM))
    lane = np.arange(LANES) % A_DQK
    ang = positions.astype(np.float32)[:, None] * inv[lane % half][None, :].astype(np.float32)
    cos = np.where((lane < ROT_DIM)[None, :], np.cos(ang.astype(np.float64)), 1.0)
    sin = np.sin(ang.astype(np.float64))
    a = np.where((lane < half)[None, :], -sin, 0.0)
    b = np.where(((lane >= half) & (lane < ROT_DIM))[None, :], sin, 0.0)
    return tuple(jnp.asarray(t, F32) for t in (cos, a, b))


def _group_mean_matrix():
    g = np.zeros((2 * LANES, 2 * LANES), np.float32)
    for s in range(0, 2 * LANES, A_DQK):
        g[s:s + A_DQK, s:s + A_DQK] = 1.0 / A_DQK
    return jnp.asarray(g, BF16)


def _inproj(x2d, pos, n_pos_blocks, tm, with_bf16, norm_g, w_in_bf16, q_norm_g, k_norm_g, lb_logits):
    rows, d_model = x2d.shape
    d_in = w_in_bf16.shape[1]
    cos, sa, sb = _rope_tables(pos)
    qg = jnp.concatenate([q_norm_g, q_norm_g]).reshape(1, LANES)
    kg = jnp.concatenate([k_norm_g, k_norm_g]).reshape(1, LANES)
    const = lambda i: (0, 0)
    row_blk = lambda i: (i, 0)
    pos_blk = lambda i: (i % n_pos_blocks, 0)
    out_f32 = jax.ShapeDtypeStruct((rows, SEG), F32)
    out_shape = [out_f32] * 8
    out_specs = [pl.BlockSpec((tm, SEG), row_blk)] * 8
    for i in (3, 7):
        out_shape[i] = jax.ShapeDtypeStruct((rows, SEG), BF16)
    for i in (1, 2):
        out_shape[i] = jax.ShapeDtypeStruct((rows * A_HEADS, LANES), F32)
        out_specs[i] = pl.BlockSpec((tm * A_HEADS, LANES), row_blk)
    if with_bf16:
        out_shape += [jax.ShapeDtypeStruct((rows, SEG), BF16)] * 2
        out_specs += [pl.BlockSpec((tm, SEG), row_blk)] * 2
    return pl.pallas_call(
        _inproj_kernel,
        grid=(rows // tm,),
        in_specs=[
            pl.BlockSpec((tm, d_model), row_blk),
            pl.BlockSpec((1, d_model), const),
            pl.BlockSpec((d_model, d_in), const, pipeline_mode=pl.Buffered(1)),
            pl.BlockSpec((1, LANES), const),
            pl.BlockSpec((1, LANES), const),
            pl.BlockSpec((tm, LANES), pos_blk),
            pl.BlockSpec((tm, LANES), pos_blk),
            pl.BlockSpec((tm, LANES), pos_blk),
            pl.BlockSpec((2, SEG), const),
            pl.BlockSpec((2 * LANES, 2 * LANES), const),
        ],
        out_specs=out_specs,
        out_shape=out_shape,
        compiler_params=pltpu.CompilerParams(dimension_semantics=("parallel",), vmem_limit_bytes=VMEM_LIMIT),
        name="inproj",
    )(x2d, norm_g.reshape(1, d_model), w_in_bf16, qg, kg, cos, sa, sb, lb_logits, _group_mean_matrix())


def _stack_maps(q_bf16):
    lane = lax.broadcasted_iota(jnp.int32, q_bf16.shape, 1)
    zero = jnp.zeros_like(q_bf16)
    return jnp.concatenate([jnp.where(lane < A_DQK, q_bf16, zero), jnp.where(lane >= A_DQK, q_bf16, zero)], axis=0)


def _softmax_block_update(qs_parts, k, v1, plan, m_ref, l_ref, acc_ref):
    width = k.shape[0]
    n = qs_parts[0].shape[0]
    scores = [lax.dot_general(qs_parts[i], k, (((1,), (1,)), ((), ())), preferred_element_type=F32)
              for i, _ in plan]
    probs, alphas, m_news = [], [], []
    for (i, mask), s in zip(plan, scores):
        if mask is not None:
            s = jnp.where(mask, s, -jnp.inf)
        m_prev = m_ref[i * n:(i + 1) * n]
        m_new = jnp.maximum(m_prev, jnp.max(s, axis=1, keepdims=True))
        alphas.append(jnp.exp2(m_prev - m_new))
        probs.append(jnp.exp2(s - jnp.concatenate([m_new] * (width // LANES), axis=1)).astype(BF16))
        m_news.append(m_new)
    for (i, _), p, alpha, m_new in zip(plan, probs, alphas, m_news):
        rows = slice(i * n, (i + 1) * n)
        pv = jnp.dot(p, v1, preferred_element_type=F32)
        acc_ref[rows] = alpha * acc_ref[rows] + pv[:, :A_DV]
        l_ref[rows] = alpha * l_ref[rows] + pv[:, A_DV:]
        m_ref[rows] = m_new


def _combine_maps(l0, acc0, l1, acc1, lam):
    return acc0 * (1.0 / l0) - lam * (acc1 * (1.0 / l1))


def _map_sum_matrix():
    g = np.zeros((LANES, 2 * LANES), np.float32)
    g[:A_DQK, :LANES] = 1.0
    g[A_DQK:, LANES:] = 1.0
    return jnp.asarray(g, BF16)


def _map_rows_matrix():
    g = np.zeros((2 * SUBLANES, LANES), np.float32)
    g[:SUBLANES, :A_DQK] = 1.0
    g[SUBLANES:, A_DQK:] = 1.0
    return jnp.asarray(g, BF16)


def _transposed_block_update(qs_parts, k, vt, plan, ref_ref, sum_ref, acc_ref):
    n = qs_parts[0].shape[0]
    scores = [lax.dot_general(k, qs_parts[i], (((1,), (1,)), ((), ())), preferred_element_type=F32)
              for i, _ in plan]
    for (i, mask), st in zip(plan, scores):
        cols = slice(i * n, (i + 1) * n)
        if mask is not None:
            st = jnp.where(mask, st, -jnp.inf)
        p = jnp.exp2(st - ref_ref[0:1, cols])
        sum_ref[:, cols] = sum_ref[:, cols] + jnp.sum(p.reshape(p.shape[0] // SUBLANES, SUBLANES, n), axis=0)
        acc_ref[:, cols] = acc_ref[:, cols] + jnp.dot(vt, p.astype(BF16), preferred_element_type=F32)


def _diagonal_plans(qblk, kblk, part, keys_on_rows):
    per_map = qblk // part
    shape = (kblk, part) if keys_on_rows else (part, kblk)
    key = lax.broadcasted_iota(jnp.int32, shape, 0 if keys_on_rows else 1)
    qry = lax.broadcasted_iota(jnp.int32, shape, 1 if keys_on_rows else 0)
    plans = []
    for j in range(qblk // kblk):
        plan = []
        for c in range(2 * per_map):
            q0 = (c % per_map) * part
            if (j + 1) * kblk - 1 <= q0:
                plan.append((c, None))
            elif j * kblk <= q0 + part - 1:
                plan.append((c, key + j * kblk <= qry + q0))
        plans.append(plan)
    return plans


def _pattn_kernel(q_ref, k_ref, v_ref, lam_ref, ms_ref, mr_ref, o_ref,
                  v1_ref, vt_ref, kn_ref, m_ref, l_ref, acc_ref, mt_ref, lt_ref, acct_ref, *, qblk, kblk):
    qi = pl.program_id(2)
    seq = k_ref.shape[0]
    ratio = qblk // kblk

    @pl.when(qi == 0)
    def _():
        for c in range(seq // kblk):
            vt_ref[:, c * kblk:(c + 1) * kblk] = jnp.transpose(v_ref[c * kblk:(c + 1) * kblk, :])
        k = k_ref[...]
        kn_ref[...] = jnp.max(jnp.dot(k * k, ms_ref[...], preferred_element_type=F32), axis=0, keepdims=True)

    q = q_ref[...].astype(BF16)
    qs = _stack_maps(q)
    part = ATTN_GROUP
    n_groups = 2 * qblk // part
    qs_parts = [qs[i * part:(i + 1) * part] for i in range(n_groups)]
    every_group = [(i, None) for i in range(n_groups)]

    qn = lax.dot_general(mr_ref[...], q * q, (((1,), (1,)), ((), ())), preferred_element_type=F32)
    kn = kn_ref[...]
    kn0 = jnp.concatenate([kn[:, :LANES]] * (qblk // LANES), axis=1)
    kn1 = jnp.concatenate([kn[:, LANES:]] * (qblk // LANES), axis=1)
    bound = jnp.concatenate([jnp.sqrt(qn[:SUBLANES] * kn0), jnp.sqrt(qn[SUBLANES:] * kn1)], axis=1) * 1.02
    mt_ref[...] = bound
    bounded = jnp.max(bound) <= SCORE_BOUND_LIMIT

    def run(block, diag_plans):
        def body(t, carry):
            for u in range(KEY_UNROLL):
                block(KEY_UNROLL * t + u, every_group)
            return carry

        lax.fori_loop(0, qi * (ratio // KEY_UNROLL), body, 0)
        for j, plan in enumerate(diag_plans):
            block(ratio * qi + j, plan)

    lam = _lam_from_ref(lam_ref)

    @pl.when(bounded)
    def _():
        acct_ref[...] = jnp.zeros(acct_ref.shape, F32)
        lt_ref[...] = jnp.zeros(lt_ref.shape, F32)

        def block(kb, plan):
            start = pl.multiple_of(kb * kblk, kblk)
            _transposed_block_update(qs_parts, k_ref[pl.ds(start, kblk), :], vt_ref[:, pl.ds(start, kblk)],
                                     plan, mt_ref, lt_ref, acct_ref)

        run(block, _diagonal_plans(qblk, kblk, part, True))
        acc = acct_ref[...]
        l = jnp.sum(lt_ref[...], axis=0, keepdims=True)
        out_t = _combine_maps(l[:, :qblk], acc[:, :qblk], l[:, qblk:], acc[:, qblk:], lam)
        o_ref[...] = jnp.transpose(out_t).astype(o_ref.dtype)

    @pl.when(jnp.logical_not(bounded))
    def _():
        m_ref[...] = jnp.full(m_ref.shape, -jnp.inf, F32)
        l_ref[...] = jnp.zeros(l_ref.shape, F32)
        acc_ref[...] = jnp.zeros(acc_ref.shape, F32)
        v1_ref[:, :A_DV] = v_ref[...]
        v1_ref[:, A_DV:] = jnp.ones((seq, LANES), BF16)

        def block(kb, plan):
            start = pl.multiple_of(kb * kblk, kblk)
            _softmax_block_update(qs_parts, k_ref[pl.ds(start, kblk), :], v1_ref[pl.ds(start, kblk), :],
                                  plan, m_ref, l_ref, acc_ref)

        run(block, _diagonal_plans(qblk, kblk, part, False))
        o_ref[...] = _combine_maps(l_ref[:qblk], acc_ref[:qblk], l_ref[qblk:], acc_ref[qblk:], lam).astype(o_ref.dtype)


def _prompt_attention(q, kb16, vb16, lam_params, batch, seq, qblk, kblk):
    nq = seq // qblk
    const = lambda b, h, i: (0, 0)
    return pl.pallas_call(
        functools.partial(_pattn_kernel, qblk=qblk, kblk=kblk),
        grid=(batch, A_HEADS, nq),
        in_specs=[
            pl.BlockSpec((qblk, LANES), lambda b, h, i: (b * nq + i, h)),
            pl.BlockSpec((seq, LANES), lambda b, h, i: (b, h)),
            pl.BlockSpec((seq, LANES), lambda b, h, i: (b, h)),
            pl.BlockSpec((4, A_DQK), const),
            pl.BlockSpec((LANES, 2 * LANES), const),
            pl.BlockSpec((2 * SUBLANES, LANES), const),
        ],
        out_specs=pl.BlockSpec((qblk, LANES), lambda b, h, i: (b * nq + i, h)),
        out_shape=jax.ShapeDtypeStruct((batch * seq, A_HEADS * A_DV), BF16),
        scratch_shapes=[
            pltpu.VMEM((seq, 2 * LANES), BF16),
            pltpu.VMEM((A_DV, seq), BF16),
            pltpu.VMEM((1, 2 * LANES), F32),
            pltpu.VMEM((2 * qblk, LANES), F32),
            pltpu.VMEM((2 * qblk, LANES), F32),
            pltpu.VMEM((2 * qblk, A_DV), F32),
            pltpu.VMEM((SUBLANES, 2 * qblk), F32),
            pltpu.VMEM((SUBLANES, 2 * qblk), F32),
            pltpu.VMEM((A_DV, 2 * qblk), F32),
        ],
        compiler_params=pltpu.CompilerParams(
            dimension_semantics=("parallel", "parallel", "arbitrary"), vmem_limit_bytes=VMEM_LIMIT),
        name="prompt_attn",
    )(q, kb16, vb16, lam_params, _map_sum_matrix(), _map_rows_matrix())


def _partial_softmax(qs, k, v, mask):
    s = lax.dot_general(qs, k, (((1,), (1,)), ((), ())), preferred_element_type=F32)
    s = jnp.where(mask, s, -jnp.inf)
    m = jnp.max(s, axis=1, keepdims=True)
    p = jnp.exp2(s - m)
    return m, jnp.sum(p, axis=1, keepdims=True), jnp.dot(p.astype(BF16), v, preferred_element_type=F32)


def _sattn_kernel(pt_ref, q_ref, kn_ref, vn_ref, lam_ref, ck_hbm, cv_hbm, o_ref, kbuf, vbuf, sems,
                  *, n_pages, group, n_new):
    b = pl.program_id(0)
    n_slots = kbuf.shape[0]

    def page_copies(elem, slot):
        copies = []
        for i in range(n_pages):
            page = pt_ref[elem * n_pages + i]
            copies.append(pltpu.make_async_copy(ck_hbm.at[page], kbuf.at[slot, i], sems.at[0, slot]))
            copies.append(pltpu.make_async_copy(cv_hbm.at[page], vbuf.at[slot, i], sems.at[1, slot]))
        return copies

    @pl.when(b == 0)
    def _():
        for first in range(n_slots - 1):
            for cp in page_copies(first, first):
                cp.start()

    @pl.when(b + n_slots - 1 < pl.num_programs(0))
    def _():
        for cp in page_copies(b + n_slots - 1, (b + n_slots - 1) % n_slots):
            cp.start()

    q = q_ref[...].astype(BF16)
    qs = jnp.concatenate([_stack_maps(q[:, h * LANES:(h + 1) * LANES]) for h in range(A_HEADS)], axis=0)
    n_rows = 2 * n_new * A_HEADS

    def same_head(width):
        row = lax.broadcasted_iota(jnp.int32, (n_rows, width), 0)
        col = lax.broadcasted_iota(jnp.int32, (n_rows, width), 1)
        return row, col, (col % A_HEADS) == (row // (2 * n_new))

    n_kv = n_new * A_HEADS
    pad = jnp.zeros((LANES - n_kv, LANES), F32)
    row, col, ok = same_head(LANES)
    parts = [_partial_softmax(qs, jnp.concatenate([kn_ref[...], pad], axis=0).astype(BF16),
                              jnp.concatenate([vn_ref[...], pad], axis=0).astype(BF16),
                              ok & ((col // A_HEADS) <= (row % n_new)))]

    slot = b % n_slots
    for cp in page_copies(b, slot):
        cp.wait()
    rows_per_page = kbuf.shape[2]
    page_mask = same_head(group * rows_per_page)[2]
    for g in range(0, n_pages, group):
        k = kbuf[slot, g:g + group].reshape(group * rows_per_page, LANES).astype(BF16)
        v = vbuf[slot, g:g + group].reshape(group * rows_per_page, LANES).astype(BF16)
        parts.append(_partial_softmax(qs, k, v, page_mask))
    m = functools.reduce(jnp.maximum, [p[0] for p in parts])
    weights = [jnp.exp2(p[0] - m) for p in parts]
    l = sum(w * p[1] for w, p in zip(weights, parts))
    acc = sum(w * p[2] for w, p in zip(weights, parts))
    lam = _lam_from_ref(lam_ref)
    for h in range(A_HEADS):
        r0 = h * 2 * n_new
        r1 = r0 + n_new
        o_ref[:, h * LANES:(h + 1) * LANES] = _combine_maps(
            l[r0:r1], acc[r0:r1], l[r1:r1 + n_new], acc[r1:r1 + n_new], lam)


def _sample_attention(q, k_new, v_new, cache_k, cache_v, page_table, lam_params, group):
    bs, n_pages = page_table.shape
    n_new = q.shape[0] // bs
    rows_per_page = cache_k.shape[1]
    row_blk = lambda b, pt: (b, 0)
    grid_spec = pltpu.PrefetchScalarGridSpec(
        num_scalar_prefetch=1,
        grid=(bs,),
        in_specs=[
            pl.BlockSpec((n_new, q.shape[1]), row_blk),
            pl.BlockSpec((n_new * A_HEADS, LANES), row_blk),
            pl.BlockSpec((n_new * A_HEADS, LANES), row_blk),
            pl.BlockSpec((4, A_DQK), lambda b, pt: (0, 0)),
            pl.BlockSpec(memory_space=pl.ANY),
            pl.BlockSpec(memory_space=pl.ANY),
        ],
        out_specs=pl.BlockSpec((n_new, q.shape[1]), row_blk),
        scratch_shapes=[
            pltpu.VMEM((PAGE_AHEAD + 1, n_pages, rows_per_page, LANES), F32),
            pltpu.VMEM((PAGE_AHEAD + 1, n_pages, rows_per_page, LANES), F32),
            pltpu.SemaphoreType.DMA((2, PAGE_AHEAD + 1)),
        ],
    )
    return pl.pallas_call(
        functools.partial(_sattn_kernel, n_pages=n_pages, group=group, n_new=n_new),
        grid_spec=grid_spec,
        out_shape=jax.ShapeDtypeStruct(q.shape, F32),
        compiler_params=pltpu.CompilerParams(dimension_semantics=("arbitrary",), vmem_limit_bytes=VMEM_LIMIT),
        name="sample_attn",
    )(page_table.reshape(-1), q, k_new, v_new, lam_params, cache_k, cache_v)


def _hgrn_kernel(q_ref, f_ref, v_ref, s0_ref, tri_ref, o_ref, s_ref, *, rows, seqs, chunks):
    @pl.when(pl.program_id(1) == 0)
    def _():
        s_ref[...] = s0_ref[...]

    n = HGRN_ROWS
    pad = n - rows
    r_idx = lax.broadcasted_iota(jnp.int32, (n, n), 0)
    c_idx = lax.broadcasted_iota(jnp.int32, (n, n), 1)
    causal = c_idx <= r_idx
    tri = tri_ref[...]
    units = [(e, h, c) for e in range(seqs) for h in range(H_HEADS) for c in range(chunks)]

    def rows_of(e, c):
        return slice((e * chunks + c) * rows, (e * chunks + c + 1) * rows)

    def lanes_of(h):
        return slice(h * LANES, (h + 1) * LANES)

    def padded(x, fill):
        return jnp.concatenate([x, jnp.full((pad, LANES), fill, F32)], axis=0) if pad else x

    logs = {}
    for (e, h, c) in units:
        hi, lo = _split_hi_lo(jnp.log2(padded(f_ref[rows_of(e, c), lanes_of(h)], 1.0)))
        logs[e, h, c] = jnp.concatenate([hi, lo], axis=0)
    cums = {u: jnp.dot(tri, logs[u], preferred_element_type=F32) for u in units}

    q_end, k_end, vals, decay, att = {}, {}, {}, {}, {}
    for u in units:
        e, h, c = u
        b = cums[u]
        b_end = b[n - 1:n]
        b_mid = b[n // 2 - 1:n // 2]
        q = padded(q_ref[rows_of(e, c), lanes_of(h)], 0.0)
        k = 1.0 - padded(f_ref[rows_of(e, c), lanes_of(h)], 1.0)
        vals[u] = padded(v_ref[rows_of(e, c), lanes_of(h)], 0.0).astype(BF16)
        q_end[u] = (q * jnp.exp2(b)).astype(BF16)
        k_end[u] = (k * jnp.exp2(b_end - b)).astype(BF16)
        decay[u] = jnp.transpose(jnp.broadcast_to(jnp.exp2(b_end), (8, LANES)))[:, 0:1]
        att[u] = lax.dot_general((q * jnp.exp2(b - b_mid)).astype(BF16), (k * jnp.exp2(b_mid - b)).astype(BF16),
                                 (((1,), (1,)), ((), ())), preferred_element_type=F32)
    kv = {u: lax.dot_general(k_end[u], vals[u], (((0,), (0,)), ((), ())), preferred_element_type=F32)
          for u in units}

    incoming = {}
    for e in range(seqs):
        for h in range(H_HEADS):
            state = s_ref[e, h]
            for c in range(chunks):
                incoming[e, h, c] = state.astype(BF16)
                state = decay[e, h, c] * state + kv[e, h, c]
            s_ref[e, h] = state
    for u in units:
        e, h, c = u
        lhs = jnp.concatenate([jnp.where(causal, att[u], 0.0).astype(BF16), q_end[u]], axis=1)
        o = jnp.dot(lhs, jnp.concatenate([vals[u], incoming[u]], axis=0), preferred_element_type=F32)
        o_ref[rows_of(e, c), lanes_of(h)] = o[:rows].astype(o_ref.dtype)


def _hgrn(hq, f, hi, s0, batch, rows, seqs, chunks):
    total = hq.shape[0]
    steps = total // (batch * rows * chunks)
    n = HGRN_ROWS
    tri = np.tril(np.ones((n, n), np.float32))
    tri = jnp.asarray(np.concatenate([tri, tri], axis=1), BF16)
    assert seqs == 1 or steps == 1
    blk = lambda b, c: (b * steps + c, 0)
    st = lambda b, c: (b, 0, 0, 0)
    return pl.pallas_call(
        functools.partial(_hgrn_kernel, rows=rows, seqs=seqs, chunks=chunks),
        grid=(batch // seqs, steps),
        in_specs=[
            pl.BlockSpec((seqs * chunks * rows, SEG), blk),
            pl.BlockSpec((seqs * chunks * rows, SEG), blk),
            pl.BlockSpec((seqs * chunks * rows, SEG), blk),
            pl.BlockSpec((seqs, H_HEADS, H_DK, H_DV), st),
            pl.BlockSpec((n, 2 * n), lambda b, c: (0, 0)),
        ],
        out_specs=[
            pl.BlockSpec((seqs * chunks * rows, SEG), blk),
            pl.BlockSpec((seqs, H_HEADS, H_DK, H_DV), st),
        ],
        out_shape=[
            jax.ShapeDtypeStruct((total, SEG), BF16),
            jax.ShapeDtypeStruct((batch, H_HEADS, H_DK, H_DV), F32),
        ],
        compiler_params=pltpu.CompilerParams(
            dimension_semantics=("parallel", "arbitrary"), vmem_limit_bytes=VMEM_LIMIT),
        name="hgrn2",
    )(hq, f, hi, s0, tri)


def _merge_kernel(x_ref, oa_ref, ga_ref, oh_ref, gh_ref, sg_ref, hg_ref, w_ref, y_ref):
    def head_norm(o_ref, g_ref, h):
        o = o_ref[:, h * LANES:(h + 1) * LANES].astype(F32)
        return o * lax.rsqrt(jnp.mean(o * o, axis=-1, keepdims=True) + EPS) * g_ref[...]

    a = jnp.concatenate([head_norm(oa_ref, sg_ref, h) for h in range(A_HEADS)], axis=1)
    a = a * (1.0 - LAM_INIT) * ga_ref[...]
    r = jnp.concatenate([head_norm(oh_ref, hg_ref, h) for h in range(H_HEADS)], axis=1)
    r = r * gh_ref[...]
    mix = jnp.concatenate([a, r], axis=1).astype(BF16)
    y_ref[...] = x_ref[...] + jnp.dot(mix, w_ref[...], preferred_element_type=F32)


def _merge(x2d, o_attn, ga, o_hgrn, gh, subln_g, hgrn_norm_g, w_out_bf16, tm):
    rows, d_model = x2d.shape
    row_blk = lambda i: (i, 0)
    const = lambda i: (0, 0)
    return pl.pallas_call(
        _merge_kernel,
        grid=(rows // tm,),
        in_specs=[
            pl.BlockSpec((tm, d_model), row_blk),
            pl.BlockSpec((tm, SEG), row_blk),
            pl.BlockSpec((tm, SEG), row_blk),
            pl.BlockSpec((tm, SEG), row_blk),
            pl.BlockSpec((tm, SEG), row_blk),
            pl.BlockSpec((1, LANES), const),
            pl.BlockSpec((1, LANES), const),
            pl.BlockSpec(w_out_bf16.shape, const, pipeline_mode=pl.Buffered(1)),
        ],
        out_specs=pl.BlockSpec((tm, d_model), row_blk),
        out_shape=jax.ShapeDtypeStruct((rows, d_model), F32),
        compiler_params=pltpu.CompilerParams(dimension_semantics=("parallel",), vmem_limit_bytes=VMEM_LIMIT),
        name="merge",
    )(x2d, o_attn, ga, o_hgrn, gh, subln_g.reshape(1, LANES), hgrn_norm_g.reshape(1, LANES), w_out_bf16)


def kernel(x_prompt, x_sample, cache_k, cache_v, state_hgrn, page_table, norm_g, w_in, q_norm_g, k_norm_g,
           lambda_q1, lambda_k1, lambda_q2, lambda_k2, subln_g, hgrn_lb_logits, hgrn_norm_g, w_out):
    bp, tp, d_model = x_prompt.shape
    bs, ts, _ = x_sample.shape
    depth, n_pool, page_size = cache_k.shape[:3]
    assert depth == 1 and hgrn_lb_logits.shape[0] == 2
    assert (A_DV, H_DK, H_DV, 2 * A_DQK) == (LANES,) * 4
    past_len = page_table.shape[1] * page_size
    tm = 512
    attn_qblk, attn_kblk = 2048, 512
    assert tp % attn_qblk == 0 and tp % (8 * HGRN_ROWS) == 0 and (bp * tp) % tm == 0
    assert (bs * ts) % (2 * tm) == 0 and (bp * tp) % (2 * tm) == 0 and tm % ts == 0 and bs % 8 == 0

    w_in16 = w_in[0].astype(BF16)
    w_out16 = w_out[0].astype(BF16)
    lam_params = jnp.stack([lambda_q1[0], lambda_k1[0], lambda_q2[0], lambda_k2[0]])
    proj = functools.partial(_inproj, norm_g=norm_g[0], w_in_bf16=w_in16, q_norm_g=q_norm_g[0],
                             k_norm_g=k_norm_g[0], lb_logits=hgrn_lb_logits)
    fin = functools.partial(_merge, subln_g=subln_g[0], hgrn_norm_g=hgrn_norm_g[0], w_out_bf16=w_out16, tm=2 * tm)

    xp = x_prompt.reshape(bp * tp, d_model)
    q, k, v, ga, hq, f, hi, gh, k16, v16 = proj(xp, np.arange(tp), tp // tm, tm, True)
    o_attn = _prompt_attention(q, k16, v16, lam_params, bp, tp, attn_qblk, attn_kblk)
    o_hgrn, s_p = _hgrn(hq, f, hi, jnp.zeros((bp, H_HEADS, H_DK, H_DV), F32), bp, HGRN_ROWS, seqs=1, chunks=8)
    y_p = fin(xp, o_attn, ga, o_hgrn, gh)

    xs = x_sample.reshape(bs * ts, d_model)
    pos_s = past_len + np.tile(np.arange(ts), tm // ts)
    sq, sk, sv, sga, shq, sf, shi, sgh = proj(xs, pos_s, 1, tm, False)
    ck = cache_k.reshape(n_pool, page_size * A_HEADS, 2 * A_DQK)
    cv = cache_v.reshape(n_pool, page_size * A_HEADS, A_DV)
    so_attn = _sample_attention(sq, sk, sv, ck, cv, page_table, lam_params, group=page_table.shape[1])
    so_hgrn, s_s = _hgrn(shq, sf, shi, state_hgrn[0], bs, ts, seqs=8, chunks=1)
    y_s = fin(xs, so_attn, sga, so_hgrn, sgh)

    return (y_p.reshape(bp, tp, d_model), y_s.reshape(bs, ts, d_model),
            k.reshape(1, bp, tp, A_HEADS, 2 * A_DQK), v.reshape(1, bp, tp, A_HEADS, A_DV), s_p[None],
            sk.reshape(1, bs, ts, A_HEADS, 2 * A_DQK), sv.reshape(1, bs, ts, A_HEADS, A_DV), s_s[None])
```
